```python
import jax, jax.numpy as jnp
from jax import lax
import numpy as np

D_MODEL = 2048
BATCH = 8
SEQ = 8192
DEPTH = 4

N_MIXERS = 3
CONV_W = 3
SC_DIM = D_MODEL
CHUNK = 128
SG_DIM = D_MODEL
SG_GROUPS = 8
SB_HEADS = 16
SB_HEAD_DIM = D_MODEL // SB_HEADS
Q_BLOCK = 128
D_FF = 5632
EPS = 1e-6
N_A = len(range(0, DEPTH, N_MIXERS))
N_B = len(range(1, DEPTH, N_MIXERS))
N_C = len(range(2, DEPTH, N_MIXERS))

kernel_name = "hybrid_shortconv_sgu_stickbreak_trunk"


def rmsnorm(x, g):
    xf = x.astype(jnp.float32)
    y = xf * lax.rsqrt(jnp.mean(xf * xf, axis=-1, keepdims=True) + EPS)
    return (y * g.astype(jnp.float32)).astype(x.dtype)


def layernorm(x, g, b):
    xf = x.astype(jnp.float32)
    mu = jnp.mean(xf, axis=-1, keepdims=True)
    var = jnp.mean(jnp.square(xf - mu), axis=-1, keepdims=True)
    y = (xf - mu) * lax.rsqrt(var + EPS)
    return (y * g.astype(jnp.float32) + b.astype(jnp.float32)).astype(x.dtype)


def causal_dwconv(x, w, b=None):
    s = x.shape[1]
    xp = jnp.pad(x, ((0, 0), (CONV_W - 1, 0), (0, 0)))
    y = w[0] * xp[:, 0:s]
    for k in range(1, CONV_W):
        y = y + w[k] * xp[:, k:k + s]
    if b is not None:
        y = y + b
    return y


def short_conv_mixer(x, w_in, w_conv, w_out):
    gb, gc, h = jnp.split(x @ w_in, 3, axis=-1)
    y = gb * causal_dwconv(gc * h, w_conv)
    return y @ w_out


def chunked_sgu_mixer(x, w_in, ln_g, ln_b, w_s, b_s, w_out):
    bn, s, _ = x.shape
    h = jax.nn.gelu(x @ w_in, approximate=False)
    u, v = jnp.split(h, 2, axis=-1)
    v = layernorm(v, ln_g, ln_b)
    v = v.reshape(bn, s // CHUNK, CHUNK, SG_GROUPS, SG_DIM // SG_GROUPS)
    mask = jnp.tril(jnp.ones((CHUNK, CHUNK), dtype=bool))
    ws = jnp.where(mask[None], w_s, 0.0).astype(v.dtype)
    mixed = jnp.einsum('gts,bnsgc->bntgc', ws, v) + b_s.T[None, None, :, :, None].astype(v.dtype)
    y = u * mixed.reshape(bn, s, SG_DIM)
    return y @ w_out


def stick_breaking_mixer(x, w_qkv, w_out):
    bn, s, _ = x.shape
    qkv = (x @ w_qkv).reshape(bn, s, 3, SB_HEADS, SB_HEAD_DIM)
    q = qkv[:, :, 0].transpose(0, 2, 1, 3)
    k = qkv[:, :, 1].transpose(0, 2, 1, 3)
    v = qkv[:, :, 2].transpose(0, 2, 1, 3)
    nb = s // Q_BLOCK
    q_blocks = q.reshape(bn, SB_HEADS, nb, Q_BLOCK, SB_HEAD_DIM).transpose(2, 0, 1, 3, 4)
    kpos = jnp.arange(s)
    scale = SB_HEAD_DIM ** -0.5

    def block(args):
        qb, i = args
        qpos = i * Q_BLOCK + jnp.arange(Q_BLOCK)
        mask = kpos[None, :] < qpos[:, None]
        z = jnp.einsum('bhqd,bhkd->bhqk', qb, k).astype(jnp.float32) * scale
        log_1mb = jnp.where(mask, jax.nn.log_sigmoid(-z), 0.0)
        acc = lax.cumsum(log_1mb, axis=3, reverse=True) - log_1mb
        a = jnp.where(mask, jnp.exp(jax.nn.log_sigmoid(z) + acc), 0.0)
        return jnp.einsum('bhqk,bhkd->bhqd', a.astype(v.dtype), v)

    o = lax.map(block, (q_blocks, jnp.arange(nb)))
    o = o.transpose(1, 0, 3, 2, 4).reshape(bn, s, SB_HEADS * SB_HEAD_DIM)
    return o @ w_out


def conv_glu_ffn(x, w_up, conv_w, conv_b, w_down):
    h = causal_dwconv(x @ w_up, conv_w, conv_b)
    g, val = jnp.split(h, 2, axis=-1)
    return (jax.nn.silu(g) * val) @ w_down


def _fwd_setup_inputs(seed: int = 0) -> dict:
    key = jax.random.key(seed)
    ks = jax.random.split(key, 20)
    f32 = jnp.float32

    def w(k, shape, fan_in):
        return jax.random.normal(k, shape, f32) * (fan_in ** -0.5)

    def gain(k, shape):
        return 1.0 + 0.05 * jax.random.normal(k, shape, f32)

    return {
        "x": jax.random.normal(ks[0], (BATCH, SEQ, D_MODEL), f32),
        "norm_mix_pre": gain(ks[1], (DEPTH, D_MODEL)),
        "norm_mix_post": gain(ks[2], (DEPTH, D_MODEL)),
        "norm_ffn_pre": gain(ks[3], (DEPTH, D_MODEL)),
        "norm_ffn_post": gain(ks[4], (DEPTH, D_MODEL)),
        "sc_w_in": w(ks[5], (N_A, D_MODEL, 3 * SC_DIM), D_MODEL),
        "sc_conv_w": w(ks[6], (N_A, CONV_W, SC_DIM), CONV_W),
        "sc_w_out": w(ks[7], (N_A, SC_DIM, D_MODEL), SC_DIM),
        "sg_w_in": w(ks[8], (N_B, D_MODEL, 2 * SG_DIM), D_MODEL),
        "sg_ln_g": gain(ks[9], (N_B, SG_DIM)),
        "sg_ln_b": 0.02 * jax.random.normal(ks[10], (N_B, SG_DIM), f32),
        "sg_w_s": w(ks[11], (N_B, SG_GROUPS, CHUNK, CHUNK), CHUNK),
        "sg_b_s": 1.0 + 0.1 * jax.random.normal(ks[12], (N_B, SG_GROUPS, CHUNK), f32),
        "sg_w_out": w(ks[13], (N_B, SG_DIM, D_MODEL), SG_DIM),
        "sb_w_qkv": w(ks[14], (N_C, D_MODEL, 3 * SB_HEADS * SB_HEAD_DIM), D_MODEL),
        "sb_w_out": w(ks[15], (N_C, SB_HEADS * SB_HEAD_DIM, D_MODEL), SB_HEADS * SB_HEAD_DIM),
        "ffn_w_up": w(ks[16], (DEPTH, D_MODEL, 2 * D_FF), D_MODEL),
        "ffn_conv_w": w(ks[17], (DEPTH, CONV_W, 2 * D_FF), CONV_W),
        "ffn_conv_b": 0.02 * jax.random.normal(ks[18], (DEPTH, 2 * D_FF), f32),
        "ffn_w_down": w(ks[19], (DEPTH, D_FF, D_MODEL), D_FF),
    }


def _fwd_reference(x, norm_mix_pre, norm_mix_post, norm_ffn_pre, norm_ffn_post,
              sc_w_in, sc_conv_w, sc_w_out,
              sg_w_in, sg_ln_g, sg_ln_b, sg_w_s, sg_b_s, sg_w_out,
              sb_w_qkv, sb_w_out,
              ffn_w_up, ffn_conv_w, ffn_conv_b, ffn_w_down):
    h = x
    for i in range(DEPTH):
        kind = i % N_MIXERS
        j = i // N_MIXERS
        hn = rmsnorm(h, norm_mix_pre[i])
        if kind == 0:
            m = short_conv_mixer(hn, sc_w_in[j], sc_conv_w[j], sc_w_out[j])
        elif kind == 1:
            m = chunked_sgu_mixer(hn, sg_w_in[j], sg_ln_g[j], sg_ln_b[j], sg_w_s[j], sg_b_s[j], sg_w_out[j])
        else:
            m = stick_breaking_mixer(hn, sb_w_qkv[j], sb_w_out[j])
        h = h + rmsnorm(m, norm_mix_post[i])
        f = conv_glu_ffn(rmsnorm(h, norm_ffn_pre[i]), ffn_w_up[i], ffn_conv_w[i], ffn_conv_b[i], ffn_w_down[i])
        h = h + rmsnorm(f, norm_ffn_post[i])
    return h


import jax as _jax
import jax.numpy as _jnp

TWIN_FORMAT = 'train_step'
FWD_PARAMS = ['x', 'norm_mix_pre', 'norm_mix_post', 'norm_ffn_pre', 'norm_ffn_post', 'sc_w_in', 'sc_conv_w', 'sc_w_out', 'sg_w_in', 'sg_ln_g', 'sg_ln_b', 'sg_w_s', 'sg_b_s', 'sg_w_out', 'sb_w_qkv', 'sb_w_out', 'ffn_w_up', 'ffn_conv_w', 'ffn_conv_b', 'ffn_w_down']
TWIN_WEIGHTS = ['norm_mix_pre', 'norm_mix_post', 'norm_ffn_pre', 'norm_ffn_post', 'sc_w_in', 'sc_conv_w', 'sc_w_out', 'sg_w_in', 'sg_ln_g', 'sg_ln_b', 'sg_w_s', 'sg_b_s', 'sg_w_out', 'sb_w_qkv', 'sb_w_out', 'ffn_w_up', 'ffn_conv_w', 'ffn_conv_b', 'ffn_w_down']
TWIN_DIFF_INPUT = 'x'
TWIN_INPUTS = ['x', 'norm_mix_pre', 'norm_mix_post', 'norm_ffn_pre', 'norm_ffn_post', 'sc_w_in', 'sc_conv_w', 'sc_w_out', 'sg_w_in', 'sg_ln_g', 'sg_ln_b', 'sg_w_s', 'sg_b_s', 'sg_w_out', 'sb_w_qkv', 'sb_w_out', 'ffn_w_up', 'ffn_conv_w', 'ffn_conv_b', 'ffn_w_down', 'loss_target', 'm_norm_mix_pre', 'm_norm_mix_post', 'm_norm_ffn_pre', 'm_norm_ffn_post', 'm_sc_w_in', 'm_sc_conv_w', 'm_sc_w_out', 'm_sg_w_in', 'm_sg_ln_g', 'm_sg_ln_b', 'm_sg_w_s', 'm_sg_b_s', 'm_sg_w_out', 'm_sb_w_qkv', 'm_sb_w_out', 'm_ffn_w_up', 'm_ffn_conv_w', 'm_ffn_conv_b', 'm_ffn_w_down', 'v_norm_mix_pre', 'v_norm_mix_post', 'v_norm_ffn_pre', 'v_norm_ffn_post', 'v_sc_w_in', 'v_sc_conv_w', 'v_sc_w_out', 'v_sg_w_in', 'v_sg_ln_g', 'v_sg_ln_b', 'v_sg_w_s', 'v_sg_b_s', 'v_sg_w_out', 'v_sb_w_qkv', 'v_sb_w_out', 'v_ffn_w_up', 'v_ffn_conv_w', 'v_ffn_conv_b', 'v_ffn_w_down']
TWIN_OUTPUTS = ['loss', 'grad_x', 'grad_norm_mix_pre', 'grad_norm_mix_post', 'grad_norm_ffn_pre', 'grad_norm_ffn_post', 'grad_sc_w_in', 'grad_sc_conv_w', 'grad_sc_w_out', 'grad_sg_w_in', 'grad_sg_ln_g', 'grad_sg_ln_b', 'grad_sg_w_s', 'grad_sg_b_s', 'grad_sg_w_out', 'grad_sb_w_qkv', 'grad_sb_w_out', 'grad_ffn_w_up', 'grad_ffn_conv_w', 'grad_ffn_conv_b', 'grad_ffn_w_down', 'delta_norm_mix_pre', 'delta_norm_mix_post', 'delta_norm_ffn_pre', 'delta_norm_ffn_post', 'delta_sc_w_in', 'delta_sc_conv_w', 'delta_sc_w_out', 'delta_sg_w_in', 'delta_sg_ln_g', 'delta_sg_ln_b', 'delta_sg_w_s', 'delta_sg_b_s', 'delta_sg_w_out', 'delta_sb_w_qkv', 'delta_sb_w_out', 'delta_ffn_w_up', 'delta_ffn_conv_w', 'delta_ffn_conv_b', 'delta_ffn_w_down', 'new_m_norm_mix_pre', 'new_m_norm_mix_post', 'new_m_norm_ffn_pre', 'new_m_norm_ffn_post', 'new_m_sc_w_in', 'new_m_sc_conv_w', 'new_m_sc_w_out', 'new_m_sg_w_in', 'new_m_sg_ln_g', 'new_m_sg_ln_b', 'new_m_sg_w_s', 'new_m_sg_b_s', 'new_m_sg_w_out', 'new_m_sb_w_qkv', 'new_m_sb_w_out', 'new_m_ffn_w_up', 'new_m_ffn_conv_w', 'new_m_ffn_conv_b', 'new_m_ffn_w_down', 'new_v_norm_mix_pre', 'new_v_norm_mix_post', 'new_v_norm_ffn_pre', 'new_v_norm_ffn_post', 'new_v_sc_w_in', 'new_v_sc_conv_w', 'new_v_sc_w_out', 'new_v_sg_w_in', 'new_v_sg_ln_g', 'new_v_sg_ln_b', 'new_v_sg_w_s', 'new_v_sg_b_s', 'new_v_sg_w_out', 'new_v_sb_w_qkv', 'new_v_sb_w_out', 'new_v_ffn_w_up', 'new_v_ffn_conv_w', 'new_v_ffn_conv_b', 'new_v_ffn_w_down']
TWIN_LEAF_KINDS = {'loss': 'loss', 'grad_x': 'grad_x', 'grad_norm_mix_pre': 'grad_w', 'grad_norm_mix_post': 'grad_w', 'grad_norm_ffn_pre': 'grad_w', 'grad_norm_ffn_post': 'grad_w', 'grad_sc_w_in': 'grad_w', 'grad_sc_conv_w': 'grad_w', 'grad_sc_w_out': 'grad_w', 'grad_sg_w_in': 'grad_w', 'grad_sg_ln_g': 'grad_w', 'grad_sg_ln_b': 'grad_w', 'grad_sg_w_s': 'grad_w', 'grad_sg_b_s': 'grad_w', 'grad_sg_w_out': 'grad_w', 'grad_sb_w_qkv': 'grad_w', 'grad_sb_w_out': 'grad_w', 'grad_ffn_w_up': 'grad_w', 'grad_ffn_conv_w': 'grad_w', 'grad_ffn_conv_b': 'grad_w', 'grad_ffn_w_down': 'grad_w', 'delta_norm_mix_pre': 'delta_w', 'delta_norm_mix_post': 'delta_w', 'delta_norm_ffn_pre': 'delta_w', 'delta_norm_ffn_post': 'delta_w', 'delta_sc_w_in': 'delta_w', 'delta_sc_conv_w': 'delta_w', 'delta_sc_w_out': 'delta_w', 'delta_sg_w_in': 'delta_w', 'delta_sg_ln_g': 'delta_w', 'delta_sg_ln_b': 'delta_w', 'delta_sg_w_s': 'delta_w', 'delta_sg_b_s': 'delta_w', 'delta_sg_w_out': 'delta_w', 'delta_sb_w_qkv': 'delta_w', 'delta_sb_w_out': 'delta_w', 'delta_ffn_w_up': 'delta_w', 'delta_ffn_conv_w': 'delta_w', 'delta_ffn_conv_b': 'delta_w', 'delta_ffn_w_down': 'delta_w', 'new_m_norm_mix_pre': 'new_m', 'new_m_norm_mix_post': 'new_m', 'new_m_norm_ffn_pre': 'new_m', 'new_m_norm_ffn_post': 'new_m', 'new_m_sc_w_in': 'new_m', 'new_m_sc_conv_w': 'new_m', 'new_m_sc_w_out': 'new_m', 'new_m_sg_w_in': 'new_m', 'new_m_sg_ln_g': 'new_m', 'new_m_sg_ln_b': 'new_m', 'new_m_sg_w_s': 'new_m', 'new_m_sg_b_s': 'new_m', 'new_m_sg_w_out': 'new_m', 'new_m_sb_w_qkv': 'new_m', 'new_m_sb_w_out': 'new_m', 'new_m_ffn_w_up': 'new_m', 'new_m_ffn_conv_w': 'new_m', 'new_m_ffn_conv_b': 'new_m', 'new_m_ffn_w_down': 'new_m', 'new_v_norm_mix_pre': 'new_v', 'new_v_norm_mix_post': 'new_v', 'new_v_norm_ffn_pre': 'new_v', 'new_v_norm_ffn_post': 'new_v', 'new_v_sc_w_in': 'new_v', 'new_v_sc_conv_w': 'new_v', 'new_v_sc_w_out': 'new_v', 'new_v_sg_w_in': 'new_v', 'new_v_sg_ln_g': 'new_v', 'new_v_sg_ln_b': 'new_v', 'new_v_sg_w_s': 'new_v', 'new_v_sg_b_s': 'new_v', 'new_v_sg_w_out': 'new_v', 'new_v_sb_w_qkv': 'new_v', 'new_v_sb_w_out': 'new_v', 'new_v_ffn_w_up': 'new_v', 'new_v_ffn_conv_w': 'new_v', 'new_v_ffn_conv_b': 'new_v', 'new_v_ffn_w_down': 'new_v'}


def _forward(args):
    return _fwd_reference(*[args[k] for k in FWD_PARAMS])


def _output_shape():
    def fwd():
        inp = _fwd_setup_inputs(0)
        return _fwd_reference(*[inp[k] for k in FWD_PARAMS])
    out = _jax.eval_shape(fwd)
    return out.shape, out.dtype

N_MICROBATCH = 1
ADAM_LR = 0.001
ADAM_B1 = 0.9
ADAM_B2 = 0.999
ADAM_EPS = 1e-08
ADAM_WD = 0.01
ADAM_STEP = 10
PER_EXAMPLE_BATCH_AXIS = {'x': 0, 'loss_target': 0}
SHARED_INPUTS = []
_WEIGHT_DTYPES = {'norm_mix_pre': _jnp.float32, 'norm_mix_post': _jnp.float32, 'norm_ffn_pre': _jnp.float32, 'norm_ffn_post': _jnp.float32, 'sc_w_in': _jnp.float32, 'sc_conv_w': _jnp.float32, 'sc_w_out': _jnp.float32, 'sg_w_in': _jnp.float32, 'sg_ln_g': _jnp.float32, 'sg_ln_b': _jnp.float32, 'sg_w_s': _jnp.float32, 'sg_b_s': _jnp.float32, 'sg_w_out': _jnp.float32, 'sb_w_qkv': _jnp.float32, 'sb_w_out': _jnp.float32, 'ffn_w_up': _jnp.float32, 'ffn_conv_w': _jnp.float32, 'ffn_conv_b': _jnp.float32, 'ffn_w_down': _jnp.float32}
MOMENT_SCALE = {'norm_mix_pre': 1.742390e+00, 'norm_mix_post': 3.213848e+01, 'norm_ffn_pre': 1.167089e+00, 'norm_ffn_post': 3.185187e+01, 'sc_w_in': 1.054903e+00, 'sc_conv_w': 1.077761e+00, 'sc_w_out': 1.079933e+00, 'sg_w_in': 7.938228e-01, 'sg_ln_g': 4.566616e-01, 'sg_ln_b': 4.734306e-01, 'sg_w_s': 6.385040e-01, 'sg_b_s': 1.034172e+00, 'sg_w_out': 3.882252e+00, 'sb_w_qkv': 1.391697e+00, 'sb_w_out': 2.633644e+00, 'ffn_w_up': 4.963506e-01, 'ffn_conv_w': 5.389005e-01, 'ffn_conv_b': 2.420506e+00, 'ffn_w_down': 9.271287e-01}


def _to_microbatches(a, axis):
    t = _jnp.moveaxis(a, axis, 0)
    t = t.reshape((N_MICROBATCH, t.shape[0] // N_MICROBATCH) + t.shape[1:])
    return _jnp.moveaxis(t, 1, axis + 1)


def setup_inputs(seed: int = 0) -> dict:
    inp = _fwd_setup_inputs(seed)
    key = _jax.random.fold_in(_jax.random.key(seed), 7919)
    shape, _ = _output_shape()
    out = dict(inp)
    out["loss_target"] = _jax.random.normal(_jax.random.fold_in(key, 0), shape, _jnp.float32)
    for i, name in enumerate(TWIN_WEIGHTS):
        w = inp[name].astype(_jnp.float32)
        if MOMENT_SCALE is None:
            s = _jnp.sqrt(_jnp.mean(_jnp.square(w)) + 1e-30)
        else:
            s = MOMENT_SCALE[name]
        km, kv = _jax.random.split(_jax.random.fold_in(key, i + 1))
        out[name] = w
        out["m_" + name] = s * _jax.random.normal(km, w.shape, _jnp.float32)
        out["v_" + name] = (s * s) * _jax.random.uniform(kv, w.shape, _jnp.float32, 0.5, 1.5)
    if N_MICROBATCH > 1:
        for name, axis in PER_EXAMPLE_BATCH_AXIS.items():
            out[name] = _to_microbatches(out[name], axis)
    return {'x': out['x'], 'norm_mix_pre': out['norm_mix_pre'], 'norm_mix_post': out['norm_mix_post'], 'norm_ffn_pre': out['norm_ffn_pre'], 'norm_ffn_post': out['norm_ffn_post'], 'sc_w_in': out['sc_w_in'], 'sc_conv_w': out['sc_conv_w'], 'sc_w_out': out['sc_w_out'], 'sg_w_in': out['sg_w_in'], 'sg_ln_g': out['sg_ln_g'], 'sg_ln_b': out['sg_ln_b'], 'sg_w_s': out['sg_w_s'], 'sg_b_s': out['sg_b_s'], 'sg_w_out': out['sg_w_out'], 'sb_w_qkv': out['sb_w_qkv'], 'sb_w_out': out['sb_w_out'], 'ffn_w_up': out['ffn_w_up'], 'ffn_conv_w': out['ffn_conv_w'], 'ffn_conv_b': out['ffn_conv_b'], 'ffn_w_down': out['ffn_w_down'], 'loss_target': out['loss_target'], 'm_norm_mix_pre': out['m_norm_mix_pre'], 'm_norm_mix_post': out['m_norm_mix_post'], 'm_norm_ffn_pre': out['m_norm_ffn_pre'], 'm_norm_ffn_post': out['m_norm_ffn_post'], 'm_sc_w_in': out['m_sc_w_in'], 'm_sc_conv_w': out['m_sc_conv_w'], 'm_sc_w_out': out['m_sc_w_out'], 'm_sg_w_in': out['m_sg_w_in'], 'm_sg_ln_g': out['m_sg_ln_g'], 'm_sg_ln_b': out['m_sg_ln_b'], 'm_sg_w_s': out['m_sg_w_s'], 'm_sg_b_s': out['m_sg_b_s'], 'm_sg_w_out': out['m_sg_w_out'], 'm_sb_w_qkv': out['m_sb_w_qkv'], 'm_sb_w_out': out['m_sb_w_out'], 'm_ffn_w_up': out['m_ffn_w_up'], 'm_ffn_conv_w': out['m_ffn_conv_w'], 'm_ffn_conv_b': out['m_ffn_conv_b'], 'm_ffn_w_down': out['m_ffn_w_down'], 'v_norm_mix_pre': out['v_norm_mix_pre'], 'v_norm_mix_post': out['v_norm_mix_post'], 'v_norm_ffn_pre': out['v_norm_ffn_pre'], 'v_norm_ffn_post': out['v_norm_ffn_post'], 'v_sc_w_in': out['v_sc_w_in'], 'v_sc_conv_w': out['v_sc_conv_w'], 'v_sc_w_out': out['v_sc_w_out'], 'v_sg_w_in': out['v_sg_w_in'], 'v_sg_ln_g': out['v_sg_ln_g'], 'v_sg_ln_b': out['v_sg_ln_b'], 'v_sg_w_s': out['v_sg_w_s'], 'v_sg_b_s': out['v_sg_b_s'], 'v_sg_w_out': out['v_sg_w_out'], 'v_sb_w_qkv': out['v_sb_w_qkv'], 'v_sb_w_out': out['v_sb_w_out'], 'v_ffn_w_up': out['v_ffn_w_up'], 'v_ffn_conv_w': out['v_ffn_conv_w'], 'v_ffn_conv_b': out['v_ffn_conv_b'], 'v_ffn_w_down': out['v_ffn_w_down']}


def _loss(weights, diff, rest, loss_target):
    with _jax.named_scope("forward"):
        args = {**rest, TWIN_DIFF_INPUT: diff, **{k: w.astype(_WEIGHT_DTYPES[k]) for k, w in weights.items()}}
        y = _forward(args)
    with _jax.named_scope("loss_head"):
        err = _jnp.square(y.astype(_jnp.float32) - loss_target)
        return 0.5 * _jnp.sum(_jnp.mean(err, axis=-1)) if err.ndim else 0.5 * err


def _adamw(w, g, m, v):
    m = ADAM_B1 * m + (1.0 - ADAM_B1) * g
    v = ADAM_B2 * v + (1.0 - ADAM_B2) * _jnp.square(g)
    m_hat = m / (1.0 - ADAM_B1 ** ADAM_STEP)
    v_hat = v / (1.0 - ADAM_B2 ** ADAM_STEP)
    delta = -ADAM_LR * (m_hat / (_jnp.sqrt(v_hat) + ADAM_EPS) + ADAM_WD * w)
    return delta, m, v


def reference(x, norm_mix_pre, norm_mix_post, norm_ffn_pre, norm_ffn_post, sc_w_in, sc_conv_w, sc_w_out, sg_w_in, sg_ln_g, sg_ln_b, sg_w_s, sg_b_s, sg_w_out, sb_w_qkv, sb_w_out, ffn_w_up, ffn_conv_w, ffn_conv_b, ffn_w_down, loss_target, m_norm_mix_pre, m_norm_mix_post, m_norm_ffn_pre, m_norm_ffn_post, m_sc_w_in, m_sc_conv_w, m_sc_w_out, m_sg_w_in, m_sg_ln_g, m_sg_ln_b, m_sg_w_s, m_sg_b_s, m_sg_w_out, m_sb_w_qkv, m_sb_w_out, m_ffn_w_up, m_ffn_conv_w, m_ffn_conv_b, m_ffn_w_down, v_norm_mix_pre, v_norm_mix_post, v_norm_ffn_pre, v_norm_ffn_post, v_sc_w_in, v_sc_conv_w, v_sc_w_out, v_sg_w_in, v_sg_ln_g, v_sg_ln_b, v_sg_w_s, v_sg_b_s, v_sg_w_out, v_sb_w_qkv, v_sb_w_out, v_ffn_w_up, v_ffn_conv_w, v_ffn_conv_b, v_ffn_w_down):
    given = dict(x=x, norm_mix_pre=norm_mix_pre, norm_mix_post=norm_mix_post, norm_ffn_pre=norm_ffn_pre, norm_ffn_post=norm_ffn_post, sc_w_in=sc_w_in, sc_conv_w=sc_conv_w, sc_w_out=sc_w_out, sg_w_in=sg_w_in, sg_ln_g=sg_ln_g, sg_ln_b=sg_ln_b, sg_w_s=sg_w_s, sg_b_s=sg_b_s, sg_w_out=sg_w_out, sb_w_qkv=sb_w_qkv, sb_w_out=sb_w_out, ffn_w_up=ffn_w_up, ffn_conv_w=ffn_conv_w, ffn_conv_b=ffn_conv_b, ffn_w_down=ffn_w_down, loss_target=loss_target, m_norm_mix_pre=m_norm_mix_pre, m_norm_mix_post=m_norm_mix_post, m_norm_ffn_pre=m_norm_ffn_pre, m_norm_ffn_post=m_norm_ffn_post, m_sc_w_in=m_sc_w_in, m_sc_conv_w=m_sc_conv_w, m_sc_w_out=m_sc_w_out, m_sg_w_in=m_sg_w_in, m_sg_ln_g=m_sg_ln_g, m_sg_ln_b=m_sg_ln_b, m_sg_w_s=m_sg_w_s, m_sg_b_s=m_sg_b_s, m_sg_w_out=m_sg_w_out, m_sb_w_qkv=m_sb_w_qkv, m_sb_w_out=m_sb_w_out, m_ffn_w_up=m_ffn_w_up, m_ffn_conv_w=m_ffn_conv_w, m_ffn_conv_b=m_ffn_conv_b, m_ffn_w_down=m_ffn_w_down, v_norm_mix_pre=v_norm_mix_pre, v_norm_mix_post=v_norm_mix_post, v_norm_ffn_pre=v_norm_ffn_pre, v_norm_ffn_post=v_norm_ffn_post, v_sc_w_in=v_sc_w_in, v_sc_conv_w=v_sc_conv_w, v_sc_w_out=v_sc_w_out, v_sg_w_in=v_sg_w_in, v_sg_ln_g=v_sg_ln_g, v_sg_ln_b=v_sg_ln_b, v_sg_w_s=v_sg_w_s, v_sg_b_s=v_sg_b_s, v_sg_w_out=v_sg_w_out, v_sb_w_qkv=v_sb_w_qkv, v_sb_w_out=v_sb_w_out, v_ffn_w_up=v_ffn_w_up, v_ffn_conv_w=v_ffn_conv_w, v_ffn_conv_b=v_ffn_conv_b, v_ffn_w_down=v_ffn_w_down)
    weights = {n: given[n] for n in TWIN_WEIGHTS}
    shared = {n: given[n] for n in SHARED_INPUTS}
    per_example = {n: given[n] for n in ['x']}
    grad_fn = _jax.value_and_grad(_loss, argnums=(0, 1))

    def one_microbatch(ex, loss_target):
        ex = dict(ex)
        diff = ex.pop(TWIN_DIFF_INPUT)
        return grad_fn(weights, diff, {**shared, **ex}, loss_target)

    if N_MICROBATCH == 1:
        loss, (grad_w, grad_x) = one_microbatch(per_example, given["loss_target"])
    else:
        def body(carry, xs):
            loss_sum, grad_sum = carry
            l_k, (gw_k, gx_k) = one_microbatch(xs[0], xs[1])
            with _jax.named_scope("update"):
                return (loss_sum + l_k, _jax.tree.map(_jnp.add, grad_sum, gw_k)), gx_k

        init = (_jnp.zeros((), _jnp.float32), _jax.tree.map(_jnp.zeros_like, weights))
        (loss, grad_w), grad_x = _jax.lax.scan(body, init, (per_example, given["loss_target"]))
    with _jax.named_scope("update"):
        delta_w, new_m, new_v = {}, {}, {}
        for n in TWIN_WEIGHTS:
            delta_w[n], new_m[n], new_v[n] = _adamw(weights[n], grad_w[n], given["m_" + n], given["v_" + n])
    return (loss, grad_x, *[grad_w[n] for n in TWIN_WEIGHTS], *[delta_w[n] for n in TWIN_WEIGHTS],
            *[new_m[n] for n in TWIN_WEIGHTS], *[new_v[n] for n in TWIN_WEIGHTS])
```

```python
import functools
import math

import jax
import jax.numpy as jnp
from jax import lax
from jax.experimental import pallas as pl
from jax.experimental.pallas import tpu as pltpu

F32 = jnp.float32
BF16 = jnp.bfloat16
MESH = pl.DeviceIdType.MESH
ANY = pl.BlockSpec(memory_space=pl.ANY)

EPS = 1e-6
HEAD_DIM = 128
N_MIXERS = 3
ADAM_LR, ADAM_B1, ADAM_B2, ADAM_EPS, ADAM_WD, ADAM_STEP = 0.001, 0.9, 0.999, 1e-08, 0.01, 10
V7X_VMEM_LIMIT = 56 * 1024 * 1024
HALO = 8
LANES = 128
INV_SQRT2 = 1.0 / math.sqrt(2.0)
INV_SQRT_2PI = 1.0 / math.sqrt(2.0 * math.pi)


def _tile(n, target, mult):
    t = min(n, target) // mult * mult
    while t >= mult:
        if n % t == 0:
            return t
        t -= mult
    return n


def _params(*sem):
    return pltpu.CompilerParams(dimension_semantics=sem, vmem_limit_bytes=V7X_VMEM_LIMIT)


def _dot(a, b, dims):
    return lax.dot_general(a, b, (dims, ((), ())), preferred_element_type=F32)


def _dot_nn(a, b):
    return _dot(a, b, ((1,), (0,)))


def _dot_nt(a, b):
    return _dot(a, b, ((1,), (1,)))


def _dot_tn(a, b):
    return _dot(a, b, ((0,), (0,)))


def _mm_nn(a, w, out_dtype, name):
    m, k = a.shape
    s, _, n = w.shape
    tm, tn, tk = _tile(m, 1024, 16), _tile(n, 1536, LANES), _tile(k, 512, LANES)
    nb, nk = n // tn, k // tk

    def body(a_ref, w_ref, o_ref, acc):
        kk = pl.program_id(2)

        @pl.when(kk == 0)
        def _():
            acc[...] = jnp.zeros_like(acc)

        acc[...] += _dot_nn(a_ref[...], w_ref[...])

        @pl.when(kk == nk - 1)
        def _():
            o_ref[...] = acc[...].astype(o_ref.dtype)

    return pl.pallas_call(
        body, name=name, grid=(m // tm, s * nb, nk),
        in_specs=[pl.BlockSpec((tm, tk), lambda i, j, kk: (i, kk)),
                  pl.BlockSpec((None, tk, tn), lambda i, j, kk: (j // nb, kk, j % nb))],
        out_specs=pl.BlockSpec((tm, tn), lambda i, j, kk: (i, j)),
        out_shape=jax.ShapeDtypeStruct((m, s * n), out_dtype),
        scratch_shapes=[pltpu.VMEM((tm, tn), F32)],
        compiler_params=_params("parallel", "parallel", "arbitrary"),
    )(a, w)


def _mm_nt(dy, w, out_dtype, name):
    m = dy.shape[0]
    s, k, n = w.shape
    tm, tn, tko = _tile(m, 1024, 16), _tile(n, 1536, LANES), _tile(k, 1024, LANES)
    nb = n // tn
    nr = s * nb

    def body(dy_ref, w_ref, o_ref, acc):
        r = pl.program_id(2)

        @pl.when(r == 0)
        def _():
            acc[...] = jnp.zeros_like(acc)

        acc[...] += _dot_nt(dy_ref[...], w_ref[...])

        @pl.when(r == nr - 1)
        def _():
            o_ref[...] = acc[...].astype(o_ref.dtype)

    return pl.pallas_call(
        body, name=name, grid=(m // tm, k // tko, nr),
        in_specs=[pl.BlockSpec((tm, tn), lambda i, j, r: (i, r)),
                  pl.BlockSpec((None, tko, tn), lambda i, j, r: (r // nb, j, r % nb))],
        out_specs=pl.BlockSpec((tm, tko), lambda i, j, r: (i, j)),
        out_shape=jax.ShapeDtypeStruct((m, k), out_dtype),
        scratch_shapes=[pltpu.VMEM((tm, tko), F32)],
        compiler_params=_params("parallel", "parallel", "arbitrary"),
    )(dy, w)


def _mm_tn(x, dy, s, name):
    t, k = x.shape
    n = dy.shape[1] // s
    tk, tn, tt = _tile(k, 512, LANES), _tile(n, 1536, LANES), _tile(t, 1024, 16)
    nb, nt = n // tn, t // tt

    def body(x_ref, dy_ref, o_ref, acc):
        tt_i = pl.program_id(2)

        @pl.when(tt_i == 0)
        def _():
            acc[...] = jnp.zeros_like(acc)

        acc[...] += _dot_tn(x_ref[...], dy_ref[...])

        @pl.when(tt_i == nt - 1)
        def _():
            o_ref[...] = acc[...].astype(o_ref.dtype)

    return pl.pallas_call(
        body, name=name, grid=(k // tk, s * nb, nt),
        in_specs=[pl.BlockSpec((tt, tk), lambda i, j, q: (q, i)),
                  pl.BlockSpec((tt, tn), lambda i, j, q: (q, j))],
        out_specs=pl.BlockSpec((None, tk, tn), lambda i, j, q: (j // nb, i, j % nb)),
        out_shape=jax.ShapeDtypeStruct((s, k, n), BF16),
        scratch_shapes=[pltpu.VMEM((tk, tn), F32)],
        compiler_params=_params("parallel", "parallel", "arbitrary"),
    )(x, dy)


def _rms(x, g):
    return x * lax.rsqrt(jnp.mean(x * x, axis=-1, keepdims=True) + EPS) * g


def _rms_bwd(x, g, dy):
    r = lax.rsqrt(jnp.mean(x * x, axis=-1, keepdims=True) + EPS)
    gy = dy * g
    dx = r * gy - x * (r * r * r * jnp.mean(x * gy, axis=-1, keepdims=True))
    return dx, dy * (x * r)


def _row_spec(tt, d):
    return pl.BlockSpec((tt, d), lambda i: (i, 0))


def _vec_spec(d):
    return pl.BlockSpec((1, d), lambda i: (0, 0))


def _rms_fwd(x, g, name):
    t, d = x.shape
    tt = _tile(t, 512, 16)

    def body(x_ref, g_ref, o_ref):
        o_ref[...] = _rms(x_ref[...], g_ref[...]).astype(o_ref.dtype)

    return pl.pallas_call(
        body, name=name, grid=(t // tt,),
        in_specs=[_row_spec(tt, d), _vec_spec(d)], out_specs=_row_spec(tt, d),
        out_shape=jax.ShapeDtypeStruct((t, d), BF16), compiler_params=_params("parallel"),
    )(x, g)


def _norm_step(h, m, g_post, g_next, name):
    t, d = h.shape
    tt = _tile(t, 512, 16)

    def body(h_ref, m_ref, gp_ref, gn_ref, ho_ref, xn_ref):
        hn = h_ref[...] + _rms(m_ref[...], gp_ref[...])
        ho_ref[...] = hn
        xn_ref[...] = _rms(hn, gn_ref[...]).astype(xn_ref.dtype)

    return pl.pallas_call(
        body, name=name, grid=(t // tt,),
        in_specs=[_row_spec(tt, d), _row_spec(tt, d), _vec_spec(d), _vec_spec(d)],
        out_specs=[_row_spec(tt, d), _row_spec(tt, d)],
        out_shape=[jax.ShapeDtypeStruct((t, d), F32), jax.ShapeDtypeStruct((t, d), BF16)],
        compiler_params=_params("parallel"),
    )(h, m, g_post, g_next)


def _final_loss(h, f, g_post, target, name):
    t, d = h.shape
    tt = _tile(t, 256, 16)

    def body(h_ref, f_ref, g_ref, tg_ref, loss_ref, dh_ref, df_ref, dg_ref):
        @pl.when(pl.program_id(0) == 0)
        def _():
            loss_ref[...] = jnp.zeros_like(loss_ref)
            dg_ref[...] = jnp.zeros_like(dg_ref)

        fv, g = f_ref[...], g_ref[...]
        err = h_ref[...] + _rms(fv, g) - tg_ref[...]
        per_row = jnp.mean(err * err, axis=-1, keepdims=True)
        loss_ref[...] += 0.5 * jnp.sum(per_row, axis=0, keepdims=True)
        dh = err * (1.0 / d)
        dh_ref[...] = dh
        df, dg = _rms_bwd(fv, g, dh)
        df_ref[...] = df.astype(df_ref.dtype)
        dg_ref[...] += jnp.sum(dg, axis=0, keepdims=True)

    return pl.pallas_call(
        body, name=name, grid=(t // tt,),
        in_specs=[_row_spec(tt, d), _row_spec(tt, d), _vec_spec(d), _row_spec(tt, d)],
        out_specs=[pl.BlockSpec((1, 1), lambda i: (0, 0)), _row_spec(tt, d), _row_spec(tt, d), _vec_spec(d)],
        out_shape=[jax.ShapeDtypeStruct((1, 1), F32), jax.ShapeDtypeStruct((t, d), F32),
                   jax.ShapeDtypeStruct((t, d), BF16), jax.ShapeDtypeStruct((1, d), F32)],
        compiler_params=_params("arbitrary"),
    )(h, f, g_post, target)


def _norm_bwd_step(dh_out, dxn, x, g_pre, prev, name):
    t, d = x.shape
    tt = _tile(t, 256, 16)
    has_prev = prev is not None

    def body(*refs):
        if has_prev:
            dho_ref, dxn_ref, x_ref, g_ref, xa_ref, ga_ref, dh_ref, dg_ref, da_ref, dga_ref = refs
        else:
            dho_ref, dxn_ref, x_ref, g_ref, dh_ref, dg_ref = refs

        @pl.when(pl.program_id(0) == 0)
        def _():
            dg_ref[...] = jnp.zeros_like(dg_ref)
            if has_prev:
                dga_ref[...] = jnp.zeros_like(dga_ref)

        dx, dg = _rms_bwd(x_ref[...], g_ref[...], dxn_ref[...])
        dh = dho_ref[...] + dx
        dh_ref[...] = dh
        dg_ref[...] += jnp.sum(dg, axis=0, keepdims=True)
        if has_prev:
            da, dga = _rms_bwd(xa_ref[...], ga_ref[...], dh)
            da_ref[...] = da.astype(da_ref.dtype)
            dga_ref[...] += jnp.sum(dga, axis=0, keepdims=True)

    ins = [dh_out, dxn, x, g_pre] + (list(prev) if has_prev else [])
    in_specs = [_row_spec(tt, d)] * 3 + [_vec_spec(d)] + ([_row_spec(tt, d), _vec_spec(d)] if has_prev else [])
    out_specs = [_row_spec(tt, d), _vec_spec(d)] + ([_row_spec(tt, d), _vec_spec(d)] if has_prev else [])
    out_shape = [jax.ShapeDtypeStruct((t, d), F32), jax.ShapeDtypeStruct((1, d), F32)]
    if has_prev:
        out_shape += [jax.ShapeDtypeStruct((t, d), BF16), jax.ShapeDtypeStruct((1, d), F32)]
    return pl.pallas_call(
        body, name=name, grid=(t // tt,), in_specs=in_specs, out_specs=out_specs, out_shape=out_shape,
        compiler_params=_params("arbitrary"),
    )(*ins)


def _shift_down(x, k, halo):
    r = pltpu.roll(x, k, 0)
    rh = pltpu.roll(halo, k, 0)
    row = lax.broadcasted_iota(jnp.int32, rh.shape, 0)
    return jnp.concatenate([jnp.where(row < k, rh, r[:HALO]), r[HALO:]], axis=0)


def _shift_up(x_ext, k, n):
    return pltpu.roll(x_ext, x_ext.shape[0] - k, 0)[:n]


def _halo_specs(tt, width, nblk):
    per = tt // HALO
    prev = pl.BlockSpec((HALO, width), lambda i: (jnp.maximum(i * per - 1, 0), 0))
    nxt = pl.BlockSpec((HALO, width), lambda i: (jnp.minimum((i + 1) * per, nblk * per - 1), 0))
    return prev, nxt


def _conv3(x, halo, w0, w1, w2):
    return w0 * _shift_down(x, 2, halo) + w1 * _shift_down(x, 1, halo) + w2 * x


def _sc_fwd(p, wc, name):
    t, c3 = p.shape
    c = c3 // 3
    tt, cc = _tile(t, 256, 16), _tile(c, 512, LANES)
    nblk = t // tt
    prev_spec, _ = _halo_specs(tt, c3, nblk)

    def body(p_ref, pp_ref, w_ref, y_ref):
        first = pl.program_id(0) == 0
        for j in range(c // cc):
            cols = slice(j * cc, (j + 1) * cc)
            gc, hv = p_ref[:, c + j * cc:c + (j + 1) * cc], p_ref[:, 2 * c + j * cc:2 * c + (j + 1) * cc]
            uh = jnp.where(first, 0.0, pp_ref[:, c + j * cc:c + (j + 1) * cc] * pp_ref[:, 2 * c + j * cc:2 * c + (j + 1) * cc])
            cv = _conv3(gc * hv, uh, w_ref[0:1, cols], w_ref[1:2, cols], w_ref[2:3, cols])
            y_ref[:, cols] = (p_ref[:, cols] * cv).astype(y_ref.dtype)

    return pl.pallas_call(
        body, name=name, grid=(nblk,),
        in_specs=[_row_spec(tt, c3), prev_spec, pl.BlockSpec((3, c), lambda i: (0, 0))],
        out_specs=_row_spec(tt, c), out_shape=jax.ShapeDtypeStruct((t, c), BF16),
        compiler_params=_params("parallel"),
    )(p, p, wc)


def _sc_bwd(p, dy, wc, name):
    t, c3 = p.shape
    c = c3 // 3
    tt, cc = _tile(t, 256, 16), _tile(c, 512, LANES)
    nblk = t // tt
    p_prev, p_next = _halo_specs(tt, c3, nblk)
    _, dy_next = _halo_specs(tt, c, nblk)

    def body(p_ref, pp_ref, pn_ref, dy_ref, dyn_ref, w_ref, dp_ref, dw_ref):
        i = pl.program_id(0)
        first, last = i == 0, i == nblk - 1

        @pl.when(first)
        def _():
            dw_ref[...] = jnp.zeros_like(dw_ref)

        for j in range(c // cc):
            a, b, d = slice(j * cc, (j + 1) * cc), slice(c + j * cc, c + (j + 1) * cc), slice(2 * c + j * cc, 2 * c + (j + 1) * cc)
            w0, w1, w2 = w_ref[0:1, a], w_ref[1:2, a], w_ref[2:3, a]
            gb, gc, hv, dyv = p_ref[:, a], p_ref[:, b], p_ref[:, d], dy_ref[:, a]
            u = gc * hv
            uh = jnp.where(first, 0.0, pp_ref[:, b] * pp_ref[:, d])
            u2, u1 = _shift_down(u, 2, uh), _shift_down(u, 1, uh)
            cv = w0 * u2 + w1 * u1 + w2 * u
            dcv = dyv * gb
            dcv_ext = jnp.concatenate([dcv, jnp.where(last, 0.0, dyn_ref[:, a] * pn_ref[:, a])], axis=0)
            du = w2 * dcv + w1 * _shift_up(dcv_ext, 1, tt) + w0 * _shift_up(dcv_ext, 2, tt)
            dp_ref[:, a] = (dyv * cv).astype(dp_ref.dtype)
            dp_ref[:, b] = (du * hv).astype(dp_ref.dtype)
            dp_ref[:, d] = (du * gc).astype(dp_ref.dtype)
            dw_ref[0:1, a] += jnp.sum(dcv * u2, axis=0, keepdims=True)
            dw_ref[1:2, a] += jnp.sum(dcv * u1, axis=0, keepdims=True)
            dw_ref[2:3, a] += jnp.sum(dcv * u, axis=0, keepdims=True)

    return pl.pallas_call(
        body, name=name, grid=(nblk,),
        in_specs=[_row_spec(tt, c3), p_prev, p_next, _row_spec(tt, c), dy_next, pl.BlockSpec((3, c), lambda i: (0, 0))],
        out_specs=[_row_spec(tt, c3), pl.BlockSpec((HALO, c), lambda i: (0, 0))],
        out_shape=[jax.ShapeDtypeStruct((t, c3), BF16), jax.ShapeDtypeStruct((HALO, c), F32)],
        compiler_params=_params("arbitrary"),
    )(p, p, p, dy, dy, wc)


def _ffn_fwd(u, cw, cb, name):
    t, f2 = u.shape
    f = f2 // 2
    tt, cc = _tile(t, 128, 16), _tile(f, 512, LANES)
    nblk = t // tt
    prev_spec, _ = _halo_specs(tt, f2, nblk)

    def body(u_ref, up_ref, w_ref, b_ref, a_ref):
        first = pl.program_id(0) == 0

        def conv(cols):
            halo = jnp.where(first, 0.0, up_ref[:, cols])
            return _conv3(u_ref[:, cols], halo, w_ref[0:1, cols], w_ref[1:2, cols], w_ref[2:3, cols]) + b_ref[:, cols]

        for j in range(f // cc):
            hg, hv = conv(slice(j * cc, (j + 1) * cc)), conv(slice(f + j * cc, f + (j + 1) * cc))
            a_ref[:, j * cc:(j + 1) * cc] = (hg * jax.nn.sigmoid(hg) * hv).astype(a_ref.dtype)

    return pl.pallas_call(
        body, name=name, grid=(nblk,),
        in_specs=[_row_spec(tt, f2), prev_spec, pl.BlockSpec((3, f2), lambda i: (0, 0)), _vec_spec(f2)],
        out_specs=_row_spec(tt, f), out_shape=jax.ShapeDtypeStruct((t, f), BF16),
        compiler_params=_params("parallel"),
    )(u, u, cw, cb)


def _ffn_bwd(u, da, cw, cb, name):
    t, f2 = u.shape
    f = f2 // 2
    tt, cc = _tile(t, 128, 16), _tile(f, 512, LANES)
    nblk = t // tt
    u_prev, u_next = _halo_specs(tt, f2, nblk)
    _, da_next = _halo_specs(tt, f, nblk)

    def body(u_ref, up_ref, un_ref, da_ref, dan_ref, w_ref, b_ref, du_ref, dw_ref):
        i = pl.program_id(0)
        first, last = i == 0, i == nblk - 1

        @pl.when(first)
        def _():
            dw_ref[...] = jnp.zeros_like(dw_ref)

        keep = jnp.where(last, 0.0, 1.0)

        def ext(cols):
            x = jnp.concatenate([u_ref[:, cols], un_ref[:, cols]], axis=0)
            halo = jnp.where(first, 0.0, up_ref[:, cols])
            x2, x1 = _shift_down(x, 2, halo), _shift_down(x, 1, halo)
            h = w_ref[0:1, cols] * x2 + w_ref[1:2, cols] * x1 + w_ref[2:3, cols] * x + b_ref[:, cols]
            return x, x1, x2, h

        def back(cols, dh_ext, x, x1, x2):
            w0, w1, w2 = w_ref[0:1, cols], w_ref[1:2, cols], w_ref[2:3, cols]
            dh = dh_ext[:tt]
            du_ref[:, cols] = (w2 * dh + w1 * _shift_up(dh_ext, 1, tt) + w0 * _shift_up(dh_ext, 2, tt)).astype(du_ref.dtype)
            dw_ref[0:1, cols] += jnp.sum(dh * x2[:tt], axis=0, keepdims=True)
            dw_ref[1:2, cols] += jnp.sum(dh * x1[:tt], axis=0, keepdims=True)
            dw_ref[2:3, cols] += jnp.sum(dh * x[:tt], axis=0, keepdims=True)
            dw_ref[3:4, cols] += jnp.sum(dh, axis=0, keepdims=True)

        row = lax.broadcasted_iota(jnp.int32, (tt + HALO, 1), 0)
        for j in range(f // cc):
            gcols, vcols = slice(j * cc, (j + 1) * cc), slice(f + j * cc, f + (j + 1) * cc)
            xg, xg1, xg2, hg = ext(gcols)
            xv, xv1, xv2, hv = ext(vcols)
            da_ext = jnp.concatenate([da_ref[:, gcols], dan_ref[:, gcols]], axis=0)
            da_ext = jnp.where(row < tt, da_ext, da_ext * keep)
            sg = jax.nn.sigmoid(hg)
            back(gcols, da_ext * hv * (sg * (1.0 + hg * (1.0 - sg))), xg, xg1, xg2)
            back(vcols, da_ext * (hg * sg), xv, xv1, xv2)

    return pl.pallas_call(
        body, name=name, grid=(nblk,),
        in_specs=[_row_spec(tt, f2), u_prev, u_next, _row_spec(tt, f), da_next,
                  pl.BlockSpec((3, f2), lambda i: (0, 0)), _vec_spec(f2)],
        out_specs=[_row_spec(tt, f2), pl.BlockSpec((HALO, f2), lambda i: (0, 0))],
        out_shape=[jax.ShapeDtypeStruct((t, f2), BF16), jax.ShapeDtypeStruct((HALO, f2), F32)],
        compiler_params=_params("arbitrary"),
    )(u, u, u, da, da, cw, cb)


def _gelu(x):
    cdf = 0.5 * (1.0 + lax.erf(x * INV_SQRT2))
    return x * cdf, cdf + x * (jnp.exp(-0.5 * x * x) * INV_SQRT_2PI)


def _sgu_common(p_ref, lg_ref, lb_ref, ws_ref, c, ch, groups):
    u, du_dp = _gelu(p_ref[:, :c])
    v, dv_dp = _gelu(p_ref[:, c:])
    mu = jnp.mean(v, axis=-1, keepdims=True)
    vc = v - mu
    rstd = lax.rsqrt(jnp.mean(vc * vc, axis=-1, keepdims=True) + EPS)
    xhat = vc * rstd
    vn = (xhat * lg_ref[...] + lb_ref[...]).astype(BF16)
    tril = lax.broadcasted_iota(jnp.int32, (ch, ch), 0) >= lax.broadcasted_iota(jnp.int32, (ch, ch), 1)
    wm = [jnp.where(tril, ws_ref[g], 0.0).astype(BF16) for g in range(groups)]
    return u, du_dp, dv_dp, xhat, rstd, vn, wm, tril


def _lane_pick(x, g):
    lane = lax.broadcasted_iota(jnp.int32, x.shape, 1)
    return jnp.sum(jnp.where(lane == g, x, 0.0), axis=1, keepdims=True)


def _sgu_specs(tt, c, ch, groups):
    return [_row_spec(tt, 2 * c), _vec_spec(c), _vec_spec(c),
            pl.BlockSpec((groups, ch, ch), lambda i: (0, 0, 0)), pl.BlockSpec((ch, LANES), lambda i: (0, 0))]


def _sgu_fwd(p, lg, lb, ws, bs_t, name):
    t, c2 = p.shape
    c = c2 // 2
    groups, ch, _ = ws.shape
    gc = c // groups
    tt = _tile(t, 2 * ch, ch)

    def body(p_ref, lg_ref, lb_ref, ws_ref, bs_ref, y_ref):
        u, _, _, _, _, vn, wm, _ = _sgu_common(p_ref, lg_ref, lb_ref, ws_ref, c, ch, groups)
        bs = bs_ref[...]
        for r in range(tt // ch):
            rows = slice(r * ch, (r + 1) * ch)
            for g in range(groups):
                cols = slice(g * gc, (g + 1) * gc)
                mixed = _dot_nn(wm[g], vn[rows, cols]) + _lane_pick(bs, g)
                y_ref[rows, cols] = (u[rows, cols] * mixed).astype(y_ref.dtype)

    return pl.pallas_call(
        body, name=name, grid=(t // tt,), in_specs=_sgu_specs(tt, c, ch, groups),
        out_specs=_row_spec(tt, c), out_shape=jax.ShapeDtypeStruct((t, c), BF16),
        compiler_params=_params("parallel"),
    )(p, lg, lb, ws, bs_t)


def _sgu_bwd(p, dy, lg, lb, ws, bs_t, name):
    t, c2 = p.shape
    c = c2 // 2
    groups, ch, _ = ws.shape
    gc = c // groups
    tt = _tile(t, 2 * ch, ch)

    def body(p_ref, dy_ref, lg_ref, lb_ref, ws_ref, bs_ref, dp_ref, dws_ref, dbs_ref, dlg_ref, dlb_ref, dvn_ref):
        @pl.when(pl.program_id(0) == 0)
        def _():
            dws_ref[...] = jnp.zeros_like(dws_ref)
            dbs_ref[...] = jnp.zeros_like(dbs_ref)
            dlg_ref[...] = jnp.zeros_like(dlg_ref)
            dlb_ref[...] = jnp.zeros_like(dlb_ref)

        u, du_dp, dv_dp, xhat, rstd, vn, wm, tril = _sgu_common(p_ref, lg_ref, lb_ref, ws_ref, c, ch, groups)
        bs = bs_ref[...]
        lane = lax.broadcasted_iota(jnp.int32, (ch, LANES), 1)
        for r in range(tt // ch):
            rows = slice(r * ch, (r + 1) * ch)
            for g in range(groups):
                cols = slice(g * gc, (g + 1) * gc)
                dyv, vng = dy_ref[rows, cols], vn[rows, cols]
                mixed = _dot_nn(wm[g], vng) + _lane_pick(bs, g)
                dp_ref[rows, cols] = (dyv * mixed * du_dp[rows, cols]).astype(dp_ref.dtype)
                dmixed = dyv * u[rows, cols]
                dmb = dmixed.astype(BF16)
                dws_ref[g] += jnp.where(tril, _dot_nt(dmb, vng), 0.0)
                dbs_ref[...] += jnp.where(lane == g, jnp.sum(dmixed, axis=1, keepdims=True), 0.0)
                dvn_ref[rows, cols] = _dot_tn(wm[g], dmb)
        dvn = dvn_ref[...]
        dlg_ref[...] += jnp.sum(dvn * xhat, axis=0, keepdims=True)
        dlb_ref[...] += jnp.sum(dvn, axis=0, keepdims=True)
        dxh = dvn * lg_ref[...]
        dv = rstd * (dxh - jnp.mean(dxh, axis=-1, keepdims=True) - xhat * jnp.mean(dxh * xhat, axis=-1, keepdims=True))
        dp_ref[:, c:] = (dv * dv_dp).astype(dp_ref.dtype)

    specs = _sgu_specs(tt, c, ch, groups)
    return pl.pallas_call(
        body, name=name, grid=(t // tt,),
        in_specs=[specs[0], _row_spec(tt, c)] + specs[1:],
        out_specs=[_row_spec(tt, c2), pl.BlockSpec((groups, ch, ch), lambda i: (0, 0, 0)),
                   pl.BlockSpec((ch, LANES), lambda i: (0, 0)), _vec_spec(c), _vec_spec(c)],
        out_shape=[jax.ShapeDtypeStruct((t, c2), BF16), jax.ShapeDtypeStruct((groups, ch, ch), F32),
                   jax.ShapeDtypeStruct((ch, LANES), F32), jax.ShapeDtypeStruct((1, c), F32), jax.ShapeDtypeStruct((1, c), F32)],
        scratch_shapes=[pltpu.VMEM((tt, c), F32)],
        compiler_params=_params("arbitrary"),
    )(p, dy, lg, lb, ws, bs_t)


def _split(x):
    hi = x.astype(BF16)
    return hi, (x - hi.astype(F32)).astype(BF16)


def _sb_block(q, ks, qpos, kb, tk, scale, r_carry, tri):
    z = _dot_nt(q, ks) * scale
    kpos = kb * tk + lax.broadcasted_iota(jnp.int32, (1, tk), 1)
    mask = kpos < qpos
    e = jnp.exp(-jnp.abs(z))
    lm = jnp.where(mask, -(jnp.maximum(z, 0.0) + jnp.log(1.0 + e)), 0.0)
    hi, lo = _split(lm)
    inc = _dot_nn(hi, tri) + _dot_nn(lo, tri)
    att = jnp.where(mask, jnp.exp(z + lm + (inc - lm + r_carry)), 0.0)
    return z, mask, e, inc, att


def _suffix_ones(tk):
    return (lax.broadcasted_iota(jnp.int32, (tk, tk), 0) >= lax.broadcasted_iota(jnp.int32, (tk, tk), 1)).astype(BF16)


def _attn_fwd(qkv, name):
    t, d3 = qkv.shape
    d = d3 // 3
    heads = d // HEAD_DIM
    tq = _tile(t, 256, LANES)
    nq = t // tq
    scale = HEAD_DIM ** -0.5

    def body(q_ref, k_ref, v_ref, of_ref, ob_ref):
        i = pl.program_id(1)
        q = q_ref[...]
        qpos = i * tq + lax.broadcasted_iota(jnp.int32, (tq, 1), 0)
        tri = _suffix_ones(tq)

        def step(n, carry):
            o, r_carry = carry
            kb = i - n
            rows = pl.ds(pl.multiple_of(kb * tq, tq), tq)
            _, _, _, inc, att = _sb_block(q, k_ref[rows, :], qpos, kb, tq, scale, r_carry, tri)
            ahi, alo = _split(att)
            vs = v_ref[rows, :]
            return o + _dot_nn(ahi, vs) + _dot_nn(alo, vs), r_carry + inc[:, 0:1]

        o, _ = lax.fori_loop(0, i + 1, step, (jnp.zeros((tq, HEAD_DIM), F32), jnp.zeros((tq, 1), F32)))
        of_ref[...] = o
        ob_ref[...] = o.astype(ob_ref.dtype)

    return pl.pallas_call(
        body, name=name, grid=(heads, nq),
        in_specs=[pl.BlockSpec((tq, HEAD_DIM), lambda h, i: (i, h)),
                  pl.BlockSpec((t, HEAD_DIM), lambda h, i: (0, heads + h)),
                  pl.BlockSpec((t, HEAD_DIM), lambda h, i: (0, 2 * heads + h))],
        out_specs=[pl.BlockSpec((tq, HEAD_DIM), lambda h, i: (i, h))] * 2,
        out_shape=[jax.ShapeDtypeStruct((t, d), F32), jax.ShapeDtypeStruct((t, d), BF16)],
        compiler_params=_params("parallel", "parallel"),
    )(qkv, qkv, qkv)


def _attn_bwd(qkv, do, of, name):
    t, d3 = qkv.shape
    d = d3 // 3
    heads = d // HEAD_DIM
    tq = _tile(t, 256, LANES)
    nq = t // tq
    scale = HEAD_DIM ** -0.5

    def body(q_ref, k_ref, v_ref, do_ref, of_ref, dq_ref, dk_ref, dv_ref, dk_acc, dv_acc):
        i = pl.program_id(1)

        @pl.when(i == 0)
        def _():
            dk_acc[...] = jnp.zeros_like(dk_acc)
            dv_acc[...] = jnp.zeros_like(dv_acc)

        q, dov = q_ref[...], do_ref[...]
        delta = jnp.sum(dov.astype(F32) * of_ref[...], axis=-1, keepdims=True)
        qpos = i * tq + lax.broadcasted_iota(jnp.int32, (tq, 1), 0)
        tri = _suffix_ones(tq)

        def step(n, carry):
            dq, r_carry, g_carry = carry
            kb = i - n
            rows = pl.ds(pl.multiple_of(kb * tq, tq), tq)
            ks, vs = k_ref[rows, :], v_ref[rows, :]
            z, mask, e, inc, att = _sb_block(q, ks, qpos, kb, tq, scale, r_carry, tri)
            g = _dot_nt(dov, vs) * att
            ghi, glo = _split(g)
            ginc = _dot_nn(ghi, tri) + _dot_nn(glo, tri)
            beta = jnp.where(z >= 0.0, 1.0, e) / (1.0 + e)
            dz = jnp.where(mask, g * (1.0 - beta) - (delta - g_carry - ginc) * beta, 0.0) * scale
            dzb = dz.astype(BF16)
            dk_acc[rows, :] += _dot_tn(dzb, q)
            dv_acc[rows, :] += _dot_tn(att.astype(BF16), dov)
            return dq + _dot_nn(dzb, ks), r_carry + inc[:, 0:1], g_carry + ginc[:, 0:1]

        zero = jnp.zeros((tq, 1), F32)
        dq, _, _ = lax.fori_loop(0, i + 1, step, (jnp.zeros((tq, HEAD_DIM), F32), zero, zero))
        dq_ref[...] = dq.astype(dq_ref.dtype)

        @pl.when(i == nq - 1)
        def _():
            dk_ref[...] = dk_acc[...].astype(dk_ref.dtype)
            dv_ref[...] = dv_acc[...].astype(dv_ref.dtype)

    blk = pl.BlockSpec((tq, HEAD_DIM), lambda h, i: (i, h))
    full = pl.BlockSpec((t, HEAD_DIM), lambda h, i: (0, h))
    return pl.pallas_call(
        body, name=name, grid=(heads, nq),
        in_specs=[blk, pl.BlockSpec((t, HEAD_DIM), lambda h, i: (0, heads + h)),
                  pl.BlockSpec((t, HEAD_DIM), lambda h, i: (0, 2 * heads + h)), blk, blk],
        out_specs=[blk, full, full],
        out_shape=[jax.ShapeDtypeStruct((t, d), BF16)] * 3,
        scratch_shapes=[pltpu.VMEM((t, HEAD_DIM), F32), pltpu.VMEM((t, HEAD_DIM), F32)],
        compiler_params=_params("parallel", "arbitrary"),
    )(qkv, qkv, qkv, do, of)


def _pair_add(g, pair, name):
    _, _, rh, c = g.shape
    tr = _tile(rh, 256, 16)
    core = lax.axis_index("c").astype(jnp.int32).reshape(1)

    def body(c_ref, g_ref, p_ref, o_ref):
        o_ref[...] = (g_ref[...].astype(F32) + p_ref[...].astype(F32)).astype(o_ref.dtype)

    return pl.pallas_call(
        body, name=name,
        grid_spec=pltpu.PrefetchScalarGridSpec(
            num_scalar_prefetch=1, grid=(4, rh // tr),
            in_specs=[pl.BlockSpec((None, None, tr, c), lambda j, r, cr: (j, cr[0], r, 0)),
                      pl.BlockSpec((None, tr, c), lambda j, r, cr: (j, r, 0))],
            out_specs=pl.BlockSpec((None, tr, c), lambda j, r, cr: (j, r, 0))),
        out_shape=jax.ShapeDtypeStruct((4, rh, c), BF16),
        compiler_params=_params("parallel", "parallel"),
    )(core, g, pair)


def _sum_slots(x, name):
    n, r, c = x.shape
    tr = _tile(r, 256, 16)

    def body(x_ref, o_ref):
        acc = x_ref[0].astype(F32)
        for k in range(1, n):
            acc = acc + x_ref[k].astype(F32)
        o_ref[...] = acc

    return pl.pallas_call(
        body, name=name, grid=(r // tr,),
        in_specs=[pl.BlockSpec((n, tr, c), lambda i: (0, i, 0))],
        out_specs=pl.BlockSpec((tr, c), lambda i: (i, 0)),
        out_shape=jax.ShapeDtypeStruct((r, c), F32), compiler_params=_params("parallel"),
    )(x)


def _adamw(w, g, m, v, name):
    r, c = w.shape
    tr = _tile(r, max(8, (1 << 18) // c), 8)
    c1, c2 = 1.0 - ADAM_B1 ** ADAM_STEP, 1.0 - ADAM_B2 ** ADAM_STEP

    def body(w_ref, g_ref, m_ref, v_ref, d_ref, mo_ref, vo_ref):
        gv = g_ref[...]
        mn = ADAM_B1 * m_ref[...] + (1.0 - ADAM_B1) * gv
        vn = ADAM_B2 * v_ref[...] + (1.0 - ADAM_B2) * (gv * gv)
        mo_ref[...] = mn
        vo_ref[...] = vn
        d_ref[...] = -ADAM_LR * ((mn / c1) / (jnp.sqrt(vn / c2) + ADAM_EPS) + ADAM_WD * w_ref[...])

    spec = pl.BlockSpec((tr, c), lambda i: (i, 0))
    return pl.pallas_call(
        body, name=name, grid=(r // tr,), in_specs=[spec] * 4, out_specs=[spec] * 3,
        out_shape=[jax.ShapeDtypeStruct((r, c), F32)] * 3, compiler_params=_params("parallel"),
    )(w, g, m, v)


def _place():
    x, y, c = lax.axis_index("x"), lax.axis_index("y"), lax.axis_index("c")
    chips = [(1 - x, y), (x, 1 - y), (1 - x, 1 - y)]
    return x, y, c, chips


def _remote(src, dst, send_sems, recv_sems, k, to):
    return pltpu.make_async_remote_copy(src_ref=src, dst_ref=dst, send_sem=send_sems.at[k], recv_sem=recv_sems.at[k],
                                        device_id=to, device_id_type=MESH)


def _comm_call(body, name, ins, out_shape, n_remote, n_local):
    return pl.pallas_call(
        body, name=name, in_specs=[ANY] * len(ins), out_specs=[ANY] * len(out_shape), out_shape=out_shape,
        scratch_shapes=[pltpu.SemaphoreType.DMA((n_remote,)), pltpu.SemaphoreType.DMA((n_remote,)),
                        pltpu.SemaphoreType.DMA((max(n_local, 1),))],
    )(*ins)


def _all_gather(shards, name):
    n = len(shards)

    def body(*refs):
        ins, outs = refs[:n], refs[n:2 * n]
        send_sems, recv_sems, loc_sems = refs[2 * n:]
        x, y, c, chips = _place()
        me, sib = 2 * x + y, (x, y, 1 - c)
        started, local = [], []
        for a in range(n):
            loc = pltpu.make_async_copy(ins[a], outs[a].at[me], loc_sems.at[a])
            loc.start()
            local.append(loc)
            for k, (px, py) in enumerate(chips):
                cp = _remote(ins[a].at[c], outs[a].at[me, c], send_sems, recv_sems, 6 * a + k, (px, py, c))
                cp.start()
                started.append(cp)
        for a in range(n):
            for k, (px, py) in enumerate(chips):
                land = outs[a].at[2 * px + py, c]
                _remote(land, land, send_sems, recv_sems, 6 * a + k, sib).wait_recv()
                cp = _remote(land, land, send_sems, recv_sems, 6 * a + 3 + k, sib)
                cp.start()
                started.append(cp)
        for a in range(n):
            for k, (px, py) in enumerate(chips):
                land = outs[a].at[2 * px + py, 1 - c]
                _remote(land, land, send_sems, recv_sems, 6 * a + 3 + k, sib).wait_recv()
        for cp in started:
            cp.wait_send()
        for loc in local:
            loc.wait()

    out_shape = [jax.ShapeDtypeStruct((4,) + s.shape, s.dtype) for s in shards]
    return _comm_call(body, name, shards, out_shape, 6 * n, n)


def _pair_exchange(grads, name):
    n = len(grads)

    def body(*refs):
        ins, outs = refs[:n], refs[n:2 * n]
        send_sems, recv_sems, _ = refs[2 * n:]
        x, y, c, _ = _place()
        sib = (x, y, 1 - c)
        started = []
        for a in range(n):
            for j in range(4):
                cp = _remote(ins[a].at[j, 1 - c], outs[a].at[j], send_sems, recv_sems, 4 * a + j, sib)
                cp.start()
                started.append(cp)
        for cp in started:
            cp.wait_recv()
        for cp in started:
            cp.wait_send()

    out_shape = [jax.ShapeDtypeStruct((4,) + g.shape[2:], g.dtype) for g in grads]
    return _comm_call(body, name, grads, out_shape, 4 * n, 0)


def _chip_exchange(sums, name):
    n = len(sums)

    def body(*refs):
        ins, outs = refs[:n], refs[n:2 * n]
        send_sems, recv_sems, loc_sems = refs[2 * n:]
        x, y, c, chips = _place()
        me = 2 * x + y
        started, local = [], []
        for a in range(n):
            loc = pltpu.make_async_copy(ins[a].at[me], outs[a].at[me], loc_sems.at[a])
            loc.start()
            local.append(loc)
            for k, (px, py) in enumerate(chips):
                cp = _remote(ins[a].at[2 * px + py], outs[a].at[me], send_sems, recv_sems, 3 * a + k, (px, py, c))
                cp.start()
                started.append(cp)
        for a in range(n):
            for k, (px, py) in enumerate(chips):
                land = outs[a].at[2 * px + py]
                _remote(land, land, send_sems, recv_sems, 3 * a + k, (px, py, c)).wait_recv()
        for cp in started:
            cp.wait_send()
        for loc in local:
            loc.wait()

    out_shape = [jax.ShapeDtypeStruct(s.shape, s.dtype) for s in sums]
    return _comm_call(body, name, sums, out_shape, 3 * n, n)


def _halves_exchange(groups, name):
    flat = [(w, l) for w, g in enumerate(groups) for l in range(len(g))]
    n = len(flat)

    def body(*refs):
        ins, outs = refs[:n], refs[n:n + len(groups)]
        send_sems, recv_sems, loc_sems = refs[n + len(groups):]
        x, y, c, _ = _place()
        sib = (x, y, 1 - c)
        started, local = [], []
        for a, (w, l) in enumerate(flat):
            loc = pltpu.make_async_copy(ins[a], outs[w].at[l, c], loc_sems.at[a])
            loc.start()
            local.append(loc)
            cp = _remote(ins[a], outs[w].at[l, c], send_sems, recv_sems, a, sib)
            cp.start()
            started.append(cp)
        for a, (w, l) in enumerate(flat):
            land = outs[w].at[l, 1 - c]
            _remote(land, land, send_sems, recv_sems, a, sib).wait_recv()
        for cp in started:
            cp.wait_send()
        for loc in local:
            loc.wait()

    ins = [t for g in groups for t in g]
    out_shape = [jax.ShapeDtypeStruct((len(g), 2) + g[0].shape, g[0].dtype) for g in groups]
    return _comm_call(body, name, ins, out_shape, n, n)


def _broadcast_small(buf, name):
    def body(in_ref, out_ref, send_sems, recv_sems, loc_sems):
        x, y, c, _ = _place()
        me = 4 * x + 2 * y + c
        loc = pltpu.make_async_copy(in_ref, out_ref.at[me], loc_sems.at[0])
        loc.start()
        peers = [(x ^ (k >> 2 & 1), y ^ (k >> 1 & 1), c ^ (k & 1)) for k in range(1, 8)]
        started = []
        for k, to in enumerate(peers):
            cp = _remote(in_ref, out_ref.at[me], send_sems, recv_sems, k, to)
            cp.start()
            started.append(cp)
        for k, (px, py, pc) in enumerate(peers):
            land = out_ref.at[4 * px + 2 * py + pc]
            _remote(land, land, send_sems, recv_sems, k, (px, py, pc)).wait_recv()
        for cp in started:
            cp.wait_send()
        loc.wait()

    return _comm_call(body, name, [buf], [jax.ShapeDtypeStruct((8,) + buf.shape, buf.dtype)], 7, 1)[0]


def _pack(arrays):
    flat = jnp.concatenate([a.reshape(-1).astype(F32) for a in arrays])
    pad = -flat.shape[0] % (256 * LANES)
    return jnp.pad(flat, (0, pad)).reshape(-1, LANES)


def _unpack(buf, shapes):
    flat, out, at = buf.reshape(-1), [], 0
    for s in shapes:
        size = math.prod(s)
        out.append(flat[at:at + size].reshape(s))
        at += size
    return out


def _adamw_any(w, g, m, v, name):
    shape = w.shape
    two = (-1, shape[-1]) if w.ndim >= 2 else (1, -1)
    outs = _adamw(w.reshape(two), g.reshape(two), m.reshape(two), v.reshape(two), name)
    return [o.reshape(shape) for o in outs]


BIG = ("sc_w_in", "sc_w_out", "sg_w_in", "sg_w_out", "sb_w_qkv", "sb_w_out", "ffn_w_up", "ffn_w_down")
ROW_SHARDED = ("sc_w_out", "sg_w_out", "sb_w_out", "ffn_w_down")
WEIGHTS = ("norm_mix_pre", "norm_mix_post", "norm_ffn_pre", "norm_ffn_post", "sc_w_in", "sc_conv_w", "sc_w_out",
           "sg_w_in", "sg_ln_g", "sg_ln_b", "sg_w_s", "sg_b_s", "sg_w_out", "sb_w_qkv", "sb_w_out",
           "ffn_w_up", "ffn_conv_w", "ffn_conv_b", "ffn_w_down")


def _step(p):
    t, d = p["x"].shape[1:]
    x, target = p["x"].reshape(t, d), p["loss_target"].reshape(t, d)
    depth = p["norm_mix_pre"].shape[0]
    chip = 2 * lax.axis_index("x") + lax.axis_index("y")

    ents = [(name, l) for name in BIG for l in range(p[name].shape[0])]
    shards = [p[name][l].astype(BF16) for name, l in ents]
    shards = [s.reshape(2, s.shape[0] // 2, s.shape[1]) for s in shards]
    full = {}
    for (name, l), g in zip(ents, _all_gather(shards, "gather_weights")):
        rows, cols = p[name].shape[1:]
        full[name, l] = g.reshape(1, 4 * rows, cols) if name in ROW_SHARDED else g.reshape(4, rows, cols)

    conv_shapes = [p["sc_conv_w"].shape, p["ffn_conv_w"].shape]
    conv_all = _broadcast_small(_pack([p["sc_conv_w"], p["ffn_conv_w"]]), "gather_conv_taps")[0::2]
    sc_cw, ffn_cw = [jnp.moveaxis(jnp.stack([_unpack(conv_all[j], conv_shapes)[i] for j in range(4)]), 0, 2)
                     .reshape(s[0], s[1], 4 * s[2]) for i, s in enumerate(conv_shapes)]

    ws = p["sg_w_s"]
    groups, ch = ws.shape[1], ws.shape[2]
    bs_t = [jnp.pad(p["sg_b_s"][j].T, ((0, 0), (0, LANES - groups))) for j in range(ws.shape[0])]
    g1, g2, g3, g4 = [[p[k][i:i + 1] for i in range(depth)] for k in WEIGHTS[:4]]

    saved = []
    h, hn = x, _rms_fwd(x, g1[0], "rms_in")
    for i in range(depth):
        kind, j = i % N_MIXERS, i // N_MIXERS
        if kind == 0:
            pre = _mm_nn(hn, full["sc_w_in", j], F32, f"sc_in_{i}")
            y = _sc_fwd(pre, sc_cw[j], f"sc_mix_{i}")
            w_in, w_out, extra = full["sc_w_in", j], full["sc_w_out", j], None
        elif kind == 1:
            pre = _mm_nn(hn, full["sg_w_in", j], F32, f"sg_in_{i}")
            y = _sgu_fwd(pre, p["sg_ln_g"][j:j + 1], p["sg_ln_b"][j:j + 1], ws[j], bs_t[j], f"sg_mix_{i}")
            w_in, w_out, extra = full["sg_w_in", j], full["sg_w_out", j], None
        else:
            pre = _mm_nn(hn, full["sb_w_qkv", j], BF16, f"sb_in_{i}")
            extra, y = _attn_fwd(pre, f"sb_mix_{i}")
            w_in, w_out = full["sb_w_qkv", j], full["sb_w_out", j]
        m = _mm_nn(y, w_out, F32, f"mix_out_{i}")
        h1, fn = _norm_step(h, m, g2[i], g3[i], f"norm_mid_{i}")
        up = _mm_nn(fn, full["ffn_w_up", i], F32, f"ffn_up_{i}")
        act = _ffn_fwd(up, ffn_cw[i], p["ffn_conv_b"][i:i + 1], f"ffn_act_{i}")
        f = _mm_nn(act, full["ffn_w_down", i], F32, f"ffn_down_{i}")
        saved.append(dict(h=h, hn=hn, pre=pre, y=y, extra=extra, m=m, h1=h1, fn=fn, up=up, act=act, f=f,
                          w_in=w_in, w_out=w_out))
        if i < depth - 1:
            h, hn = _norm_step(h1, f, g4[i], g1[i + 1], f"norm_end_{i}")
    loss, dh, df, dg4 = _final_loss(h1, f, g4[depth - 1], target, "loss_head")

    gbig, gsm = {}, {k: [None] * depth for k in ("g1", "g2", "g3", "g4", "ffn_cw", "ffn_cb")}
    gsm["g4"][depth - 1] = dg4
    gsm["sc_cw"] = [None] * p["sc_conv_w"].shape[0]
    for k in ("sg_lg", "sg_lb", "sg_ws", "sg_bs"):
        gsm[k] = [None] * ws.shape[0]
    for i in reversed(range(depth)):
        s = saved[i]
        kind, j = i % N_MIXERS, i // N_MIXERS
        dact = _mm_nt(df, full["ffn_w_down", i], F32, f"d_ffn_act_{i}")
        gbig["ffn_w_down", i] = _mm_tn(s["act"], df, 1, f"g_ffn_down_{i}")
        dup, dcw = _ffn_bwd(s["up"], dact, ffn_cw[i], p["ffn_conv_b"][i:i + 1], f"d_ffn_up_{i}")
        gsm["ffn_cw"][i], gsm["ffn_cb"][i] = dcw[0:3], dcw[3:4]
        dfn = _mm_nt(dup, full["ffn_w_up", i], F32, f"d_ffn_in_{i}")
        gbig["ffn_w_up", i] = _mm_tn(s["fn"], dup, 4, f"g_ffn_up_{i}")
        dh1, gsm["g3"][i], dm, gsm["g2"][i] = _norm_bwd_step(dh, dfn, s["h1"], g3[i], (s["m"], g2[i]), f"d_norm_mid_{i}")
        name_in, name_out = (("sc_w_in", "sc_w_out"), ("sg_w_in", "sg_w_out"), ("sb_w_qkv", "sb_w_out"))[kind]
        gbig[name_out, j] = _mm_tn(s["y"], dm, 1, f"g_mix_out_{i}")
        if kind == 0:
            dy = _mm_nt(dm, s["w_out"], F32, f"d_mix_y_{i}")
            dpre, dwc = _sc_bwd(s["pre"], dy, sc_cw[j], f"d_sc_mix_{i}")
            gsm["sc_cw"][j] = dwc[0:3]
        elif kind == 1:
            dy = _mm_nt(dm, s["w_out"], F32, f"d_mix_y_{i}")
            dpre, dws, dbs, dlg, dlb = _sgu_bwd(s["pre"], dy, p["sg_ln_g"][j:j + 1], p["sg_ln_b"][j:j + 1], ws[j], bs_t[j],
                                                f"d_sg_mix_{i}")
            gsm["sg_lg"][j], gsm["sg_lb"][j], gsm["sg_ws"][j], gsm["sg_bs"][j] = dlg, dlb, dws, dbs[:, :groups].T
        else:
            do = _mm_nt(dm, s["w_out"], BF16, f"d_mix_y_{i}")
            dpre = jnp.concatenate(_attn_bwd(s["pre"], do, s["extra"], f"d_sb_mix_{i}"), axis=1)
        dhn = _mm_nt(dpre, s["w_in"], F32, f"d_mix_in_{i}")
        gbig[name_in, j] = _mm_tn(s["hn"], dpre, 4, f"g_mix_in_{i}")
        if i > 0:
            dh, gsm["g1"][i], df, gsm["g4"][i - 1] = _norm_bwd_step(dh1, dhn, s["h"], g1[i],
                                                                   (saved[i - 1]["f"], g4[i - 1]), f"d_norm_in_{i}")
        else:
            dx, gsm["g1"][0] = _norm_bwd_step(dh1, dhn, s["h"], g1[0], None, "d_norm_in_0")

    parts = [gbig[e].reshape(4, 2, -1, gbig[e].shape[-1]) for e in ents]
    pair = _pair_exchange(parts, "grad_pair_exchange")
    sums = [_pair_add(g, q, f"grad_pair_add_{a}") for a, (g, q) in enumerate(zip(parts, pair))]
    slots = _chip_exchange(sums, "grad_chip_exchange")
    halves = [_sum_slots(s, f"grad_chip_sum_{a}") for a, s in enumerate(slots)]
    by_weight = [[halves[a] for a, (n, _) in enumerate(ents) if n == name] for name in BIG]
    grads = {name: g.reshape(p[name].shape) for name, g in zip(BIG, _halves_exchange(by_weight, "grad_halves_exchange"))}

    small = [jnp.concatenate(gsm[k]) for k in ("g1", "g2", "g3", "g4", "sg_lg", "sg_lb")] + [
        jnp.stack(gsm["sg_ws"]), jnp.stack(gsm["sg_bs"]),
        jnp.concatenate(gsm["ffn_cb"]), jnp.stack(gsm["sc_cw"]), jnp.stack(gsm["ffn_cw"])]
    small_shapes = [a.shape for a in small]
    total = _sum_slots(_broadcast_small(_pack(small), "small_grad_exchange"), "small_grad_sum")
    sm = _unpack(total, small_shapes)
    for k, name in enumerate(("norm_mix_pre", "norm_mix_post", "norm_ffn_pre", "norm_ffn_post", "sg_ln_g", "sg_ln_b")):
        grads[name] = sm[k].reshape(p[name].shape)
    grads["sg_w_s"], grads["sg_b_s"] = sm[6].reshape(ws.shape), sm[7].reshape(p["sg_b_s"].shape)
    grads["ffn_conv_b"] = sm[8].reshape(p["ffn_conv_b"].shape)
    for name, full_g in (("sc_conv_w", sm[9]), ("ffn_conv_w", sm[10])):
        n = p[name].shape[-1]
        grads[name] = lax.dynamic_slice_in_dim(full_g, chip * n, n, axis=2)

    delta, new_m, new_v = {}, {}, {}
    for name in WEIGHTS:
        delta[name], new_m[name], new_v[name] = _adamw_any(p[name], grads[name], p["m_" + name], p["v_" + name], f"adamw_{name}")

    loss = lax.psum(loss[0, 0], ("x", "y", "c"))
    return (loss, dx.reshape(p["x"].shape), *[grads[n] for n in WEIGHTS], *[delta[n] for n in WEIGHTS],
            *[new_m[n] for n in WEIGHTS], *[new_v[n] for n in WEIGHTS])


def kernel(x, norm_mix_pre, norm_mix_post, norm_ffn_pre, norm_ffn_post, sc_w_in, sc_conv_w, sc_w_out, sg_w_in, sg_ln_g, sg_ln_b, sg_w_s, sg_b_s, sg_w_out, sb_w_qkv, sb_w_out, ffn_w_up, ffn_conv_w, ffn_conv_b, ffn_w_down, loss_target, m_norm_mix_pre, m_norm_mix_post, m_norm_ffn_pre, m_norm_ffn_post, m_sc_w_in, m_sc_conv_w, m_sc_w_out, m_sg_w_in, m_sg_ln_g, m_sg_ln_b, m_sg_w_s, m_sg_b_s, m_sg_w_out, m_sb_w_qkv, m_sb_w_out, m_ffn_w_up, m_ffn_conv_w, m_ffn_conv_b, m_ffn_w_down, v_norm_mix_pre, v_norm_mix_post, v_norm_ffn_pre, v_norm_ffn_post, v_sc_w_in, v_sc_conv_w, v_sc_w_out, v_sg_w_in, v_sg_ln_g, v_sg_ln_b, v_sg_w_s, v_sg_b_s, v_sg_w_out, v_sb_w_qkv, v_sb_w_out, v_ffn_w_up, v_ffn_conv_w, v_ffn_conv_b, v_ffn_w_down):
    return _step(dict(locals()))
```

```python
import functools
import math

import jax
import jax.numpy as jnp
from jax import lax
from jax.experimental import pallas as pl
from jax.experimental.pallas import tpu as pltpu

F32 = jnp.float32
BF16 = jnp.bfloat16
MESH = pl.DeviceIdType.MESH
ANY = pl.BlockSpec(memory_space=pl.ANY)

EPS = 1e-6
HEAD_DIM = 128
N_MIXERS = 3
ADAM_LR, ADAM_B1, ADAM_B2, ADAM_EPS, ADAM_WD, ADAM_STEP = 0.001, 0.9, 0.999, 1e-08, 0.01, 10
V7X_VMEM_LIMIT = 56 * 1024 * 1024
HALO = 8
LANES = 128
EXP_UNDERFLOW = -104.0
INV_SQRT2 = 1.0 / math.sqrt(2.0)
INV_SQRT_2PI = 1.0 / math.sqrt(2.0 * math.pi)


def _tile(n, target, mult):
    t = min(n, target) // mult * mult
    while t >= mult:
        if n % t == 0:
            return t
        t -= mult
    return n


def _params(*sem):
    return pltpu.CompilerParams(dimension_semantics=sem, vmem_limit_bytes=V7X_VMEM_LIMIT)


def _dot(a, b, dims):
    return lax.dot_general(a, b, (dims, ((), ())), preferred_element_type=F32)


def _dot_nn(a, b):
    return _dot(a, b, ((1,), (0,)))


def _dot_nt(a, b):
    return _dot(a, b, ((1,), (1,)))


def _dot_tn(a, b):
    return _dot(a, b, ((0,), (0,)))


def _mm_nn(a, w, out_dtype, name):
    m, k = a.shape
    s, _, n = w.shape
    tm, tn, tk = _tile(m, 1024, 16), _tile(n, 1536, LANES), _tile(k, 512, LANES)
    nb, nk = n // tn, k // tk

    def body(a_ref, w_ref, o_ref, acc):
        kk = pl.program_id(2)

        @pl.when(kk == 0)
        def _():
            acc[...] = jnp.zeros_like(acc)

        acc[...] += _dot_nn(a_ref[...], w_ref[...])

        @pl.when(kk == nk - 1)
        def _():
            o_ref[...] = acc[...].astype(o_ref.dtype)

    return pl.pallas_call(
        body, name=name, grid=(m // tm, s * nb, nk),
        in_specs=[pl.BlockSpec((tm, tk), lambda i, j, kk: (i, kk)),
                  pl.BlockSpec((None, tk, tn), lambda i, j, kk: (j // nb, kk, j % nb))],
        out_specs=pl.BlockSpec((tm, tn), lambda i, j, kk: (i, j)),
        out_shape=jax.ShapeDtypeStruct((m, s * n), out_dtype),
        scratch_shapes=[pltpu.VMEM((tm, tn), F32)],
        compiler_params=_params("parallel", "parallel", "arbitrary"),
    )(a, w)


def _mm_nt(dy, w, out_dtype, name):
    m = dy.shape[0]
    s, k, n = w.shape
    tm, tn, tko = _tile(m, 1024, 16), _tile(n, 1536, LANES), _tile(k, 1024, LANES)
    nb = n // tn
    nr = s * nb

    def body(dy_ref, w_ref, o_ref, acc):
        r = pl.program_id(2)

        @pl.when(r == 0)
        def _():
            acc[...] = jnp.zeros_like(acc)

        acc[...] += _dot_nt(dy_ref[...], w_ref[...])

        @pl.when(r == nr - 1)
        def _():
            o_ref[...] = acc[...].astype(o_ref.dtype)

    return pl.pallas_call(
        body, name=name, grid=(m // tm, k // tko, nr),
        in_specs=[pl.BlockSpec((tm, tn), lambda i, j, r: (i, r)),
                  pl.BlockSpec((None, tko, tn), lambda i, j, r: (r // nb, j, r % nb))],
        out_specs=pl.BlockSpec((tm, tko), lambda i, j, r: (i, j)),
        out_shape=jax.ShapeDtypeStruct((m, k), out_dtype),
        scratch_shapes=[pltpu.VMEM((tm, tko), F32)],
        compiler_params=_params("parallel", "parallel", "arbitrary"),
    )(dy, w)


def _mm_tn(x, dy, s, name):
    t, k = x.shape
    n = dy.shape[1] // s
    tk, tn, tt = _tile(k, 512, LANES), _tile(n, 1536, LANES), _tile(t, 1024, 16)
    nb, nt = n // tn, t // tt

    def body(x_ref, dy_ref, o_ref, acc):
        tt_i = pl.program_id(2)

        @pl.when(tt_i == 0)
        def _():
            acc[...] = jnp.zeros_like(acc)

        acc[...] += _dot_tn(x_ref[...], dy_ref[...])

        @pl.when(tt_i == nt - 1)
        def _():
            o_ref[...] = acc[...].astype(o_ref.dtype)

    return pl.pallas_call(
        body, name=name, grid=(k // tk, s * nb, nt),
        in_specs=[pl.BlockSpec((tt, tk), lambda i, j, q: (q, i)),
                  pl.BlockSpec((tt, tn), lambda i, j, q: (q, j))],
        out_specs=pl.BlockSpec((None, tk, tn), lambda i, j, q: (j // nb, i, j % nb)),
        out_shape=jax.ShapeDtypeStruct((s, k, n), BF16),
        scratch_shapes=[pltpu.VMEM((tk, tn), F32)],
        compiler_params=_params("parallel", "parallel", "arbitrary"),
    )(x, dy)


def _rms(x, g):
    return x * lax.rsqrt(jnp.mean(x * x, axis=-1, keepdims=True) + EPS) * g


def _rms_bwd(x, g, dy):
    r = lax.rsqrt(jnp.mean(x * x, axis=-1, keepdims=True) + EPS)
    gy = dy * g
    dx = r * gy - x * (r * r * r * jnp.mean(x * gy, axis=-1, keepdims=True))
    return dx, dy * (x * r)


def _row_spec(tt, d):
    return pl.BlockSpec((tt, d), lambda i: (i, 0))


def _vec_spec(d):
    return pl.BlockSpec((1, d), lambda i: (0, 0))


def _rms_fwd(x, g, name):
    t, d = x.shape
    tt = _tile(t, 512, 16)

    def body(x_ref, g_ref, o_ref):
        o_ref[...] = _rms(x_ref[...], g_ref[...]).astype(o_ref.dtype)

    return pl.pallas_call(
        body, name=name, grid=(t // tt,),
        in_specs=[_row_spec(tt, d), _vec_spec(d)], out_specs=_row_spec(tt, d),
        out_shape=jax.ShapeDtypeStruct((t, d), BF16), compiler_params=_params("parallel"),
    )(x, g)


def _norm_step(h, m, g_post, g_next, name):
    t, d = h.shape
    tt = _tile(t, 512, 16)

    def body(h_ref, m_ref, gp_ref, gn_ref, ho_ref, xn_ref):
        hn = h_ref[...] + _rms(m_ref[...], gp_ref[...])
        ho_ref[...] = hn
        xn_ref[...] = _rms(hn, gn_ref[...]).astype(xn_ref.dtype)

    return pl.pallas_call(
        body, name=name, grid=(t // tt,),
        in_specs=[_row_spec(tt, d), _row_spec(tt, d), _vec_spec(d), _vec_spec(d)],
        out_specs=[_row_spec(tt, d), _row_spec(tt, d)],
        out_shape=[jax.ShapeDtypeStruct((t, d), F32), jax.ShapeDtypeStruct((t, d), BF16)],
        compiler_params=_params("parallel"),
    )(h, m, g_post, g_next)


def _final_loss(h, f, g_post, target, name):
    t, d = h.shape
    tt = _tile(t, 256, 16)

    def body(h_ref, f_ref, g_ref, tg_ref, loss_ref, dh_ref, df_ref, dg_ref):
        @pl.when(pl.program_id(0) == 0)
        def _():
            loss_ref[...] = jnp.zeros_like(loss_ref)
            dg_ref[...] = jnp.zeros_like(dg_ref)

        fv, g = f_ref[...], g_ref[...]
        err = h_ref[...] + _rms(fv, g) - tg_ref[...]
        per_row = jnp.mean(err * err, axis=-1, keepdims=True)
        loss_ref[...] += 0.5 * jnp.sum(per_row, axis=0, keepdims=True)
        dh = err * (1.0 / d)
        dh_ref[...] = dh
        df, dg = _rms_bwd(fv, g, dh)
        df_ref[...] = df.astype(df_ref.dtype)
        dg_ref[...] += jnp.sum(dg, axis=0, keepdims=True)

    return pl.pallas_call(
        body, name=name, grid=(t // tt,),
        in_specs=[_row_spec(tt, d), _row_spec(tt, d), _vec_spec(d), _row_spec(tt, d)],
        out_specs=[pl.BlockSpec((1, 1), lambda i: (0, 0)), _row_spec(tt, d), _row_spec(tt, d), _vec_spec(d)],
        out_shape=[jax.ShapeDtypeStruct((1, 1), F32), jax.ShapeDtypeStruct((t, d), F32),
                   jax.ShapeDtypeStruct((t, d), BF16), jax.ShapeDtypeStruct((1, d), F32)],
        compiler_params=_params("arbitrary"),
    )(h, f, g_post, target)


def _norm_bwd_step(dh_out, dxn, x, g_pre, prev, name):
    t, d = x.shape
    tt = _tile(t, 256, 16)
    has_prev = prev is not None

    def body(*refs):
        if has_prev:
            dho_ref, dxn_ref, x_ref, g_ref, xa_ref, ga_ref, dh_ref, dg_ref, da_ref, dga_ref = refs
        else:
            dho_ref, dxn_ref, x_ref, g_ref, dh_ref, dg_ref = refs

        @pl.when(pl.program_id(0) == 0)
        def _():
            dg_ref[...] = jnp.zeros_like(dg_ref)
            if has_prev:
                dga_ref[...] = jnp.zeros_like(dga_ref)

        dx, dg = _rms_bwd(x_ref[...], g_ref[...], dxn_ref[...])
        dh = dho_ref[...] + dx
        dh_ref[...] = dh
        dg_ref[...] += jnp.sum(dg, axis=0, keepdims=True)
        if has_prev:
            da, dga = _rms_bwd(xa_ref[...], ga_ref[...], dh)
            da_ref[...] = da.astype(da_ref.dtype)
            dga_ref[...] += jnp.sum(dga, axis=0, keepdims=True)

    ins = [dh_out, dxn, x, g_pre] + (list(prev) if has_prev else [])
    in_specs = [_row_spec(tt, d)] * 3 + [_vec_spec(d)] + ([_row_spec(tt, d), _vec_spec(d)] if has_prev else [])
    out_specs = [_row_spec(tt, d), _vec_spec(d)] + ([_row_spec(tt, d), _vec_spec(d)] if has_prev else [])
    out_shape = [jax.ShapeDtypeStruct((t, d), F32), jax.ShapeDtypeStruct((1, d), F32)]
    if has_prev:
        out_shape += [jax.ShapeDtypeStruct((t, d), BF16), jax.ShapeDtypeStruct((1, d), F32)]
    return pl.pallas_call(
        body, name=name, grid=(t // tt,), in_specs=in_specs, out_specs=out_specs, out_shape=out_shape,
        compiler_params=_params("arbitrary"),
    )(*ins)


def _shift_down(x, k, halo):
    r = pltpu.roll(x, k, 0)
    rh = pltpu.roll(halo, k, 0)
    row = lax.broadcasted_iota(jnp.int32, rh.shape, 0)
    return jnp.concatenate([jnp.where(row < k, rh, r[:HALO]), r[HALO:]], axis=0)


def _shift_up(x_ext, k, n):
    return pltpu.roll(x_ext, x_ext.shape[0] - k, 0)[:n]


def _halo_specs(tt, width, nblk):
    per = tt // HALO
    prev = pl.BlockSpec((HALO, width), lambda i: (jnp.maximum(i * per - 1, 0), 0))
    nxt = pl.BlockSpec((HALO, width), lambda i: (jnp.minimum((i + 1) * per, nblk * per - 1), 0))
    return prev, nxt


def _conv3(x, halo, w0, w1, w2):
    return w0 * _shift_down(x, 2, halo) + w1 * _shift_down(x, 1, halo) + w2 * x


def _sc_fwd(p, wc, name):
    t, c3 = p.shape
    c = c3 // 3
    tt, cc = _tile(t, 256, 16), _tile(c, 512, LANES)
    nblk = t // tt
    prev_spec, _ = _halo_specs(tt, c3, nblk)

    def body(p_ref, pp_ref, w_ref, y_ref):
        first = pl.program_id(0) == 0
        for j in range(c // cc):
            cols = slice(j * cc, (j + 1) * cc)
            gc, hv = p_ref[:, c + j * cc:c + (j + 1) * cc], p_ref[:, 2 * c + j * cc:2 * c + (j + 1) * cc]
            uh = jnp.where(first, 0.0, pp_ref[:, c + j * cc:c + (j + 1) * cc] * pp_ref[:, 2 * c + j * cc:2 * c + (j + 1) * cc])
            cv = _conv3(gc * hv, uh, w_ref[0:1, cols], w_ref[1:2, cols], w_ref[2:3, cols])
            y_ref[:, cols] = (p_ref[:, cols] * cv).astype(y_ref.dtype)

    return pl.pallas_call(
        body, name=name, grid=(nblk,),
        in_specs=[_row_spec(tt, c3), prev_spec, pl.BlockSpec((3, c), lambda i: (0, 0))],
        out_specs=_row_spec(tt, c), out_shape=jax.ShapeDtypeStruct((t, c), BF16),
        compiler_params=_params("parallel"),
    )(p, p, wc)


def _sc_bwd(p, dy, wc, name):
    t, c3 = p.shape
    c = c3 // 3
    tt, cc = _tile(t, 256, 16), _tile(c, 512, LANES)
    nblk = t // tt
    p_prev, p_next = _halo_specs(tt, c3, nblk)
    _, dy_next = _halo_specs(tt, c, nblk)

    def body(p_ref, pp_ref, pn_ref, dy_ref, dyn_ref, w_ref, dp_ref, dw_ref):
        i = pl.program_id(0)
        first, last = i == 0, i == nblk - 1

        @pl.when(first)
        def _():
            dw_ref[...] = jnp.zeros_like(dw_ref)

        for j in range(c // cc):
            a, b, d = slice(j * cc, (j + 1) * cc), slice(c + j * cc, c + (j + 1) * cc), slice(2 * c + j * cc, 2 * c + (j + 1) * cc)
            w0, w1, w2 = w_ref[0:1, a], w_ref[1:2, a], w_ref[2:3, a]
            gb, gc, hv, dyv = p_ref[:, a], p_ref[:, b], p_ref[:, d], dy_ref[:, a]
            u = gc * hv
            uh = jnp.where(first, 0.0, pp_ref[:, b] * pp_ref[:, d])
            u2, u1 = _shift_down(u, 2, uh), _shift_down(u, 1, uh)
            cv = w0 * u2 + w1 * u1 + w2 * u
            dcv = dyv * gb
            dcv_ext = jnp.concatenate([dcv, jnp.where(last, 0.0, dyn_ref[:, a] * pn_ref[:, a])], axis=0)
            du = w2 * dcv + w1 * _shift_up(dcv_ext, 1, tt) + w0 * _shift_up(dcv_ext, 2, tt)
            dp_ref[:, a] = (dyv * cv).astype(dp_ref.dtype)
            dp_ref[:, b] = (du * hv).astype(dp_ref.dtype)
            dp_ref[:, d] = (du * gc).astype(dp_ref.dtype)
            dw_ref[0:1, a] += jnp.sum(dcv * u2, axis=0, keepdims=True)
            dw_ref[1:2, a] += jnp.sum(dcv * u1, axis=0, keepdims=True)
            dw_ref[2:3, a] += jnp.sum(dcv * u, axis=0, keepdims=True)

    return pl.pallas_call(
        body, name=name, grid=(nblk,),
        in_specs=[_row_spec(tt, c3), p_prev, p_next, _row_spec(tt, c), dy_next, pl.BlockSpec((3, c), lambda i: (0, 0))],
        out_specs=[_row_spec(tt, c3), pl.BlockSpec((HALO, c), lambda i: (0, 0))],
        out_shape=[jax.ShapeDtypeStruct((t, c3), BF16), jax.ShapeDtypeStruct((HALO, c), F32)],
        compiler_params=_params("arbitrary"),
    )(p, p, p, dy, dy, wc)


def _ffn_fwd(u, cw, cb, name):
    t, f2 = u.shape
    f = f2 // 2
    tt, cc = _tile(t, 128, 16), _tile(f, 512, LANES)
    nblk = t // tt
    prev_spec, _ = _halo_specs(tt, f2, nblk)

    def body(u_ref, up_ref, w_ref, b_ref, a_ref):
        first = pl.program_id(0) == 0

        def conv(cols):
            halo = jnp.where(first, 0.0, up_ref[:, cols])
            return _conv3(u_ref[:, cols], halo, w_ref[0:1, cols], w_ref[1:2, cols], w_ref[2:3, cols]) + b_ref[:, cols]

        for j in range(f // cc):
            hg, hv = conv(slice(j * cc, (j + 1) * cc)), conv(slice(f + j * cc, f + (j + 1) * cc))
            a_ref[:, j * cc:(j + 1) * cc] = (hg * jax.nn.sigmoid(hg) * hv).astype(a_ref.dtype)

    return pl.pallas_call(
        body, name=name, grid=(nblk,),
        in_specs=[_row_spec(tt, f2), prev_spec, pl.BlockSpec((3, f2), lambda i: (0, 0)), _vec_spec(f2)],
        out_specs=_row_spec(tt, f), out_shape=jax.ShapeDtypeStruct((t, f), BF16),
        compiler_params=_params("parallel"),
    )(u, u, cw, cb)


def _ffn_bwd(u, da, cw, cb, name):
    t, f2 = u.shape
    f = f2 // 2
    tt, cc = _tile(t, 128, 16), _tile(f, 512, LANES)
    nblk = t // tt
    u_prev, u_next = _halo_specs(tt, f2, nblk)
    _, da_next = _halo_specs(tt, f, nblk)

    def body(u_ref, up_ref, un_ref, da_ref, dan_ref, w_ref, b_ref, du_ref, dw_ref):
        i = pl.program_id(0)
        first, last = i == 0, i == nblk - 1

        @pl.when(first)
        def _():
            dw_ref[...] = jnp.zeros_like(dw_ref)

        keep = jnp.where(last, 0.0, 1.0)

        def ext(cols):
            x = jnp.concatenate([u_ref[:, cols], un_ref[:, cols]], axis=0)
            halo = jnp.where(first, 0.0, up_ref[:, cols])
            x2, x1 = _shift_down(x, 2, halo), _shift_down(x, 1, halo)
            h = w_ref[0:1, cols] * x2 + w_ref[1:2, cols] * x1 + w_ref[2:3, cols] * x + b_ref[:, cols]
            return x, x1, x2, h

        def back(cols, dh_ext, x, x1, x2):
            w0, w1, w2 = w_ref[0:1, cols], w_ref[1:2, cols], w_ref[2:3, cols]
            dh = dh_ext[:tt]
            du_ref[:, cols] = (w2 * dh + w1 * _shift_up(dh_ext, 1, tt) + w0 * _shift_up(dh_ext, 2, tt)).astype(du_ref.dtype)
            dw_ref[0:1, cols] += jnp.sum(dh * x2[:tt], axis=0, keepdims=True)
            dw_ref[1:2, cols] += jnp.sum(dh * x1[:tt], axis=0, keepdims=True)
            dw_ref[2:3, cols] += jnp.sum(dh * x[:tt], axis=0, keepdims=True)
            dw_ref[3:4, cols] += jnp.sum(dh, axis=0, keepdims=True)

        row = lax.broadcasted_iota(jnp.int32, (tt + HALO, 1), 0)
        for j in range(f // cc):
            gcols, vcols = slice(j * cc, (j + 1) * cc), slice(f + j * cc, f + (j + 1) * cc)
            xg, xg1, xg2, hg = ext(gcols)
            xv, xv1, xv2, hv = ext(vcols)
            da_ext = jnp.concatenate([da_ref[:, gcols], dan_ref[:, gcols]], axis=0)
            da_ext = jnp.where(row < tt, da_ext, da_ext * keep)
            sg = jax.nn.sigmoid(hg)
            back(gcols, da_ext * hv * (sg * (1.0 + hg * (1.0 - sg))), xg, xg1, xg2)
            back(vcols, da_ext * (hg * sg), xv, xv1, xv2)

    return pl.pallas_call(
        body, name=name, grid=(nblk,),
        in_specs=[_row_spec(tt, f2), u_prev, u_next, _row_spec(tt, f), da_next,
                  pl.BlockSpec((3, f2), lambda i: (0, 0)), _vec_spec(f2)],
        out_specs=[_row_spec(tt, f2), pl.BlockSpec((HALO, f2), lambda i: (0, 0))],
        out_shape=[jax.ShapeDtypeStruct((t, f2), BF16), jax.ShapeDtypeStruct((HALO, f2), F32)],
        compiler_params=_params("arbitrary"),
    )(u, u, u, da, da, cw, cb)


def _gelu(x):
    cdf = 0.5 * (1.0 + lax.erf(x * INV_SQRT2))
    return x * cdf, cdf + x * (jnp.exp(-0.5 * x * x) * INV_SQRT_2PI)


def _sgu_common(p_ref, lg_ref, lb_ref, ws_ref, c, ch, groups):
    u, du_dp = _gelu(p_ref[:, :c])
    v, dv_dp = _gelu(p_ref[:, c:])
    mu = jnp.mean(v, axis=-1, keepdims=True)
    vc = v - mu
    rstd = lax.rsqrt(jnp.mean(vc * vc, axis=-1, keepdims=True) + EPS)
    xhat = vc * rstd
    vn = (xhat * lg_ref[...] + lb_ref[...]).astype(BF16)
    tril = lax.broadcasted_iota(jnp.int32, (ch, ch), 0) >= lax.broadcasted_iota(jnp.int32, (ch, ch), 1)
    wm = [jnp.where(tril, ws_ref[g], 0.0).astype(BF16) for g in range(groups)]
    return u, du_dp, dv_dp, xhat, rstd, vn, wm, tril


def _lane_pick(x, g):
    lane = lax.broadcasted_iota(jnp.int32, x.shape, 1)
    return jnp.sum(jnp.where(lane == g, x, 0.0), axis=1, keepdims=True)


def _sgu_specs(tt, c, ch, groups):
    return [_row_spec(tt, 2 * c), _vec_spec(c), _vec_spec(c),
            pl.BlockSpec((groups, ch, ch), lambda i: (0, 0, 0)), pl.BlockSpec((ch, LANES), lambda i: (0, 0))]


def _sgu_fwd(p, lg, lb, ws, bs_t, name):
    t, c2 = p.shape
    c = c2 // 2
    groups, ch, _ = ws.shape
    gc = c // groups
    tt = _tile(t, 2 * ch, ch)

    def body(p_ref, lg_ref, lb_ref, ws_ref, bs_ref, y_ref):
        u, _, _, _, _, vn, wm, _ = _sgu_common(p_ref, lg_ref, lb_ref, ws_ref, c, ch, groups)
        bs = bs_ref[...]
        for r in range(tt // ch):
            rows = slice(r * ch, (r + 1) * ch)
            for g in range(groups):
                cols = slice(g * gc, (g + 1) * gc)
                mixed = _dot_nn(wm[g], vn[rows, cols]) + _lane_pick(bs, g)
                y_ref[rows, cols] = (u[rows, cols] * mixed).astype(y_ref.dtype)

    return pl.pallas_call(
        body, name=name, grid=(t // tt,), in_specs=_sgu_specs(tt, c, ch, groups),
        out_specs=_row_spec(tt, c), out_shape=jax.ShapeDtypeStruct((t, c), BF16),
        compiler_params=_params("parallel"),
    )(p, lg, lb, ws, bs_t)


def _sgu_bwd(p, dy, lg, lb, ws, bs_t, name):
    t, c2 = p.shape
    c = c2 // 2
    groups, ch, _ = ws.shape
    gc = c // groups
    tt = _tile(t, 2 * ch, ch)

    def body(p_ref, dy_ref, lg_ref, lb_ref, ws_ref, bs_ref, dp_ref, dws_ref, dbs_ref, dlg_ref, dlb_ref, dvn_ref):
        @pl.when(pl.program_id(0) == 0)
        def _():
            dws_ref[...] = jnp.zeros_like(dws_ref)
            dbs_ref[...] = jnp.zeros_like(dbs_ref)
            dlg_ref[...] = jnp.zeros_like(dlg_ref)
            dlb_ref[...] = jnp.zeros_like(dlb_ref)

        u, du_dp, dv_dp, xhat, rstd, vn, wm, tril = _sgu_common(p_ref, lg_ref, lb_ref, ws_ref, c, ch, groups)
        bs = bs_ref[...]
        lane = lax.broadcasted_iota(jnp.int32, (ch, LANES), 1)
        for r in range(tt // ch):
            rows = slice(r * ch, (r + 1) * ch)
            for g in range(groups):
                cols = slice(g * gc, (g + 1) * gc)
                dyv, vng = dy_ref[rows, cols], vn[rows, cols]
                mixed = _dot_nn(wm[g], vng) + _lane_pick(bs, g)
                dp_ref[rows, cols] = (dyv * mixed * du_dp[rows, cols]).astype(dp_ref.dtype)
                dmixed = dyv * u[rows, cols]
                dmb = dmixed.astype(BF16)
                dws_ref[g] += jnp.where(tril, _dot_nt(dmb, vng), 0.0)
                dbs_ref[...] += jnp.where(lane == g, jnp.sum(dmixed, axis=1, keepdims=True), 0.0)
                dvn_ref[rows, cols] = _dot_tn(wm[g], dmb)
        dvn = dvn_ref[...]
        dlg_ref[...] += jnp.sum(dvn * xhat, axis=0, keepdims=True)
        dlb_ref[...] += jnp.sum(dvn, axis=0, keepdims=True)
        dxh = dvn * lg_ref[...]
        dv = rstd * (dxh - jnp.mean(dxh, axis=-1, keepdims=True) - xhat * jnp.mean(dxh * xhat, axis=-1, keepdims=True))
        dp_ref[:, c:] = (dv * dv_dp).astype(dp_ref.dtype)

    specs = _sgu_specs(tt, c, ch, groups)
    return pl.pallas_call(
        body, name=name, grid=(t // tt,),
        in_specs=[specs[0], _row_spec(tt, c)] + specs[1:],
        out_specs=[_row_spec(tt, c2), pl.BlockSpec((groups, ch, ch), lambda i: (0, 0, 0)),
                   pl.BlockSpec((ch, LANES), lambda i: (0, 0)), _vec_spec(c), _vec_spec(c)],
        out_shape=[jax.ShapeDtypeStruct((t, c2), BF16), jax.ShapeDtypeStruct((groups, ch, ch), F32),
                   jax.ShapeDtypeStruct((ch, LANES), F32), jax.ShapeDtypeStruct((1, c), F32), jax.ShapeDtypeStruct((1, c), F32)],
        scratch_shapes=[pltpu.VMEM((tt, c), F32)],
        compiler_params=_params("arbitrary"),
    )(p, dy, lg, lb, ws, bs_t)


def _split(x):
    hi = x.astype(BF16)
    return hi, (x - hi.astype(F32)).astype(BF16)


def _sb_block(q, ks, qpos, kb, tk, scale, r_carry, tri):
    z = _dot_nt(q, ks) * scale
    kpos = kb * tk + lax.broadcasted_iota(jnp.int32, (1, tk), 1)
    mask = kpos < qpos
    e = jnp.exp(-jnp.abs(z))
    lm = jnp.where(mask, -(jnp.maximum(z, 0.0) + jnp.log(1.0 + e)), 0.0)
    hi, lo = _split(lm)
    inc = _dot_nn(hi, tri) + _dot_nn(lo, tri)
    att = jnp.where(mask, jnp.exp(z + lm + (inc - lm + r_carry)), 0.0)
    return z, mask, e, inc, att


def _more_keys(i, carry):
    return jnp.logical_and(carry[0] <= i, carry[-1] > EXP_UNDERFLOW)


def _suffix_ones(tk):
    return (lax.broadcasted_iota(jnp.int32, (tk, tk), 0) >= lax.broadcasted_iota(jnp.int32, (tk, tk), 1)).astype(BF16)


def _attn_fwd(qkv, name):
    t, d3 = qkv.shape
    d = d3 // 3
    heads = d // HEAD_DIM
    tq = _tile(t, 256, LANES)
    nq = t // tq
    scale = HEAD_DIM ** -0.5

    def body(q_ref, k_ref, v_ref, of_ref, ob_ref):
        i = pl.program_id(1)
        q = q_ref[...]
        qpos = i * tq + lax.broadcasted_iota(jnp.int32, (tq, 1), 0)
        tri = _suffix_ones(tq)

        def step(carry):
            n, o, r_carry, _ = carry
            kb = i - n
            rows = pl.ds(pl.multiple_of(kb * tq, tq), tq)
            _, _, _, inc, att = _sb_block(q, k_ref[rows, :], qpos, kb, tq, scale, r_carry, tri)
            ahi, alo = _split(att)
            vs = v_ref[rows, :]
            r_new = r_carry + inc[:, 0:1]
            return n + 1, o + _dot_nn(ahi, vs) + _dot_nn(alo, vs), r_new, jnp.max(r_new)

        init = (jnp.int32(0), jnp.zeros((tq, HEAD_DIM), F32), jnp.zeros((tq, 1), F32), jnp.float32(0.0))
        _, o, _, _ = lax.while_loop(functools.partial(_more_keys, i), step, init)
        of_ref[...] = o
        ob_ref[...] = o.astype(ob_ref.dtype)

    return pl.pallas_call(
        body, name=name, grid=(heads, nq),
        in_specs=[pl.BlockSpec((tq, HEAD_DIM), lambda h, i: (i, h)),
                  pl.BlockSpec((t, HEAD_DIM), lambda h, i: (0, heads + h)),
                  pl.BlockSpec((t, HEAD_DIM), lambda h, i: (0, 2 * heads + h))],
        out_specs=[pl.BlockSpec((tq, HEAD_DIM), lambda h, i: (i, h))] * 2,
        out_shape=[jax.ShapeDtypeStruct((t, d), F32), jax.ShapeDtypeStruct((t, d), BF16)],
        compiler_params=_params("parallel", "parallel"),
    )(qkv, qkv, qkv)


def _attn_bwd(qkv, do, of, name):
    t, d3 = qkv.shape
    d = d3 // 3
    heads = d // HEAD_DIM
    tq = _tile(t, 256, LANES)
    nq = t // tq
    scale = HEAD_DIM ** -0.5

    def body(q_ref, k_ref, v_ref, do_ref, of_ref, dq_ref, dk_ref, dv_ref, dk_acc, dv_acc):
        i = pl.program_id(1)

        @pl.when(i == 0)
        def _():
            dk_acc[...] = jnp.zeros_like(dk_acc)
            dv_acc[...] = jnp.zeros_like(dv_acc)

        q, dov = q_ref[...], do_ref[...]
        delta = jnp.sum(dov.astype(F32) * of_ref[...], axis=-1, keepdims=True)
        qpos = i * tq + lax.broadcasted_iota(jnp.int32, (tq, 1), 0)
        tri = _suffix_ones(tq)

        def step(carry):
            n, dq, r_carry, g_carry, _ = carry
            kb = i - n
            rows = pl.ds(pl.multiple_of(kb * tq, tq), tq)
            ks, vs = k_ref[rows, :], v_ref[rows, :]
            z, mask, e, inc, att = _sb_block(q, ks, qpos, kb, tq, scale, r_carry, tri)
            g = _dot_nt(dov, vs) * att
            ghi, glo = _split(g)
            ginc = _dot_nn(ghi, tri) + _dot_nn(glo, tri)
            beta = jnp.where(z >= 0.0, 1.0, e) / (1.0 + e)
            dz = jnp.where(mask, g * (1.0 - beta) - (delta - g_carry - ginc) * beta, 0.0) * scale
            dzb = dz.astype(BF16)
            dk_acc[rows, :] += _dot_tn(dzb, q)
            dv_acc[rows, :] += _dot_tn(att.astype(BF16), dov)
            r_new = r_carry + inc[:, 0:1]
            return n + 1, dq + _dot_nn(dzb, ks), r_new, g_carry + ginc[:, 0:1], jnp.max(r_new)

        zero = jnp.zeros((tq, 1), F32)
        init = (jnp.int32(0), jnp.zeros((tq, HEAD_DIM), F32), zero, zero, jnp.float32(0.0))
        _, dq, _, _, _ = lax.while_loop(functools.partial(_more_keys, i), step, init)
        dq_ref[...] = dq.astype(dq_ref.dtype)

        @pl.when(i == nq - 1)
        def _():
            dk_ref[...] = dk_acc[...].astype(dk_ref.dtype)
            dv_ref[...] = dv_acc[...].astype(dv_ref.dtype)

    blk = pl.BlockSpec((tq, HEAD_DIM), lambda h, i: (i, h))
    full = pl.BlockSpec((t, HEAD_DIM), lambda h, i: (0, h))
    return pl.pallas_call(
        body, name=name, grid=(heads, nq),
        in_specs=[blk, pl.BlockSpec((t, HEAD_DIM), lambda h, i: (0, heads + h)),
                  pl.BlockSpec((t, HEAD_DIM), lambda h, i: (0, 2 * heads + h)), blk, blk],
        out_specs=[blk, full, full],
        out_shape=[jax.ShapeDtypeStruct((t, d), BF16)] * 3,
        scratch_shapes=[pltpu.VMEM((t, HEAD_DIM), F32), pltpu.VMEM((t, HEAD_DIM), F32)],
        compiler_params=_params("parallel", "arbitrary"),
    )(qkv, qkv, qkv, do, of)


def _pair_add(g, pair, name):
    _, _, rh, c = g.shape
    tr = _tile(rh, 256, 16)
    core = lax.axis_index("c").astype(jnp.int32).reshape(1)

    def body(c_ref, g_ref, p_ref, o_ref):
        o_ref[...] = (g_ref[...].astype(F32) + p_ref[...].astype(F32)).astype(o_ref.dtype)

    return pl.pallas_call(
        body, name=name,
        grid_spec=pltpu.PrefetchScalarGridSpec(
            num_scalar_prefetch=1, grid=(4, rh // tr),
            in_specs=[pl.BlockSpec((None, None, tr, c), lambda j, r, cr: (j, cr[0], r, 0)),
                      pl.BlockSpec((None, tr, c), lambda j, r, cr: (j, r, 0))],
            out_specs=pl.BlockSpec((None, tr, c), lambda j, r, cr: (j, r, 0))),
        out_shape=jax.ShapeDtypeStruct((4, rh, c), BF16),
        compiler_params=_params("parallel", "parallel"),
    )(core, g, pair)


def _sum_slots(x, name):
    n, r, c = x.shape
    tr = _tile(r, 256, 16)

    def body(x_ref, o_ref):
        acc = x_ref[0].astype(F32)
        for k in range(1, n):
            acc = acc + x_ref[k].astype(F32)
        o_ref[...] = acc

    return pl.pallas_call(
        body, name=name, grid=(r // tr,),
        in_specs=[pl.BlockSpec((n, tr, c), lambda i: (0, i, 0))],
        out_specs=pl.BlockSpec((tr, c), lambda i: (i, 0)),
        out_shape=jax.ShapeDtypeStruct((r, c), F32), compiler_params=_params("parallel"),
    )(x)


def _sum_own_slots(sums, slots, name):
    _, rh, c = sums.shape
    tr = _tile(rh, 256, 16)
    chip = (2 * lax.axis_index("x") + lax.axis_index("y")).astype(jnp.int32).reshape(1)

    def body(chip_ref, own_ref, slot_ref, o_ref):
        acc = own_ref[...].astype(F32)
        for k in range(3):
            acc = acc + slot_ref[k].astype(F32)
        o_ref[...] = acc

    return pl.pallas_call(
        body, name=name,
        grid_spec=pltpu.PrefetchScalarGridSpec(
            num_scalar_prefetch=1, grid=(rh // tr,),
            in_specs=[pl.BlockSpec((None, tr, c), lambda r, me: (me[0], r, 0)),
                      pl.BlockSpec((3, tr, c), lambda r, me: (0, r, 0))],
            out_specs=pl.BlockSpec((tr, c), lambda r, me: (r, 0))),
        out_shape=jax.ShapeDtypeStruct((rh, c), F32), compiler_params=_params("parallel"),
    )(chip, sums, slots)


def _adam_update(w, g, m, v):
    mn = ADAM_B1 * m + (1.0 - ADAM_B1) * g
    vn = ADAM_B2 * v + (1.0 - ADAM_B2) * (g * g)
    c1, c2 = 1.0 - ADAM_B1 ** ADAM_STEP, 1.0 - ADAM_B2 ** ADAM_STEP
    return -ADAM_LR * ((mn / c1) / (jnp.sqrt(vn / c2) + ADAM_EPS) + ADAM_WD * w), mn, vn


def _adamw_halves(w, own, recv, m, v, name):
    layers, _, rh, c = w.shape
    tr = _tile(rh, max(8, (1 << 18) // c), 8)
    core = lax.axis_index("c").astype(jnp.int32).reshape(1)

    def body(c_ref, w_ref, own_ref, recv_ref, m_ref, v_ref, g_ref, d_ref, mo_ref, vo_ref):
        g = jnp.where(pl.program_id(1) == c_ref[0], own_ref[...], recv_ref[...])
        g_ref[...] = g
        d_ref[...], mo_ref[...], vo_ref[...] = _adam_update(w_ref[...], g, m_ref[...], v_ref[...])

    full = pl.BlockSpec((None, None, tr, c), lambda l, hf, r, cr: (l, hf, r, 0))
    half = pl.BlockSpec((None, tr, c), lambda l, hf, r, cr: (l, r, 0))
    return pl.pallas_call(
        body, name=name,
        grid_spec=pltpu.PrefetchScalarGridSpec(
            num_scalar_prefetch=1, grid=(layers, 2, rh // tr),
            in_specs=[full, half, half, full, full], out_specs=[full] * 4),
        out_shape=[jax.ShapeDtypeStruct(w.shape, F32)] * 4,
        compiler_params=_params("parallel", "parallel", "parallel"),
    )(core, w, own, recv, m, v)


def _adamw(w, g, m, v, name):
    r, c = w.shape
    tr = _tile(r, max(8, (1 << 18) // c), 8)

    def body(w_ref, g_ref, m_ref, v_ref, d_ref, mo_ref, vo_ref):
        d_ref[...], mo_ref[...], vo_ref[...] = _adam_update(w_ref[...], g_ref[...], m_ref[...], v_ref[...])

    spec = pl.BlockSpec((tr, c), lambda i: (i, 0))
    return pl.pallas_call(
        body, name=name, grid=(r // tr,), in_specs=[spec] * 4, out_specs=[spec] * 3,
        out_shape=[jax.ShapeDtypeStruct((r, c), F32)] * 3, compiler_params=_params("parallel"),
    )(w, g, m, v)


def _place():
    x, y, c = lax.axis_index("x"), lax.axis_index("y"), lax.axis_index("c")
    chips = [(1 - x, y), (x, 1 - y), (1 - x, 1 - y)]
    return x, y, c, chips


def _remote(src, dst, send_sems, recv_sems, k, to):
    return pltpu.make_async_remote_copy(src_ref=src, dst_ref=dst, send_sem=send_sems.at[k], recv_sem=recv_sems.at[k],
                                        device_id=to, device_id_type=MESH)


def _comm_call(body, name, ins, out_shape, n_remote, n_local=0):
    sems = [pltpu.SemaphoreType.DMA((n_remote,)), pltpu.SemaphoreType.DMA((n_remote,))]
    if n_local:
        sems.append(pltpu.SemaphoreType.DMA((n_local,)))
    return pl.pallas_call(
        body, name=name, in_specs=[ANY] * len(ins), out_specs=[ANY] * len(out_shape), out_shape=out_shape,
        scratch_shapes=sems,
    )(*ins)


def _all_gather(shards, name):
    n = len(shards)

    def body(*refs):
        ins, outs = refs[:n], refs[n:2 * n]
        send_sems, recv_sems = refs[2 * n:]
        x, y, c, chips = _place()
        me, sib = 2 * x + y, (x, y, 1 - c)
        started = []
        for a in range(n):
            cp = _remote(ins[a], outs[a].at[me], send_sems, recv_sems, 7 * a + 6, sib)
            cp.start()
            started.append(cp)
            for k, (px, py) in enumerate(chips):
                cp = _remote(ins[a].at[c], outs[a].at[me, c], send_sems, recv_sems, 7 * a + k, (px, py, c))
                cp.start()
                started.append(cp)
        for a in range(n):
            for k, (px, py) in enumerate(chips):
                land = outs[a].at[2 * px + py, c]
                _remote(land, land, send_sems, recv_sems, 7 * a + k, sib).wait_recv()
                cp = _remote(land, land, send_sems, recv_sems, 7 * a + 3 + k, sib)
                cp.start()
                started.append(cp)
        for a in range(n):
            for k, (px, py) in enumerate(chips):
                land = outs[a].at[2 * px + py, 1 - c]
                _remote(land, land, send_sems, recv_sems, 7 * a + 3 + k, sib).wait_recv()
            own = outs[a].at[me]
            _remote(own, own, send_sems, recv_sems, 7 * a + 6, sib).wait_recv()
        for cp in started:
            cp.wait_send()

    out_shape = [jax.ShapeDtypeStruct((4,) + s.shape, s.dtype) for s in shards]
    return _comm_call(body, name, shards, out_shape, 7 * n)


def _pair_exchange(grads, name):
    n = len(grads)

    def body(*refs):
        ins, outs = refs[:n], refs[n:2 * n]
        send_sems, recv_sems = refs[2 * n:]
        x, y, c, _ = _place()
        sib = (x, y, 1 - c)
        started = []
        for a in range(n):
            for j in range(4):
                cp = _remote(ins[a].at[j, 1 - c], outs[a].at[j], send_sems, recv_sems, 4 * a + j, sib)
                cp.start()
                started.append(cp)
        for cp in started:
            cp.wait_recv()
        for cp in started:
            cp.wait_send()

    out_shape = [jax.ShapeDtypeStruct((4,) + g.shape[2:], g.dtype) for g in grads]
    return _comm_call(body, name, grads, out_shape, 4 * n)


def _chip_exchange(sums, name):
    n = len(sums)

    def body(*refs):
        ins, outs = refs[:n], refs[n:2 * n]
        send_sems, recv_sems = refs[2 * n:]
        _, _, c, chips = _place()
        started = []
        for a in range(n):
            for k, (px, py) in enumerate(chips):
                cp = _remote(ins[a].at[2 * px + py], outs[a].at[k], send_sems, recv_sems, 3 * a + k, (px, py, c))
                cp.start()
                started.append(cp)
        for cp in started:
            cp.wait_recv()
        for cp in started:
            cp.wait_send()

    out_shape = [jax.ShapeDtypeStruct((3,) + s.shape[1:], s.dtype) for s in sums]
    return _comm_call(body, name, sums, out_shape, 3 * n)


def _halves_exchange(own, name):
    n = len(own)

    def body(*refs):
        ins, outs = refs[:n], refs[n:2 * n]
        send_sems, recv_sems = refs[2 * n:]
        x, y, c, _ = _place()
        started = [_remote(ins[a], outs[a], send_sems, recv_sems, a, (x, y, 1 - c)) for a in range(n)]
        for cp in started:
            cp.start()
        for cp in started:
            cp.wait_recv()
        for cp in started:
            cp.wait_send()

    return _comm_call(body, name, own, [jax.ShapeDtypeStruct(t.shape, t.dtype) for t in own], n)


def _broadcast_small(buf, name):
    def body(in_ref, out_ref, send_sems, recv_sems, loc_sems):
        x, y, c, _ = _place()
        me = 4 * x + 2 * y + c
        loc = pltpu.make_async_copy(in_ref, out_ref.at[me], loc_sems.at[0])
        loc.start()
        peers = [(x ^ (k >> 2 & 1), y ^ (k >> 1 & 1), c ^ (k & 1)) for k in range(1, 8)]
        started = []
        for k, to in enumerate(peers):
            cp = _remote(in_ref, out_ref.at[me], send_sems, recv_sems, k, to)
            cp.start()
            started.append(cp)
        for k, (px, py, pc) in enumerate(peers):
            land = out_ref.at[4 * px + 2 * py + pc]
            _remote(land, land, send_sems, recv_sems, k, (px, py, pc)).wait_recv()
        for cp in started:
            cp.wait_send()
        loc.wait()

    return _comm_call(body, name, [buf], [jax.ShapeDtypeStruct((8,) + buf.shape, buf.dtype)], 7, 1)[0]


def _pack(arrays):
    flat = jnp.concatenate([a.reshape(-1).astype(F32) for a in arrays])
    pad = -flat.shape[0] % (256 * LANES)
    return jnp.pad(flat, (0, pad)).reshape(-1, LANES)


def _unpack(buf, shapes):
    flat, out, at = buf.reshape(-1), [], 0
    for s in shapes:
        size = math.prod(s)
        out.append(flat[at:at + size].reshape(s))
        at += size
    return out


def _adamw_any(w, g, m, v, name):
    shape = w.shape
    two = (-1, shape[-1]) if w.ndim >= 2 else (1, -1)
    outs = _adamw(w.reshape(two), g.reshape(two), m.reshape(two), v.reshape(two), name)
    return [o.reshape(shape) for o in outs]


BIG = ("sc_w_in", "sc_w_out", "sg_w_in", "sg_w_out", "sb_w_qkv", "sb_w_out", "ffn_w_up", "ffn_w_down")
ROW_SHARDED = ("sc_w_out", "sg_w_out", "sb_w_out", "ffn_w_down")
WEIGHTS = ("norm_mix_pre", "norm_mix_post", "norm_ffn_pre", "norm_ffn_post", "sc_w_in", "sc_conv_w", "sc_w_out",
           "sg_w_in", "sg_ln_g", "sg_ln_b", "sg_w_s", "sg_b_s", "sg_w_out", "sb_w_qkv", "sb_w_out",
           "ffn_w_up", "ffn_conv_w", "ffn_conv_b", "ffn_w_down")


def _step(p):
    t, d = p["x"].shape[1:]
    x, target = p["x"].reshape(t, d), p["loss_target"].reshape(t, d)
    depth = p["norm_mix_pre"].shape[0]
    chip = 2 * lax.axis_index("x") + lax.axis_index("y")

    ents = [(name, l) for name in BIG for l in range(p[name].shape[0])]
    shards = [p[name][l].astype(BF16) for name, l in ents]
    shards = [s.reshape(2, s.shape[0] // 2, s.shape[1]) for s in shards]
    full = {}
    for (name, l), g in zip(ents, _all_gather(shards, "gather_weights")):
        rows, cols = p[name].shape[1:]
        full[name, l] = g.reshape(1, 4 * rows, cols) if name in ROW_SHARDED else g.reshape(4, rows, cols)

    conv_shapes = [p["sc_conv_w"].shape, p["ffn_conv_w"].shape]
    conv_all = _broadcast_small(_pack([p["sc_conv_w"], p["ffn_conv_w"]]), "gather_conv_taps")[0::2]
    sc_cw, ffn_cw = [jnp.moveaxis(jnp.stack([_unpack(conv_all[j], conv_shapes)[i] for j in range(4)]), 0, 2)
                     .reshape(s[0], s[1], 4 * s[2]) for i, s in enumerate(conv_shapes)]

    ws = p["sg_w_s"]
    groups, ch = ws.shape[1], ws.shape[2]
    bs_t = [jnp.pad(p["sg_b_s"][j].T, ((0, 0), (0, LANES - groups))) for j in range(ws.shape[0])]
    g1, g2, g3, g4 = [[p[k][i:i + 1] for i in range(depth)] for k in WEIGHTS[:4]]

    saved = []
    h, hn = x, _rms_fwd(x, g1[0], "rms_in")
    for i in range(depth):
        kind, j = i % N_MIXERS, i // N_MIXERS
        if kind == 0:
            pre = _mm_nn(hn, full["sc_w_in", j], F32, f"sc_in_{i}")
            y = _sc_fwd(pre, sc_cw[j], f"sc_mix_{i}")
            w_in, w_out, extra = full["sc_w_in", j], full["sc_w_out", j], None
        elif kind == 1:
            pre = _mm_nn(hn, full["sg_w_in", j], F32, f"sg_in_{i}")
            y = _sgu_fwd(pre, p["sg_ln_g"][j:j + 1], p["sg_ln_b"][j:j + 1], ws[j], bs_t[j], f"sg_mix_{i}")
            w_in, w_out, extra = full["sg_w_in", j], full["sg_w_out", j], None
        else:
            pre = _mm_nn(hn, full["sb_w_qkv", j], BF16, f"sb_in_{i}")
            extra, y = _attn_fwd(pre, f"sb_mix_{i}")
            w_in, w_out = full["sb_w_qkv", j], full["sb_w_out", j]
        m = _mm_nn(y, w_out, F32, f"mix_out_{i}")
        h1, fn = _norm_step(h, m, g2[i], g3[i], f"norm_mid_{i}")
        up = _mm_nn(fn, full["ffn_w_up", i], F32, f"ffn_up_{i}")
        act = _ffn_fwd(up, ffn_cw[i], p["ffn_conv_b"][i:i + 1], f"ffn_act_{i}")
        f = _mm_nn(act, full["ffn_w_down", i], F32, f"ffn_down_{i}")
        saved.append(dict(h=h, hn=hn, pre=pre, y=y, extra=extra, m=m, h1=h1, fn=fn, up=up, act=act, f=f,
                          w_in=w_in, w_out=w_out))
        if i < depth - 1:
            h, hn = _norm_step(h1, f, g4[i], g1[i + 1], f"norm_end_{i}")
    loss, dh, df, dg4 = _final_loss(h1, f, g4[depth - 1], target, "loss_head")

    gbig, gsm = {}, {k: [None] * depth for k in ("g1", "g2", "g3", "g4", "ffn_cw", "ffn_cb")}
    gsm["g4"][depth - 1] = dg4
    gsm["sc_cw"] = [None] * p["sc_conv_w"].shape[0]
    for k in ("sg_lg", "sg_lb", "sg_ws", "sg_bs"):
        gsm[k] = [None] * ws.shape[0]
    for i in reversed(range(depth)):
        s = saved[i]
        kind, j = i % N_MIXERS, i // N_MIXERS
        dact = _mm_nt(df, full["ffn_w_down", i], F32, f"d_ffn_act_{i}")
        gbig["ffn_w_down", i] = _mm_tn(s["act"], df, 1, f"g_ffn_down_{i}")
        dup, dcw = _ffn_bwd(s["up"], dact, ffn_cw[i], p["ffn_conv_b"][i:i + 1], f"d_ffn_up_{i}")
        gsm["ffn_cw"][i], gsm["ffn_cb"][i] = dcw[0:3], dcw[3:4]
        dfn = _mm_nt(dup, full["ffn_w_up", i], F32, f"d_ffn_in_{i}")
        gbig["ffn_w_up", i] = _mm_tn(s["fn"], dup, 4, f"g_ffn_up_{i}")
        dh1, gsm["g3"][i], dm, gsm["g2"][i] = _norm_bwd_step(dh, dfn, s["h1"], g3[i], (s["m"], g2[i]), f"d_norm_mid_{i}")
        name_in, name_out = (("sc_w_in", "sc_w_out"), ("sg_w_in", "sg_w_out"), ("sb_w_qkv", "sb_w_out"))[kind]
        gbig[name_out, j] = _mm_tn(s["y"], dm, 1, f"g_mix_out_{i}")
        if kind == 0:
            dy = _mm_nt(dm, s["w_out"], F32, f"d_mix_y_{i}")
            dpre, dwc = _sc_bwd(s["pre"], dy, sc_cw[j], f"d_sc_mix_{i}")
            gsm["sc_cw"][j] = dwc[0:3]
        elif kind == 1:
            dy = _mm_nt(dm, s["w_out"], F32, f"d_mix_y_{i}")
            dpre, dws, dbs, dlg, dlb = _sgu_bwd(s["pre"], dy, p["sg_ln_g"][j:j + 1], p["sg_ln_b"][j:j + 1], ws[j], bs_t[j],
                                                f"d_sg_mix_{i}")
            gsm["sg_lg"][j], gsm["sg_lb"][j], gsm["sg_ws"][j], gsm["sg_bs"][j] = dlg, dlb, dws, dbs[:, :groups].T
        else:
            do = _mm_nt(dm, s["w_out"], BF16, f"d_mix_y_{i}")
            dpre = jnp.concatenate(_attn_bwd(s["pre"], do, s["extra"], f"d_sb_mix_{i}"), axis=1)
        dhn = _mm_nt(dpre, s["w_in"], F32, f"d_mix_in_{i}")
        gbig[name_in, j] = _mm_tn(s["hn"], dpre, 4, f"g_mix_in_{i}")
        if i > 0:
            dh, gsm["g1"][i], df, gsm["g4"][i - 1] = _norm_bwd_step(dh1, dhn, s["h"], g1[i],
                                                                   (saved[i - 1]["f"], g4[i - 1]), f"d_norm_in_{i}")
        else:
            dx, gsm["g1"][0] = _norm_bwd_step(dh1, dhn, s["h"], g1[0], None, "d_norm_in_0")

    parts = [gbig[e].reshape(4, 2, -1, gbig[e].shape[-1]) for e in ents]
    pair = _pair_exchange(parts, "grad_pair_exchange")
    sums = [_pair_add(g, q, f"grad_pair_add_{a}") for a, (g, q) in enumerate(zip(parts, pair))]
    slots = _chip_exchange(sums, "grad_chip_exchange")
    halves = [_sum_own_slots(s, q, f"grad_chip_sum_{a}") for a, (s, q) in enumerate(zip(sums, slots))]
    own = [jnp.stack([halves[a] for a, (n, _) in enumerate(ents) if n == name]) for name in BIG]
    recv = _halves_exchange(own, "grad_halves_exchange")
    grads, delta, new_m, new_v = {}, {}, {}, {}
    for name, mine, theirs in zip(BIG, own, recv):
        shape = p[name].shape
        view = (shape[0], 2, shape[1] // 2, shape[2])
        outs = _adamw_halves(p[name].reshape(view), mine, theirs, p["m_" + name].reshape(view), p["v_" + name].reshape(view),
                             f"adamw_{name}")
        grads[name], delta[name], new_m[name], new_v[name] = [o.reshape(shape) for o in outs]

    small = [jnp.concatenate(gsm[k]) for k in ("g1", "g2", "g3", "g4", "sg_lg", "sg_lb")] + [
        jnp.stack(gsm["sg_ws"]), jnp.stack(gsm["sg_bs"]),
        jnp.concatenate(gsm["ffn_cb"]), jnp.stack(gsm["sc_cw"]), jnp.stack(gsm["ffn_cw"])]
    small_shapes = [a.shape for a in small]
    total = _sum_slots(_broadcast_small(_pack(small), "small_grad_exchange"), "small_grad_sum")
    sm = _unpack(total, small_shapes)
    for k, name in enumerate(("norm_mix_pre", "norm_mix_post", "norm_ffn_pre", "norm_ffn_post", "sg_ln_g", "sg_ln_b")):
        grads[name] = sm[k].reshape(p[name].shape)
    grads["sg_w_s"], grads["sg_b_s"] = sm[6].reshape(ws.shape), sm[7].reshape(p["sg_b_s"].shape)
    grads["ffn_conv_b"] = sm[8].reshape(p["ffn_conv_b"].shape)
    for name, full_g in (("sc_conv_w", sm[9]), ("ffn_conv_w", sm[10])):
        n = p[name].shape[-1]
        grads[name] = lax.dynamic_slice_in_dim(full_g, chip * n, n, axis=2)

    for name in WEIGHTS:
        if name in BIG:
            continue
        delta[name], new_m[name], new_v[name] = _adamw_any(p[name], grads[name], p["m_" + name], p["v_" + name], f"adamw_{name}")

    loss = lax.psum(loss[0, 0], ("x", "y", "c"))
    return (loss, dx.reshape(p["x"].shape), *[grads[n] for n in WEIGHTS], *[delta[n] for n in WEIGHTS],
            *[new_m[n] for n in WEIGHTS], *[new_v[n] for n in WEIGHTS])


def kernel(x, norm_mix_pre, norm_mix_post, norm_ffn_pre, norm_ffn_post, sc_w_in, sc_conv_w, sc_w_out, sg_w_in, sg_ln_g, sg_ln_b, sg_w_s, sg_b_s, sg_w_out, sb_w_qkv, sb_w_out, ffn_w_up, ffn_conv_w, ffn_conv_b, ffn_w_down, loss_target, m_norm_mix_pre, m_norm_mix_post, m_norm_ffn_pre, m_norm_ffn_post, m_sc_w_in, m_sc_conv_w, m_sc_w_out, m_sg_w_in, m_sg_ln_g, m_sg_ln_b, m_sg_w_s, m_sg_b_s, m_sg_w_out, m_sb_w_qkv, m_sb_w_out, m_ffn_w_up, m_ffn_conv_w, m_ffn_conv_b, m_ffn_w_down, v_norm_mix_pre, v_norm_mix_post, v_norm_ffn_pre, v_norm_ffn_post, v_sc_w_in, v_sc_conv_w, v_sc_w_out, v_sg_w_in, v_sg_ln_g, v_sg_ln_b, v_sg_w_s, v_sg_b_s, v_sg_w_out, v_sb_w_qkv, v_sb_w_out, v_ffn_w_up, v_ffn_conv_w, v_ffn_conv_b, v_ffn_w_down):
    return _step(dict(locals()))
```

```python
import functools
import math

import jax
import jax.numpy as jnp
from jax import lax
from jax.experimental import pallas as pl
from jax.experimental.pallas import tpu as pltpu

F32 = jnp.float32
BF16 = jnp.bfloat16
MESH = pl.DeviceIdType.MESH
ANY = pl.BlockSpec(memory_space=pl.ANY)

EPS = 1e-6
HEAD_DIM = 128
N_MIXERS = 3
ADAM_LR, ADAM_B1, ADAM_B2, ADAM_EPS, ADAM_WD, ADAM_STEP = 0.001, 0.9, 0.999, 1e-08, 0.01, 10
V7X_VMEM_LIMIT = 56 * 1024 * 1024
HALO = 8
LANES = 128
EXP_UNDERFLOW = -104.0
INV_SQRT2 = 1.0 / math.sqrt(2.0)
INV_SQRT_2PI = 1.0 / math.sqrt(2.0 * math.pi)


def _tile(n, target, mult):
    t = min(n, target) // mult * mult
    while t >= mult:
        if n % t == 0:
            return t
        t -= mult
    return n


def _params(*sem):
    return pltpu.CompilerParams(dimension_semantics=sem, vmem_limit_bytes=V7X_VMEM_LIMIT)


def _dot(a, b, dims):
    return lax.dot_general(a, b, (dims, ((), ())), preferred_element_type=F32)


def _dot_nn(a, b):
    return _dot(a, b, ((1,), (0,)))


def _dot_nt(a, b):
    return _dot(a, b, ((1,), (1,)))


def _dot_tn(a, b):
    return _dot(a, b, ((0,), (0,)))


class Side:
    def __init__(self, ins, out_shape, n, copies, aliases=None):
        self.ins, self.out_shape, self.n, self.copies, self.aliases = list(ins), list(out_shape), n, copies, aliases or {}


def _grid_call(core, name, grid, in_specs, out_spec, out_shape, scratch, operands, side):
    if side is None:
        return pl.pallas_call(
            core, name=name, grid=grid, in_specs=in_specs, out_specs=out_spec, out_shape=out_shape, scratch_shapes=scratch,
            compiler_params=_params("parallel", "parallel", "arbitrary"))(*operands)
    n_in, n_sin, n_sout = len(operands), len(side.ins), len(side.out_shape)

    def body(*refs):
        ins, sins = refs[:n_in], refs[n_in:n_in + n_sin]
        out, souts = refs[n_in + n_sin], refs[n_in + n_sin + 1:n_in + n_sin + 1 + n_sout]
        scr, (send_sems, recv_sems) = refs[n_in + n_sin + 1 + n_sout:-2], refs[-2:]
        ids = [pl.program_id(d) for d in range(len(grid))]
        first = functools.reduce(jnp.logical_and, [i == 0 for i in ids])
        last = functools.reduce(jnp.logical_and, [i == g - 1 for i, g in zip(ids, grid)])

        @pl.when(first)
        def _():
            for cp in side.copies(sins, souts, send_sems, recv_sems):
                cp.start()

        core(*ins, out, *scr)

        @pl.when(last)
        def _():
            cps = side.copies(sins, souts, send_sems, recv_sems)
            for cp in cps:
                cp.wait_recv()
            for cp in cps:
                cp.wait_send()

    return pl.pallas_call(
        body, name=name, grid=grid, in_specs=list(in_specs) + [ANY] * n_sin, out_specs=[out_spec] + [ANY] * n_sout,
        out_shape=[out_shape] + side.out_shape,
        scratch_shapes=list(scratch) + [pltpu.SemaphoreType.DMA((side.n,)), pltpu.SemaphoreType.DMA((side.n,))],
        input_output_aliases={n_in + i: 1 + o for i, o in side.aliases.items()},
        compiler_params=_params("arbitrary", "arbitrary", "arbitrary"))(*operands, *side.ins)


def _side_call(side, name):
    n_sin, n_sout = len(side.ins), len(side.out_shape)

    def body(*refs):
        cps = side.copies(refs[:n_sin], refs[n_sin:n_sin + n_sout], refs[-2], refs[-1])
        for cp in cps:
            cp.start()
        for cp in cps:
            cp.wait_recv()
        for cp in cps:
            cp.wait_send()

    return pl.pallas_call(
        body, name=name, in_specs=[ANY] * n_sin, out_specs=[ANY] * n_sout, out_shape=side.out_shape,
        scratch_shapes=[pltpu.SemaphoreType.DMA((side.n,)), pltpu.SemaphoreType.DMA((side.n,))],
        input_output_aliases=dict(side.aliases))(*side.ins)


def _reduce_core(dot, steps):
    if steps == 1:
        def core(a_ref, b_ref, o_ref):
            o_ref[...] = dot(a_ref[...], b_ref[...]).astype(o_ref.dtype)
        return core

    def core(a_ref, b_ref, o_ref, acc):
        r = pl.program_id(2)

        @pl.when(r == 0)
        def _():
            acc[...] = jnp.zeros_like(acc)

        acc[...] += dot(a_ref[...], b_ref[...])

        @pl.when(r == steps - 1)
        def _():
            o_ref[...] = acc[...].astype(o_ref.dtype)
    return core


def _acc(steps, shape):
    return [] if steps == 1 else [pltpu.VMEM(shape, F32)]


def _mm_nn(a, w, out_dtype, name, side=None):
    m, k = a.shape
    s, _, n = w.shape
    tm, tn, tk = _tile(m, 1024, 16), _tile(n, 1536, LANES), _tile(k, 2816, LANES)
    nb, nk = n // tn, k // tk
    return _grid_call(
        _reduce_core(_dot_nn, nk), name, (m // tm, s * nb, nk),
        [pl.BlockSpec((tm, tk), lambda i, j, kk: (i, kk)),
         pl.BlockSpec((None, tk, tn), lambda i, j, kk: (j // nb, kk, j % nb))],
        pl.BlockSpec((tm, tn), lambda i, j, kk: (i, j)), jax.ShapeDtypeStruct((m, s * n), out_dtype),
        _acc(nk, (tm, tn)), (a, w), side)


def _mm_nt(dy, w, out_dtype, name, side=None):
    m = dy.shape[0]
    s, k, n = w.shape
    tm, tn, tko = _tile(m, 1024, 16), _tile(n, 2816, LANES), _tile(k, 1536, LANES)
    nb = n // tn
    nr = s * nb
    return _grid_call(
        _reduce_core(_dot_nt, nr), name, (m // tm, k // tko, nr),
        [pl.BlockSpec((tm, tn), lambda i, j, r: (i, r)),
         pl.BlockSpec((None, tko, tn), lambda i, j, r: (r // nb, j, r % nb))],
        pl.BlockSpec((tm, tko), lambda i, j, r: (i, j)), jax.ShapeDtypeStruct((m, k), out_dtype),
        _acc(nr, (tm, tko)), (dy, w), side)


def _mm_tn(x, dy, s, name, side=None):
    t, k = x.shape
    n = dy.shape[1] // s
    tk, tn, tt = _tile(k, 1024, LANES), _tile(n, 1536, LANES), _tile(t, 2048, 16)
    nb, nt = n // tn, t // tt
    return _grid_call(
        _reduce_core(_dot_tn, nt), name, (k // tk, s * nb, nt),
        [pl.BlockSpec((tt, tk), lambda i, j, q: (q, i)),
         pl.BlockSpec((tt, tn), lambda i, j, q: (q, j))],
        pl.BlockSpec((None, tk, tn), lambda i, j, q: (j // nb, i, j % nb)), jax.ShapeDtypeStruct((s, k, n), BF16),
        _acc(nt, (tk, tn)), (x, dy), side)


def _rms(x, g):
    return x * lax.rsqrt(jnp.mean(x * x, axis=-1, keepdims=True) + EPS) * g


def _rms_bwd(x, g, dy):
    r = lax.rsqrt(jnp.mean(x * x, axis=-1, keepdims=True) + EPS)
    gy = dy * g
    dx = r * gy - x * (r * r * r * jnp.mean(x * gy, axis=-1, keepdims=True))
    return dx, dy * (x * r)


def _row_spec(tt, d):
    return pl.BlockSpec((tt, d), lambda i: (i, 0))


def _vec_spec(d):
    return pl.BlockSpec((1, d), lambda i: (0, 0))


def _rms_fwd(x, g, name):
    t, d = x.shape
    tt = _tile(t, 512, 16)

    def body(x_ref, g_ref, o_ref):
        o_ref[...] = _rms(x_ref[...], g_ref[...]).astype(o_ref.dtype)

    return pl.pallas_call(
        body, name=name, grid=(t // tt,),
        in_specs=[_row_spec(tt, d), _vec_spec(d)], out_specs=_row_spec(tt, d),
        out_shape=jax.ShapeDtypeStruct((t, d), BF16), compiler_params=_params("parallel"),
    )(x, g)


def _norm_step(h, m, g_post, g_next, name):
    t, d = h.shape
    tt = _tile(t, 512, 16)

    def body(h_ref, m_ref, gp_ref, gn_ref, ho_ref, xn_ref):
        hn = h_ref[...] + _rms(m_ref[...], gp_ref[...])
        ho_ref[...] = hn
        xn_ref[...] = _rms(hn, gn_ref[...]).astype(xn_ref.dtype)

    return pl.pallas_call(
        body, name=name, grid=(t // tt,),
        in_specs=[_row_spec(tt, d), _row_spec(tt, d), _vec_spec(d), _vec_spec(d)],
        out_specs=[_row_spec(tt, d), _row_spec(tt, d)],
        out_shape=[jax.ShapeDtypeStruct((t, d), F32), jax.ShapeDtypeStruct((t, d), BF16)],
        compiler_params=_params("parallel"),
    )(h, m, g_post, g_next)


def _final_loss(h, f, g_post, target, name):
    t, d = h.shape
    tt = _tile(t, 256, 16)

    def body(h_ref, f_ref, g_ref, tg_ref, loss_ref, dh_ref, df_ref, dg_ref):
        @pl.when(pl.program_id(0) == 0)
        def _():
            loss_ref[...] = jnp.zeros_like(loss_ref)
            dg_ref[...] = jnp.zeros_like(dg_ref)

        fv, g = f_ref[...], g_ref[...]
        err = h_ref[...] + _rms(fv, g) - tg_ref[...]
        per_row = jnp.mean(err * err, axis=-1, keepdims=True)
        loss_ref[...] += 0.5 * jnp.sum(per_row, axis=0, keepdims=True)
        dh = err * (1.0 / d)
        dh_ref[...] = dh
        df, dg = _rms_bwd(fv, g, dh)
        df_ref[...] = df.astype(df_ref.dtype)
        dg_ref[...] += jnp.sum(dg, axis=0, keepdims=True)

    return pl.pallas_call(
        body, name=name, grid=(t // tt,),
        in_specs=[_row_spec(tt, d), _row_spec(tt, d), _vec_spec(d), _row_spec(tt, d)],
        out_specs=[pl.BlockSpec((1, 1), lambda i: (0, 0)), _row_spec(tt, d), _row_spec(tt, d), _vec_spec(d)],
        out_shape=[jax.ShapeDtypeStruct((1, 1), F32), jax.ShapeDtypeStruct((t, d), F32),
                   jax.ShapeDtypeStruct((t, d), BF16), jax.ShapeDtypeStruct((1, d), F32)],
        compiler_params=_params("arbitrary"),
    )(h, f, g_post, target)


def _norm_bwd_step(dh_out, dxn, x, g_pre, prev, name):
    t, d = x.shape
    tt = _tile(t, 256, 16)
    has_prev = prev is not None

    def body(*refs):
        if has_prev:
            dho_ref, dxn_ref, x_ref, g_ref, xa_ref, ga_ref, dh_ref, dg_ref, da_ref, dga_ref = refs
        else:
            dho_ref, dxn_ref, x_ref, g_ref, dh_ref, dg_ref = refs

        @pl.when(pl.program_id(0) == 0)
        def _():
            dg_ref[...] = jnp.zeros_like(dg_ref)
            if has_prev:
                dga_ref[...] = jnp.zeros_like(dga_ref)

        dx, dg = _rms_bwd(x_ref[...], g_ref[...], dxn_ref[...])
        dh = dho_ref[...] + dx
        dh_ref[...] = dh
        dg_ref[...] += jnp.sum(dg, axis=0, keepdims=True)
        if has_prev:
            da, dga = _rms_bwd(xa_ref[...], ga_ref[...], dh)
            da_ref[...] = da.astype(da_ref.dtype)
            dga_ref[...] += jnp.sum(dga, axis=0, keepdims=True)

    ins = [dh_out, dxn, x, g_pre] + (list(prev) if has_prev else [])
    in_specs = [_row_spec(tt, d)] * 3 + [_vec_spec(d)] + ([_row_spec(tt, d), _vec_spec(d)] if has_prev else [])
    out_specs = [_row_spec(tt, d), _vec_spec(d)] + ([_row_spec(tt, d), _vec_spec(d)] if has_prev else [])
    out_shape = [jax.ShapeDtypeStruct((t, d), F32), jax.ShapeDtypeStruct((1, d), F32)]
    if has_prev:
        out_shape += [jax.ShapeDtypeStruct((t, d), BF16), jax.ShapeDtypeStruct((1, d), F32)]
    return pl.pallas_call(
        body, name=name, grid=(t // tt,), in_specs=in_specs, out_specs=out_specs, out_shape=out_shape,
        compiler_params=_params("arbitrary"),
    )(*ins)


def _shift_down(x, k, halo):
    r = pltpu.roll(x, k, 0)
    rh = pltpu.roll(halo, k, 0)
    row = lax.broadcasted_iota(jnp.int32, rh.shape, 0)
    return jnp.concatenate([jnp.where(row < k, rh, r[:HALO]), r[HALO:]], axis=0)


def _shift_up(x_ext, k, n):
    return pltpu.roll(x_ext, x_ext.shape[0] - k, 0)[:n]


def _halo_specs(tt, width, nblk):
    per = tt // HALO
    prev = pl.BlockSpec((HALO, width), lambda i: (jnp.maximum(i * per - 1, 0), 0))
    nxt = pl.BlockSpec((HALO, width), lambda i: (jnp.minimum((i + 1) * per, nblk * per - 1), 0))
    return prev, nxt


def _conv3(x, halo, w0, w1, w2):
    return w0 * _shift_down(x, 2, halo) + w1 * _shift_down(x, 1, halo) + w2 * x


def _sc_fwd(p, wc, name):
    t, c3 = p.shape
    c = c3 // 3
    tt, cc = _tile(t, 256, 16), _tile(c, 512, LANES)
    nblk = t // tt
    prev_spec, _ = _halo_specs(tt, c3, nblk)

    def body(p_ref, pp_ref, w_ref, y_ref):
        first = pl.program_id(0) == 0
        for j in range(c // cc):
            cols = slice(j * cc, (j + 1) * cc)
            gc, hv = p_ref[:, c + j * cc:c + (j + 1) * cc], p_ref[:, 2 * c + j * cc:2 * c + (j + 1) * cc]
            uh = jnp.where(first, 0.0, pp_ref[:, c + j * cc:c + (j + 1) * cc] * pp_ref[:, 2 * c + j * cc:2 * c + (j + 1) * cc])
            cv = _conv3(gc * hv, uh, w_ref[0:1, cols], w_ref[1:2, cols], w_ref[2:3, cols])
            y_ref[:, cols] = (p_ref[:, cols] * cv).astype(y_ref.dtype)

    return pl.pallas_call(
        body, name=name, grid=(nblk,),
        in_specs=[_row_spec(tt, c3), prev_spec, pl.BlockSpec((3, c), lambda i: (0, 0))],
        out_specs=_row_spec(tt, c), out_shape=jax.ShapeDtypeStruct((t, c), BF16),
        compiler_params=_params("parallel"),
    )(p, p, wc)


def _sc_bwd(p, dy, wc, name):
    t, c3 = p.shape
    c = c3 // 3
    tt, cc = _tile(t, 256, 16), _tile(c, 512, LANES)
    nblk = t // tt
    p_prev, p_next = _halo_specs(tt, c3, nblk)
    _, dy_next = _halo_specs(tt, c, nblk)

    def body(p_ref, pp_ref, pn_ref, dy_ref, dyn_ref, w_ref, dp_ref, dw_ref):
        i = pl.program_id(0)
        first, last = i == 0, i == nblk - 1

        @pl.when(first)
        def _():
            dw_ref[...] = jnp.zeros_like(dw_ref)

        for j in range(c // cc):
            a, b, d = slice(j * cc, (j + 1) * cc), slice(c + j * cc, c + (j + 1) * cc), slice(2 * c + j * cc, 2 * c + (j + 1) * cc)
            w0, w1, w2 = w_ref[0:1, a], w_ref[1:2, a], w_ref[2:3, a]
            gb, gc, hv, dyv = p_ref[:, a], p_ref[:, b], p_ref[:, d], dy_ref[:, a]
            u = gc * hv
            uh = jnp.where(first, 0.0, pp_ref[:, b] * pp_ref[:, d])
            u2, u1 = _shift_down(u, 2, uh), _shift_down(u, 1, uh)
            cv = w0 * u2 + w1 * u1 + w2 * u
            dcv = dyv * gb
            dcv_ext = jnp.concatenate([dcv, jnp.where(last, 0.0, dyn_ref[:, a] * pn_ref[:, a])], axis=0)
            du = w2 * dcv + w1 * _shift_up(dcv_ext, 1, tt) + w0 * _shift_up(dcv_ext, 2, tt)
            dp_ref[:, a] = (dyv * cv).astype(dp_ref.dtype)
            dp_ref[:, b] = (du * hv).astype(dp_ref.dtype)
            dp_ref[:, d] = (du * gc).astype(dp_ref.dtype)
            dw_ref[0:1, a] += jnp.sum(dcv * u2, axis=0, keepdims=True)
            dw_ref[1:2, a] += jnp.sum(dcv * u1, axis=0, keepdims=True)
            dw_ref[2:3, a] += jnp.sum(dcv * u, axis=0, keepdims=True)

    return pl.pallas_call(
        body, name=name, grid=(nblk,),
        in_specs=[_row_spec(tt, c3), p_prev, p_next, _row_spec(tt, c), dy_next, pl.BlockSpec((3, c), lambda i: (0, 0))],
        out_specs=[_row_spec(tt, c3), pl.BlockSpec((HALO, c), lambda i: (0, 0))],
        out_shape=[jax.ShapeDtypeStruct((t, c3), BF16), jax.ShapeDtypeStruct((HALO, c), F32)],
        compiler_params=_params("arbitrary"),
    )(p, p, p, dy, dy, wc)


def _ffn_fwd(u, cw, cb, name):
    t, f2 = u.shape
    f = f2 // 2
    tt, cc = _tile(t, 128, 16), _tile(f, 512, LANES)
    nblk = t // tt
    prev_spec, _ = _halo_specs(tt, f2, nblk)

    def body(u_ref, up_ref, w_ref, b_ref, a_ref):
        first = pl.program_id(0) == 0

        def conv(cols):
            halo = jnp.where(first, 0.0, up_ref[:, cols])
            return _conv3(u_ref[:, cols], halo, w_ref[0:1, cols], w_ref[1:2, cols], w_ref[2:3, cols]) + b_ref[:, cols]

        for j in range(f // cc):
            hg, hv = conv(slice(j * cc, (j + 1) * cc)), conv(slice(f + j * cc, f + (j + 1) * cc))
            a_ref[:, j * cc:(j + 1) * cc] = (hg * jax.nn.sigmoid(hg) * hv).astype(a_ref.dtype)

    return pl.pallas_call(
        body, name=name, grid=(nblk,),
        in_specs=[_row_spec(tt, f2), prev_spec, pl.BlockSpec((3, f2), lambda i: (0, 0)), _vec_spec(f2)],
        out_specs=_row_spec(tt, f), out_shape=jax.ShapeDtypeStruct((t, f), BF16),
        compiler_params=_params("parallel"),
    )(u, u, cw, cb)


def _ffn_bwd(u, da, cw, cb, name):
    t, f2 = u.shape
    f = f2 // 2
    tt, cc = _tile(t, 128, 16), _tile(f, 512, LANES)
    nblk = t // tt
    u_prev, u_next = _halo_specs(tt, f2, nblk)
    _, da_next = _halo_specs(tt, f, nblk)

    def body(u_ref, up_ref, un_ref, da_ref, dan_ref, w_ref, b_ref, du_ref, dw_ref):
        i = pl.program_id(0)
        first, last = i == 0, i == nblk - 1

        @pl.when(first)
        def _():
            dw_ref[...] = jnp.zeros_like(dw_ref)

        keep = jnp.where(last, 0.0, 1.0)

        def ext(cols):
            x = jnp.concatenate([u_ref[:, cols], un_ref[:, cols]], axis=0)
            halo = jnp.where(first, 0.0, up_ref[:, cols])
            x2, x1 = _shift_down(x, 2, halo), _shift_down(x, 1, halo)
            h = w_ref[0:1, cols] * x2 + w_ref[1:2, cols] * x1 + w_ref[2:3, cols] * x + b_ref[:, cols]
            return x, x1, x2, h

        def back(cols, dh_ext, x, x1, x2):
            w0, w1, w2 = w_ref[0:1, cols], w_ref[1:2, cols], w_ref[2:3, cols]
            dh = dh_ext[:tt]
            du_ref[:, cols] = (w2 * dh + w1 * _shift_up(dh_ext, 1, tt) + w0 * _shift_up(dh_ext, 2, tt)).astype(du_ref.dtype)
            dw_ref[0:1, cols] += jnp.sum(dh * x2[:tt], axis=0, keepdims=True)
            dw_ref[1:2, cols] += jnp.sum(dh * x1[:tt], axis=0, keepdims=True)
            dw_ref[2:3, cols] += jnp.sum(dh * x[:tt], axis=0, keepdims=True)
            dw_ref[3:4, cols] += jnp.sum(dh, axis=0, keepdims=True)

        row = lax.broadcasted_iota(jnp.int32, (tt + HALO, 1), 0)
        for j in range(f // cc):
            gcols, vcols = slice(j * cc, (j + 1) * cc), slice(f + j * cc, f + (j + 1) * cc)
            xg, xg1, xg2, hg = ext(gcols)
            xv, xv1, xv2, hv = ext(vcols)
            da_ext = jnp.concatenate([da_ref[:, gcols], dan_ref[:, gcols]], axis=0)
            da_ext = jnp.where(row < tt, da_ext, da_ext * keep)
            sg = jax.nn.sigmoid(hg)
            back(gcols, da_ext * hv * (sg * (1.0 + hg * (1.0 - sg))), xg, xg1, xg2)
            back(vcols, da_ext * (hg * sg), xv, xv1, xv2)

    return pl.pallas_call(
        body, name=name, grid=(nblk,),
        in_specs=[_row_spec(tt, f2), u_prev, u_next, _row_spec(tt, f), da_next,
                  pl.BlockSpec((3, f2), lambda i: (0, 0)), _vec_spec(f2)],
        out_specs=[_row_spec(tt, f2), pl.BlockSpec((HALO, f2), lambda i: (0, 0))],
        out_shape=[jax.ShapeDtypeStruct((t, f2), BF16), jax.ShapeDtypeStruct((HALO, f2), F32)],
        compiler_params=_params("arbitrary"),
    )(u, u, u, da, da, cw, cb)


def _gelu(x):
    cdf = 0.5 * (1.0 + lax.erf(x * INV_SQRT2))
    return x * cdf, cdf + x * (jnp.exp(-0.5 * x * x) * INV_SQRT_2PI)


def _sgu_common(p_ref, lg_ref, lb_ref, ws_ref, c, ch, groups):
    u, du_dp = _gelu(p_ref[:, :c])
    v, dv_dp = _gelu(p_ref[:, c:])
    mu = jnp.mean(v, axis=-1, keepdims=True)
    vc = v - mu
    rstd = lax.rsqrt(jnp.mean(vc * vc, axis=-1, keepdims=True) + EPS)
    xhat = vc * rstd
    vn = (xhat * lg_ref[...] + lb_ref[...]).astype(BF16)
    tril = lax.broadcasted_iota(jnp.int32, (ch, ch), 0) >= lax.broadcasted_iota(jnp.int32, (ch, ch), 1)
    wm = [jnp.where(tril, ws_ref[g], 0.0).astype(BF16) for g in range(groups)]
    return u, du_dp, dv_dp, xhat, rstd, vn, wm, tril


def _lane_pick(x, g):
    lane = lax.broadcasted_iota(jnp.int32, x.shape, 1)
    return jnp.sum(jnp.where(lane == g, x, 0.0), axis=1, keepdims=True)


def _sgu_specs(tt, c, ch, groups):
    return [_row_spec(tt, 2 * c), _vec_spec(c), _vec_spec(c),
            pl.BlockSpec((groups, ch, ch), lambda i: (0, 0, 0)), pl.BlockSpec((ch, LANES), lambda i: (0, 0))]


def _sgu_fwd(p, lg, lb, ws, bs_t, name):
    t, c2 = p.shape
    c = c2 // 2
    groups, ch, _ = ws.shape
    gc = c // groups
    tt = _tile(t, 2 * ch, ch)

    def body(p_ref, lg_ref, lb_ref, ws_ref, bs_ref, y_ref):
        u, _, _, _, _, vn, wm, _ = _sgu_common(p_ref, lg_ref, lb_ref, ws_ref, c, ch, groups)
        bs = bs_ref[...]
        for r in range(tt // ch):
            rows = slice(r * ch, (r + 1) * ch)
            for g in range(groups):
                cols = slice(g * gc, (g + 1) * gc)
                mixed = _dot_nn(wm[g], vn[rows, cols]) + _lane_pick(bs, g)
                y_ref[rows, cols] = (u[rows, cols] * mixed).astype(y_ref.dtype)

    return pl.pallas_call(
        body, name=name, grid=(t // tt,), in_specs=_sgu_specs(tt, c, ch, groups),
        out_specs=_row_spec(tt, c), out_shape=jax.ShapeDtypeStruct((t, c), BF16),
        compiler_params=_params("parallel"),
    )(p, lg, lb, ws, bs_t)


def _sgu_bwd(p, dy, lg, lb, ws, bs_t, name):
    t, c2 = p.shape
    c = c2 // 2
    groups, ch, _ = ws.shape
    gc = c // groups
    tt = _tile(t, 2 * ch, ch)

    def body(p_ref, dy_ref, lg_ref, lb_ref, ws_ref, bs_ref, dp_ref, dws_ref, dbs_ref, dlg_ref, dlb_ref, dvn_ref):
        @pl.when(pl.program_id(0) == 0)
        def _():
            dws_ref[...] = jnp.zeros_like(dws_ref)
            dbs_ref[...] = jnp.zeros_like(dbs_ref)
            dlg_ref[...] = jnp.zeros_like(dlg_ref)
            dlb_ref[...] = jnp.zeros_like(dlb_ref)

        u, du_dp, dv_dp, xhat, rstd, vn, wm, tril = _sgu_common(p_ref, lg_ref, lb_ref, ws_ref, c, ch, groups)
        bs = bs_ref[...]
        lane = lax.broadcasted_iota(jnp.int32, (ch, LANES), 1)
        for r in range(tt // ch):
            rows = slice(r * ch, (r + 1) * ch)
            for g in range(groups):
                cols = slice(g * gc, (g + 1) * gc)
                dyv, vng = dy_ref[rows, cols], vn[rows, cols]
                mixed = _dot_nn(wm[g], vng) + _lane_pick(bs, g)
                dp_ref[rows, cols] = (dyv * mixed * du_dp[rows, cols]).astype(dp_ref.dtype)
                dmixed = dyv * u[rows, cols]
                dmb = dmixed.astype(BF16)
                dws_ref[g] += jnp.where(tril, _dot_nt(dmb, vng), 0.0)
                dbs_ref[...] += jnp.where(lane == g, jnp.sum(dmixed, axis=1, keepdims=True), 0.0)
                dvn_ref[rows, cols] = _dot_tn(wm[g], dmb)
        dvn = dvn_ref[...]
        dlg_ref[...] += jnp.sum(dvn * xhat, axis=0, keepdims=True)
        dlb_ref[...] += jnp.sum(dvn, axis=0, keepdims=True)
        dxh = dvn * lg_ref[...]
        dv = rstd * (dxh - jnp.mean(dxh, axis=-1, keepdims=True) - xhat * jnp.mean(dxh * xhat, axis=-1, keepdims=True))
        dp_ref[:, c:] = (dv * dv_dp).astype(dp_ref.dtype)

    specs = _sgu_specs(tt, c, ch, groups)
    return pl.pallas_call(
        body, name=name, grid=(t // tt,),
        in_specs=[specs[0], _row_spec(tt, c)] + specs[1:],
        out_specs=[_row_spec(tt, c2), pl.BlockSpec((groups, ch, ch), lambda i: (0, 0, 0)),
                   pl.BlockSpec((ch, LANES), lambda i: (0, 0)), _vec_spec(c), _vec_spec(c)],
        out_shape=[jax.ShapeDtypeStruct((t, c2), BF16), jax.ShapeDtypeStruct((groups, ch, ch), F32),
                   jax.ShapeDtypeStruct((ch, LANES), F32), jax.ShapeDtypeStruct((1, c), F32), jax.ShapeDtypeStruct((1, c), F32)],
        scratch_shapes=[pltpu.VMEM((tt, c), F32)],
        compiler_params=_params("arbitrary"),
    )(p, dy, lg, lb, ws, bs_t)


def _split(x):
    hi = x.astype(BF16)
    return hi, (x - hi.astype(F32)).astype(BF16)


def _sb_block(q, ks, qpos, kb, tk, scale, r_carry, tri):
    z = _dot_nt(q, ks) * scale
    kpos = kb * tk + lax.broadcasted_iota(jnp.int32, (1, tk), 1)
    mask = kpos < qpos
    e = jnp.exp(-jnp.abs(z))
    lm = jnp.where(mask, -(jnp.maximum(z, 0.0) + jnp.log(1.0 + e)), 0.0)
    hi, lo = _split(lm)
    inc = _dot_nn(hi, tri) + _dot_nn(lo, tri)
    att = jnp.where(mask, jnp.exp(z + lm + (inc - lm + r_carry)), 0.0)
    return z, mask, e, inc, att


def _more_keys(i, carry):
    return jnp.logical_and(carry[0] <= i, carry[-1] > EXP_UNDERFLOW)


def _suffix_ones(tk):
    return (lax.broadcasted_iota(jnp.int32, (tk, tk), 0) >= lax.broadcasted_iota(jnp.int32, (tk, tk), 1)).astype(BF16)


def _attn_fwd(qkv, name):
    t, d3 = qkv.shape
    d = d3 // 3
    heads = d // HEAD_DIM
    tq = _tile(t, 256, LANES)
    nq = t // tq
    scale = HEAD_DIM ** -0.5

    def body(q_ref, k_ref, v_ref, of_ref, ob_ref):
        i = pl.program_id(1)
        q = q_ref[...]
        qpos = i * tq + lax.broadcasted_iota(jnp.int32, (tq, 1), 0)
        tri = _suffix_ones(tq)

        def step(carry):
            n, o, r_carry, _ = carry
            kb = i - n
            rows = pl.ds(pl.multiple_of(kb * tq, tq), tq)
            _, _, _, inc, att = _sb_block(q, k_ref[rows, :], qpos, kb, tq, scale, r_carry, tri)
            ahi, alo = _split(att)
            vs = v_ref[rows, :]
            r_new = r_carry + inc[:, 0:1]
            return n + 1, o + _dot_nn(ahi, vs) + _dot_nn(alo, vs), r_new, jnp.max(r_new)

        init = (jnp.int32(0), jnp.zeros((tq, HEAD_DIM), F32), jnp.zeros((tq, 1), F32), jnp.float32(0.0))
        _, o, _, _ = lax.while_loop(functools.partial(_more_keys, i), step, init)
        of_ref[...] = o
        ob_ref[...] = o.astype(ob_ref.dtype)

    return pl.pallas_call(
        body, name=name, grid=(heads, nq),
        in_specs=[pl.BlockSpec((tq, HEAD_DIM), lambda h, i: (i, h)),
                  pl.BlockSpec((t, HEAD_DIM), lambda h, i: (0, heads + h)),
                  pl.BlockSpec((t, HEAD_DIM), lambda h, i: (0, 2 * heads + h))],
        out_specs=[pl.BlockSpec((tq, HEAD_DIM), lambda h, i: (i, h))] * 2,
        out_shape=[jax.ShapeDtypeStruct((t, d), F32), jax.ShapeDtypeStruct((t, d), BF16)],
        compiler_params=_params("parallel", "parallel"),
    )(qkv, qkv, qkv)


def _attn_bwd(qkv, do, of, name):
    t, d3 = qkv.shape
    d = d3 // 3
    heads = d // HEAD_DIM
    tq = _tile(t, 256, LANES)
    nq = t // tq
    scale = HEAD_DIM ** -0.5

    def body(q_ref, k_ref, v_ref, do_ref, of_ref, dq_ref, dk_ref, dv_ref, dk_acc, dv_acc):
        i = pl.program_id(1)

        @pl.when(i == 0)
        def _():
            dk_acc[...] = jnp.zeros_like(dk_acc)
            dv_acc[...] = jnp.zeros_like(dv_acc)

        q, dov = q_ref[...], do_ref[...]
        delta = jnp.sum(dov.astype(F32) * of_ref[...], axis=-1, keepdims=True)
        qpos = i * tq + lax.broadcasted_iota(jnp.int32, (tq, 1), 0)
        tri = _suffix_ones(tq)

        def step(carry):
            n, dq, r_carry, g_carry, _ = carry
            kb = i - n
            rows = pl.ds(pl.multiple_of(kb * tq, tq), tq)
            ks, vs = k_ref[rows, :], v_ref[rows, :]
            z, mask, e, inc, att = _sb_block(q, ks, qpos, kb, tq, scale, r_carry, tri)
            g = _dot_nt(dov, vs) * att
            ghi, glo = _split(g)
            ginc = _dot_nn(ghi, tri) + _dot_nn(glo, tri)
            beta = jnp.where(z >= 0.0, 1.0, e) / (1.0 + e)
            dz = jnp.where(mask, g * (1.0 - beta) - (delta - g_carry - ginc) * beta, 0.0) * scale
            dzb = dz.astype(BF16)
            dk_acc[rows, :] += _dot_tn(dzb, q)
            dv_acc[rows, :] += _dot_tn(att.astype(BF16), dov)
            r_new = r_carry + inc[:, 0:1]
            return n + 1, dq + _dot_nn(dzb, ks), r_new, g_carry + ginc[:, 0:1], jnp.max(r_new)

        zero = jnp.zeros((tq, 1), F32)
        init = (jnp.int32(0), jnp.zeros((tq, HEAD_DIM), F32), zero, zero, jnp.float32(0.0))
        _, dq, _, _, _ = lax.while_loop(functools.partial(_more_keys, i), step, init)
        dq_ref[...] = dq.astype(dq_ref.dtype)

        @pl.when(i == nq - 1)
        def _():
            dk_ref[...] = dk_acc[...].astype(dk_ref.dtype)
            dv_ref[...] = dv_acc[...].astype(dv_ref.dtype)

    blk = pl.BlockSpec((tq, HEAD_DIM), lambda h, i: (i, h))
    full = pl.BlockSpec((t, HEAD_DIM), lambda h, i: (0, h))
    return pl.pallas_call(
        body, name=name, grid=(heads, nq),
        in_specs=[blk, pl.BlockSpec((t, HEAD_DIM), lambda h, i: (0, heads + h)),
                  pl.BlockSpec((t, HEAD_DIM), lambda h, i: (0, 2 * heads + h)), blk, blk],
        out_specs=[blk, full, full],
        out_shape=[jax.ShapeDtypeStruct((t, d), BF16)] * 3,
        scratch_shapes=[pltpu.VMEM((t, HEAD_DIM), F32), pltpu.VMEM((t, HEAD_DIM), F32)],
        compiler_params=_params("parallel", "arbitrary"),
    )(qkv, qkv, qkv, do, of)


def _pair_add(g, pair, name):
    _, _, rh, c = g.shape
    tr = _tile(rh, 256, 16)
    core = lax.axis_index("c").astype(jnp.int32).reshape(1)

    def body(c_ref, g_ref, p_ref, o_ref):
        o_ref[...] = (g_ref[...].astype(F32) + p_ref[...].astype(F32)).astype(o_ref.dtype)

    return pl.pallas_call(
        body, name=name,
        grid_spec=pltpu.PrefetchScalarGridSpec(
            num_scalar_prefetch=1, grid=(4, rh // tr),
            in_specs=[pl.BlockSpec((None, None, tr, c), lambda j, r, cr: (j, cr[0], r, 0)),
                      pl.BlockSpec((None, tr, c), lambda j, r, cr: (j, r, 0))],
            out_specs=pl.BlockSpec((None, tr, c), lambda j, r, cr: (j, r, 0))),
        out_shape=jax.ShapeDtypeStruct((4, rh, c), BF16),
        compiler_params=_params("parallel", "parallel"),
    )(core, g, pair)


def _sum_slots(x, name):
    n, r, c = x.shape
    tr = _tile(r, 256, 16)

    def body(x_ref, o_ref):
        acc = x_ref[0].astype(F32)
        for k in range(1, n):
            acc = acc + x_ref[k].astype(F32)
        o_ref[...] = acc

    return pl.pallas_call(
        body, name=name, grid=(r // tr,),
        in_specs=[pl.BlockSpec((n, tr, c), lambda i: (0, i, 0))],
        out_specs=pl.BlockSpec((tr, c), lambda i: (i, 0)),
        out_shape=jax.ShapeDtypeStruct((r, c), F32), compiler_params=_params("parallel"),
    )(x)


def _sum_own_slots(sums, slots, name):
    _, rh, c = sums.shape
    tr = _tile(rh, 256, 16)
    chip = (2 * lax.axis_index("x") + lax.axis_index("y")).astype(jnp.int32).reshape(1)

    def body(chip_ref, own_ref, slot_ref, o_ref):
        acc = own_ref[...].astype(F32)
        for k in range(3):
            acc = acc + slot_ref[k].astype(F32)
        o_ref[...] = acc

    return pl.pallas_call(
        body, name=name,
        grid_spec=pltpu.PrefetchScalarGridSpec(
            num_scalar_prefetch=1, grid=(rh // tr,),
            in_specs=[pl.BlockSpec((None, tr, c), lambda r, me: (me[0], r, 0)),
                      pl.BlockSpec((3, tr, c), lambda r, me: (0, r, 0))],
            out_specs=pl.BlockSpec((tr, c), lambda r, me: (r, 0))),
        out_shape=jax.ShapeDtypeStruct((rh, c), F32), compiler_params=_params("parallel"),
    )(chip, sums, slots)


def _adam_update(w, g, m, v):
    mn = ADAM_B1 * m + (1.0 - ADAM_B1) * g
    vn = ADAM_B2 * v + (1.0 - ADAM_B2) * (g * g)
    c1, c2 = 1.0 - ADAM_B1 ** ADAM_STEP, 1.0 - ADAM_B2 ** ADAM_STEP
    return -ADAM_LR * ((mn / c1) / (jnp.sqrt(vn / c2) + ADAM_EPS) + ADAM_WD * w), mn, vn


def _adamw_halves(w, own, recv, m, v, name):
    layers, _, rh, c = w.shape
    tr = _tile(rh, max(8, (1 << 18) // c), 8)
    core = lax.axis_index("c").astype(jnp.int32).reshape(1)

    def body(c_ref, w_ref, own_ref, recv_ref, m_ref, v_ref, g_ref, d_ref, mo_ref, vo_ref):
        g = jnp.where(pl.program_id(1) == c_ref[0], own_ref[...], recv_ref[...])
        g_ref[...] = g
        d_ref[...], mo_ref[...], vo_ref[...] = _adam_update(w_ref[...], g, m_ref[...], v_ref[...])

    full = pl.BlockSpec((None, None, tr, c), lambda l, hf, r, cr: (l, hf, r, 0))
    half = pl.BlockSpec((None, tr, c), lambda l, hf, r, cr: (l, r, 0))
    return pl.pallas_call(
        body, name=name,
        grid_spec=pltpu.PrefetchScalarGridSpec(
            num_scalar_prefetch=1, grid=(layers, 2, rh // tr),
            in_specs=[full, half, half, full, full], out_specs=[full] * 4),
        out_shape=[jax.ShapeDtypeStruct(w.shape, F32)] * 4,
        compiler_params=_params("parallel", "parallel", "parallel"),
    )(core, w, own, recv, m, v)


def _adamw(w, g, m, v, name):
    r, c = w.shape
    tr = _tile(r, max(8, (1 << 18) // c), 8)

    def body(w_ref, g_ref, m_ref, v_ref, d_ref, mo_ref, vo_ref):
        d_ref[...], mo_ref[...], vo_ref[...] = _adam_update(w_ref[...], g_ref[...], m_ref[...], v_ref[...])

    spec = pl.BlockSpec((tr, c), lambda i: (i, 0))
    return pl.pallas_call(
        body, name=name, grid=(r // tr,), in_specs=[spec] * 4, out_specs=[spec] * 3,
        out_shape=[jax.ShapeDtypeStruct((r, c), F32)] * 3, compiler_params=_params("parallel"),
    )(w, g, m, v)


def _place():
    x, y, c = lax.axis_index("x"), lax.axis_index("y"), lax.axis_index("c")
    chips = [(1 - x, y), (x, 1 - y), (1 - x, 1 - y)]
    return x, y, c, chips


def _remote(src, dst, send_sems, recv_sems, k, to):
    return pltpu.make_async_remote_copy(src_ref=src, dst_ref=dst, send_sem=send_sems.at[k], recv_sem=recv_sems.at[k],
                                        device_id=to, device_id_type=MESH)


def _comm_call(body, name, ins, out_shape, n_remote, n_local=0):
    sems = [pltpu.SemaphoreType.DMA((n_remote,)), pltpu.SemaphoreType.DMA((n_remote,))]
    if n_local:
        sems.append(pltpu.SemaphoreType.DMA((n_local,)))
    return pl.pallas_call(
        body, name=name, in_specs=[ANY] * len(ins), out_specs=[ANY] * len(out_shape), out_shape=out_shape,
        scratch_shapes=sems,
    )(*ins)


def _all_gather(shards, name):
    n = len(shards)

    def body(*refs):
        ins, outs = refs[:n], refs[n:2 * n]
        send_sems, recv_sems = refs[2 * n:]
        x, y, c, chips = _place()
        me, sib = 2 * x + y, (x, y, 1 - c)
        started = []
        for a in range(n):
            cp = _remote(ins[a], outs[a].at[me], send_sems, recv_sems, 7 * a + 6, sib)
            cp.start()
            started.append(cp)
            for k, (px, py) in enumerate(chips):
                cp = _remote(ins[a].at[c], outs[a].at[me, c], send_sems, recv_sems, 7 * a + k, (px, py, c))
                cp.start()
                started.append(cp)
        for a in range(n):
            for k, (px, py) in enumerate(chips):
                land = outs[a].at[2 * px + py, c]
                _remote(land, land, send_sems, recv_sems, 7 * a + k, sib).wait_recv()
                cp = _remote(land, land, send_sems, recv_sems, 7 * a + 3 + k, sib)
                cp.start()
                started.append(cp)
        for a in range(n):
            for k, (px, py) in enumerate(chips):
                land = outs[a].at[2 * px + py, 1 - c]
                _remote(land, land, send_sems, recv_sems, 7 * a + 3 + k, sib).wait_recv()
            own = outs[a].at[me]
            _remote(own, own, send_sems, recv_sems, 7 * a + 6, sib).wait_recv()
        for cp in started:
            cp.wait_send()

    out_shape = [jax.ShapeDtypeStruct((4,) + s.shape, s.dtype) for s in shards]
    return _comm_call(body, name, shards, out_shape, 7 * n)


def _pair_exchange(grads, name):
    n = len(grads)

    def body(*refs):
        ins, outs = refs[:n], refs[n:2 * n]
        send_sems, recv_sems = refs[2 * n:]
        x, y, c, _ = _place()
        sib = (x, y, 1 - c)
        started = []
        for a in range(n):
            for j in range(4):
                cp = _remote(ins[a].at[j, 1 - c], outs[a].at[j], send_sems, recv_sems, 4 * a + j, sib)
                cp.start()
                started.append(cp)
        for cp in started:
            cp.wait_recv()
        for cp in started:
            cp.wait_send()

    out_shape = [jax.ShapeDtypeStruct((4,) + g.shape[2:], g.dtype) for g in grads]
    return _comm_call(body, name, grads, out_shape, 4 * n)


def _gather_sides(shards):
    n = len(shards)
    full_shape = [jax.ShapeDtypeStruct((4,) + s.shape, s.dtype) for s in shards]

    def ici(ins, outs, send_sems, recv_sems):
        x, y, c, chips = _place()
        return [_remote(ins[a].at[c], outs[a].at[2 * x + y, c], send_sems, recv_sems, 3 * a + k, (px, py, c))
                for a in range(n) for k, (px, py) in enumerate(chips)]

    def d2d(ins, outs, send_sems, recv_sems):
        x, y, c, chips = _place()
        sib, cps = (x, y, 1 - c), []
        for a in range(n):
            cps.append(_remote(ins[a], outs[a].at[2 * x + y], send_sems, recv_sems, 4 * a + 3, sib))
            for k, (px, py) in enumerate(chips):
                land = outs[a].at[2 * px + py, c]
                cps.append(_remote(land, land, send_sems, recv_sems, 4 * a + k, sib))
        return cps

    def second(partial):
        return Side(list(shards) + list(partial), full_shape, 4 * n, d2d, aliases={n + a: a for a in range(n)})

    return Side(shards, full_shape, 3 * n, ici), second


def _chip_exchange_side(sums):
    n = len(sums)

    def copies(ins, outs, send_sems, recv_sems):
        _, _, c, chips = _place()
        return [_remote(ins[a].at[2 * px + py], outs[a].at[k], send_sems, recv_sems, 3 * a + k, (px, py, c))
                for a in range(n) for k, (px, py) in enumerate(chips)]

    return Side(sums, [jax.ShapeDtypeStruct((3,) + s.shape[1:], s.dtype) for s in sums], 3 * n, copies)


def _halves_exchange(own, name):
    n = len(own)

    def body(*refs):
        ins, outs = refs[:n], refs[n:2 * n]
        send_sems, recv_sems = refs[2 * n:]
        x, y, c, _ = _place()
        started = [_remote(ins[a], outs[a], send_sems, recv_sems, a, (x, y, 1 - c)) for a in range(n)]
        for cp in started:
            cp.start()
        for cp in started:
            cp.wait_recv()
        for cp in started:
            cp.wait_send()

    return _comm_call(body, name, own, [jax.ShapeDtypeStruct(t.shape, t.dtype) for t in own], n)


def _broadcast_small(buf, name):
    def body(in_ref, out_ref, send_sems, recv_sems, loc_sems):
        x, y, c, _ = _place()
        me = 4 * x + 2 * y + c
        loc = pltpu.make_async_copy(in_ref, out_ref.at[me], loc_sems.at[0])
        loc.start()
        peers = [(x ^ (k >> 2 & 1), y ^ (k >> 1 & 1), c ^ (k & 1)) for k in range(1, 8)]
        started = []
        for k, to in enumerate(peers):
            cp = _remote(in_ref, out_ref.at[me], send_sems, recv_sems, k, to)
            cp.start()
            started.append(cp)
        for k, (px, py, pc) in enumerate(peers):
            land = out_ref.at[4 * px + 2 * py + pc]
            _remote(land, land, send_sems, recv_sems, k, (px, py, pc)).wait_recv()
        for cp in started:
            cp.wait_send()
        loc.wait()

    return _comm_call(body, name, [buf], [jax.ShapeDtypeStruct((8,) + buf.shape, buf.dtype)], 7, 1)[0]


def _pack(arrays):
    flat = jnp.concatenate([a.reshape(-1).astype(F32) for a in arrays])
    pad = -flat.shape[0] % (256 * LANES)
    return jnp.pad(flat, (0, pad)).reshape(-1, LANES)


def _unpack(buf, shapes):
    flat, out, at = buf.reshape(-1), [], 0
    for s in shapes:
        size = math.prod(s)
        out.append(flat[at:at + size].reshape(s))
        at += size
    return out


def _adamw_any(w, g, m, v, name):
    shape = w.shape
    two = (-1, shape[-1]) if w.ndim >= 2 else (1, -1)
    outs = _adamw(w.reshape(two), g.reshape(two), m.reshape(two), v.reshape(two), name)
    return [o.reshape(shape) for o in outs]


BIG = ("sc_w_in", "sc_w_out", "sg_w_in", "sg_w_out", "sb_w_qkv", "sb_w_out", "ffn_w_up", "ffn_w_down")
ROW_SHARDED = ("sc_w_out", "sg_w_out", "sb_w_out", "ffn_w_down")
MIXER_WEIGHTS = (("sc_w_in", "sc_w_out"), ("sg_w_in", "sg_w_out"), ("sb_w_qkv", "sb_w_out"))
WEIGHTS = ("norm_mix_pre", "norm_mix_post", "norm_ffn_pre", "norm_ffn_post", "sc_w_in", "sc_conv_w", "sc_w_out",
           "sg_w_in", "sg_ln_g", "sg_ln_b", "sg_w_s", "sg_b_s", "sg_w_out", "sb_w_qkv", "sb_w_out",
           "ffn_w_up", "ffn_conv_w", "ffn_conv_b", "ffn_w_down")


def _step(p):
    t, d = p["x"].shape[1:]
    x, target = p["x"].reshape(t, d), p["loss_target"].reshape(t, d)
    depth = p["norm_mix_pre"].shape[0]
    chip = 2 * lax.axis_index("x") + lax.axis_index("y")

    def layer_entities(i):
        name_in, name_out = MIXER_WEIGHTS[i % N_MIXERS]
        return [(name_in, i // N_MIXERS), (name_out, i // N_MIXERS), ("ffn_w_up", i), ("ffn_w_down", i)]

    def shard(e):
        s = p[e[0]][e[1]].astype(BF16)
        return s.reshape(2, s.shape[0] // 2, s.shape[1])

    def as_weight(e, g):
        rows, cols = p[e[0]].shape[1:]
        return g.reshape(1, 4 * rows, cols) if e[0] in ROW_SHARDED else g.reshape(4, rows, cols)

    ents = [e for i in range(depth) for e in layer_entities(i)]
    full = {e: as_weight(e, g) for e, g in zip(layer_entities(0), _all_gather([shard(e) for e in layer_entities(0)],
                                                                              "gather_weights_0"))}

    conv_shapes = [p["sc_conv_w"].shape, p["ffn_conv_w"].shape]
    conv_all = _broadcast_small(_pack([p["sc_conv_w"], p["ffn_conv_w"]]), "gather_conv_taps")[0::2]
    sc_cw, ffn_cw = [jnp.moveaxis(jnp.stack([_unpack(conv_all[j], conv_shapes)[i] for j in range(4)]), 0, 2)
                     .reshape(s[0], s[1], 4 * s[2]) for i, s in enumerate(conv_shapes)]

    ws = p["sg_w_s"]
    groups, ch = ws.shape[1], ws.shape[2]
    bs_t = [jnp.pad(p["sg_b_s"][j].T, ((0, 0), (0, LANES - groups))) for j in range(ws.shape[0])]
    g1, g2, g3, g4 = [[p[k][i:i + 1] for i in range(depth)] for k in WEIGHTS[:4]]

    saved = []
    h, hn = x, _rms_fwd(x, g1[0], "rms_in")
    for i in range(depth):
        kind, j = i % N_MIXERS, i // N_MIXERS
        if kind == 0:
            pre = _mm_nn(hn, full["sc_w_in", j], F32, f"sc_in_{i}")
            y = _sc_fwd(pre, sc_cw[j], f"sc_mix_{i}")
            w_in, w_out, extra = full["sc_w_in", j], full["sc_w_out", j], None
        elif kind == 1:
            pre = _mm_nn(hn, full["sg_w_in", j], F32, f"sg_in_{i}")
            y = _sgu_fwd(pre, p["sg_ln_g"][j:j + 1], p["sg_ln_b"][j:j + 1], ws[j], bs_t[j], f"sg_mix_{i}")
            w_in, w_out, extra = full["sg_w_in", j], full["sg_w_out", j], None
        else:
            pre = _mm_nn(hn, full["sb_w_qkv", j], BF16, f"sb_in_{i}")
            extra, y = _attn_fwd(pre, f"sb_mix_{i}")
            w_in, w_out = full["sb_w_qkv", j], full["sb_w_out", j]
        m = _mm_nn(y, w_out, F32, f"mix_out_{i}")
        h1, fn = _norm_step(h, m, g2[i], g3[i], f"norm_mid_{i}")
        if i < depth - 1:
            nxt = layer_entities(i + 1)
            first, second = _gather_sides([shard(e) for e in nxt])
            up, *partial = _mm_nn(fn, full["ffn_w_up", i], F32, f"ffn_up_{i}", first)
            act = _ffn_fwd(up, ffn_cw[i], p["ffn_conv_b"][i:i + 1], f"ffn_act_{i}")
            f, *gathered = _mm_nn(act, full["ffn_w_down", i], F32, f"ffn_down_{i}", second(partial))
            full.update({e: as_weight(e, g) for e, g in zip(nxt, gathered)})
        else:
            up = _mm_nn(fn, full["ffn_w_up", i], F32, f"ffn_up_{i}")
            act = _ffn_fwd(up, ffn_cw[i], p["ffn_conv_b"][i:i + 1], f"ffn_act_{i}")
            f = _mm_nn(act, full["ffn_w_down", i], F32, f"ffn_down_{i}")
        saved.append(dict(h=h, hn=hn, pre=pre, y=y, extra=extra, m=m, h1=h1, fn=fn, up=up, act=act, f=f,
                          w_in=w_in, w_out=w_out))
        if i < depth - 1:
            h, hn = _norm_step(h1, f, g4[i], g1[i + 1], f"norm_end_{i}")
    loss, dh, df, dg4 = _final_loss(h1, f, g4[depth - 1], target, "loss_head")

    gbig, gsm = {}, {k: [None] * depth for k in ("g1", "g2", "g3", "g4", "ffn_cw", "ffn_cb")}
    gsm["g4"][depth - 1] = dg4
    gsm["sc_cw"] = [None] * p["sc_conv_w"].shape[0]
    for k in ("sg_lg", "sg_lb", "sg_ws", "sg_bs"):
        gsm[k] = [None] * ws.shape[0]
    pending, sums, slots = None, {}, {}
    for i in reversed(range(depth)):
        s = saved[i]
        kind, j = i % N_MIXERS, i // N_MIXERS
        dact = _mm_nt(df, full["ffn_w_down", i], F32, f"d_ffn_act_{i}")
        gbig["ffn_w_down", i] = _mm_tn(s["act"], df, 1, f"g_ffn_down_{i}")
        dup, dcw = _ffn_bwd(s["up"], dact, ffn_cw[i], p["ffn_conv_b"][i:i + 1], f"d_ffn_up_{i}")
        gsm["ffn_cw"][i], gsm["ffn_cb"][i] = dcw[0:3], dcw[3:4]
        dfn = _mm_nt(dup, full["ffn_w_up", i], F32, f"d_ffn_in_{i}")
        if pending:
            gbig["ffn_w_up", i], *landed = _mm_tn(s["fn"], dup, 4, f"g_ffn_up_{i}", _chip_exchange_side([sums[e] for e in pending]))
            slots.update(zip(pending, landed))
        else:
            gbig["ffn_w_up", i] = _mm_tn(s["fn"], dup, 4, f"g_ffn_up_{i}")
        dh1, gsm["g3"][i], dm, gsm["g2"][i] = _norm_bwd_step(dh, dfn, s["h1"], g3[i], (s["m"], g2[i]), f"d_norm_mid_{i}")
        name_in, name_out = MIXER_WEIGHTS[kind]
        gbig[name_out, j] = _mm_tn(s["y"], dm, 1, f"g_mix_out_{i}")
        if kind == 0:
            dy = _mm_nt(dm, s["w_out"], F32, f"d_mix_y_{i}")
            dpre, dwc = _sc_bwd(s["pre"], dy, sc_cw[j], f"d_sc_mix_{i}")
            gsm["sc_cw"][j] = dwc[0:3]
        elif kind == 1:
            dy = _mm_nt(dm, s["w_out"], F32, f"d_mix_y_{i}")
            dpre, dws, dbs, dlg, dlb = _sgu_bwd(s["pre"], dy, p["sg_ln_g"][j:j + 1], p["sg_ln_b"][j:j + 1], ws[j], bs_t[j],
                                                f"d_sg_mix_{i}")
            gsm["sg_lg"][j], gsm["sg_lb"][j], gsm["sg_ws"][j], gsm["sg_bs"][j] = dlg, dlb, dws, dbs[:, :groups].T
        else:
            do = _mm_nt(dm, s["w_out"], BF16, f"d_mix_y_{i}")
            dpre = jnp.concatenate(_attn_bwd(s["pre"], do, s["extra"], f"d_sb_mix_{i}"), axis=1)
        dhn = _mm_nt(dpre, s["w_in"], F32, f"d_mix_in_{i}")
        gbig[name_in, j] = _mm_tn(s["hn"], dpre, 4, f"g_mix_in_{i}")
        if i > 0:
            dh, gsm["g1"][i], df, gsm["g4"][i - 1] = _norm_bwd_step(dh1, dhn, s["h"], g1[i],
                                                                   (saved[i - 1]["f"], g4[i - 1]), f"d_norm_in_{i}")
        else:
            dx, gsm["g1"][0] = _norm_bwd_step(dh1, dhn, s["h"], g1[0], None, "d_norm_in_0")
        pending = layer_entities(i)
        parts = [gbig[e].reshape(4, 2, -1, gbig[e].shape[-1]) for e in pending]
        pair = _pair_exchange(parts, f"grad_pair_exchange_{i}")
        sums.update({e: _pair_add(g, q, f"grad_pair_add_{i}_{a}") for a, (e, g, q) in enumerate(zip(pending, parts, pair))})
    slots.update(zip(pending, _side_call(_chip_exchange_side([sums[e] for e in pending]), "grad_chip_exchange_0")))

    halves = {e: _sum_own_slots(sums[e], slots[e], f"grad_chip_sum_{a}") for a, e in enumerate(ents)}
    own = [jnp.stack([halves[name, l] for l in range(p[name].shape[0])]) for name in BIG]
    recv = _halves_exchange(own, "grad_halves_exchange")
    grads, delta, new_m, new_v = {}, {}, {}, {}
    for name, mine, theirs in zip(BIG, own, recv):
        shape = p[name].shape
        view = (shape[0], 2, shape[1] // 2, shape[2])
        outs = _adamw_halves(p[name].reshape(view), mine, theirs, p["m_" + name].reshape(view), p["v_" + name].reshape(view),
                             f"adamw_{name}")
        grads[name], delta[name], new_m[name], new_v[name] = [o.reshape(shape) for o in outs]

    small = [jnp.concatenate(gsm[k]) for k in ("g1", "g2", "g3", "g4", "sg_lg", "sg_lb")] + [
        jnp.stack(gsm["sg_ws"]), jnp.stack(gsm["sg_bs"]),
        jnp.concatenate(gsm["ffn_cb"]), jnp.stack(gsm["sc_cw"]), jnp.stack(gsm["ffn_cw"])]
    small_shapes = [a.shape for a in small]
    total = _sum_slots(_broadcast_small(_pack(small), "small_grad_exchange"), "small_grad_sum")
    sm = _unpack(total, small_shapes)
    for k, name in enumerate(("norm_mix_pre", "norm_mix_post", "norm_ffn_pre", "norm_ffn_post", "sg_ln_g", "sg_ln_b")):
        grads[name] = sm[k].reshape(p[name].shape)
    grads["sg_w_s"], grads["sg_b_s"] = sm[6].reshape(ws.shape), sm[7].reshape(p["sg_b_s"].shape)
    grads["ffn_conv_b"] = sm[8].reshape(p["ffn_conv_b"].shape)
    for name, full_g in (("sc_conv_w", sm[9]), ("ffn_conv_w", sm[10])):
        n = p[name].shape[-1]
        grads[name] = lax.dynamic_slice_in_dim(full_g, chip * n, n, axis=2)

    for name in WEIGHTS:
        if name in BIG:
            continue
        delta[name], new_m[name], new_v[name] = _adamw_any(p[name], grads[name], p["m_" + name], p["v_" + name], f"adamw_{name}")

    loss = lax.psum(loss[0, 0], ("x", "y", "c"))
    return (loss, dx.reshape(p["x"].shape), *[grads[n] for n in WEIGHTS], *[delta[n] for n in WEIGHTS],
            *[new_m[n] for n in WEIGHTS], *[new_v[n] for n in WEIGHTS])


def kernel(x, norm_mix_pre, norm_mix_post, norm_ffn_pre, norm_ffn_post, sc_w_in, sc_conv_w, sc_w_out, sg_w_in, sg_ln_g, sg_ln_b, sg_w_s, sg_b_s, sg_w_out, sb_w_qkv, sb_w_out, ffn_w_up, ffn_conv_w, ffn_conv_b, ffn_w_down, loss_target, m_norm_mix_pre, m_norm_mix_post, m_norm_ffn_pre, m_norm_ffn_post, m_sc_w_in, m_sc_conv_w, m_sc_w_out, m_sg_w_in, m_sg_ln_g, m_sg_ln_b, m_sg_w_s, m_sg_b_s, m_sg_w_out, m_sb_w_qkv, m_sb_w_out, m_ffn_w_up, m_ffn_conv_w, m_ffn_conv_b, m_ffn_w_down, v_norm_mix_pre, v_norm_mix_post, v_norm_ffn_pre, v_norm_ffn_post, v_sc_w_in, v_sc_conv_w, v_sc_w_out, v_sg_w_in, v_sg_ln_g, v_sg_ln_b, v_sg_w_s, v_sg_b_s, v_sg_w_out, v_sb_w_qkv, v_sb_w_out, v_ffn_w_up, v_ffn_conv_w, v_ffn_conv_b, v_ffn_w_down):
    return _step(dict(locals()))
```

```python
import functools
import math

import jax
import jax.numpy as jnp
from jax import lax
from jax.experimental import pallas as pl
from jax.experimental.pallas import tpu as pltpu

F32 = jnp.float32
BF16 = jnp.bfloat16
MESH = pl.DeviceIdType.MESH
ANY = pl.BlockSpec(memory_space=pl.ANY)

EPS = 1e-6
HEAD_DIM = 128
N_MIXERS = 3
ADAM_LR, ADAM_B1, ADAM_B2, ADAM_EPS, ADAM_WD, ADAM_STEP = 0.001, 0.9, 0.999, 1e-08, 0.01, 10
V7X_VMEM_LIMIT = 56 * 1024 * 1024
HALO = 8
LANES = 128
EXP_UNDERFLOW = -104.0
INV_SQRT2 = 1.0 / math.sqrt(2.0)
INV_SQRT_2PI = 1.0 / math.sqrt(2.0 * math.pi)


def _tile(n, target, mult):
    t = min(n, target) // mult * mult
    while t >= mult:
        if n % t == 0:
            return t
        t -= mult
    return n


def _params(*sem):
    return pltpu.CompilerParams(dimension_semantics=sem, vmem_limit_bytes=V7X_VMEM_LIMIT)


def _dot(a, b, dims):
    return lax.dot_general(a, b, (dims, ((), ())), preferred_element_type=F32)


def _dot_nn(a, b):
    return _dot(a, b, ((1,), (0,)))


def _dot_nt(a, b):
    return _dot(a, b, ((1,), (1,)))


def _dot_tn(a, b):
    return _dot(a, b, ((0,), (0,)))


class Side:
    def __init__(self, ins, out_shape, n, copies, aliases=None):
        self.ins, self.out_shape, self.n, self.copies, self.aliases = list(ins), list(out_shape), n, copies, aliases or {}


def _grid_call(core, name, grid, in_specs, out_spec, out_shape, scratch, operands, side):
    if side is None:
        return pl.pallas_call(
            core, name=name, grid=grid, in_specs=in_specs, out_specs=out_spec, out_shape=out_shape, scratch_shapes=scratch,
            compiler_params=_params("parallel", "parallel", "arbitrary"))(*operands)
    n_in, n_sin, n_sout = len(operands), len(side.ins), len(side.out_shape)

    def body(*refs):
        ins, sins = refs[:n_in], refs[n_in:n_in + n_sin]
        out, souts = refs[n_in + n_sin], refs[n_in + n_sin + 1:n_in + n_sin + 1 + n_sout]
        scr, (send_sems, recv_sems) = refs[n_in + n_sin + 1 + n_sout:-2], refs[-2:]
        ids = [pl.program_id(d) for d in range(len(grid))]
        first = functools.reduce(jnp.logical_and, [i == 0 for i in ids])
        last = functools.reduce(jnp.logical_and, [i == g - 1 for i, g in zip(ids, grid)])

        @pl.when(first)
        def _():
            for cp in side.copies(sins, souts, send_sems, recv_sems):
                cp.start()

        core(*ins, out, *scr)

        @pl.when(last)
        def _():
            cps = side.copies(sins, souts, send_sems, recv_sems)
            for cp in cps:
                cp.wait_recv()
            for cp in cps:
                cp.wait_send()

    return pl.pallas_call(
        body, name=name, grid=grid, in_specs=list(in_specs) + [ANY] * n_sin, out_specs=[out_spec] + [ANY] * n_sout,
        out_shape=[out_shape] + side.out_shape,
        scratch_shapes=list(scratch) + [pltpu.SemaphoreType.DMA((side.n,)), pltpu.SemaphoreType.DMA((side.n,))],
        input_output_aliases={n_in + i: 1 + o for i, o in side.aliases.items()},
        compiler_params=_params("arbitrary", "arbitrary", "arbitrary"))(*operands, *side.ins)


def _side_call(side, name):
    n_sin, n_sout = len(side.ins), len(side.out_shape)

    def body(*refs):
        cps = side.copies(refs[:n_sin], refs[n_sin:n_sin + n_sout], refs[-2], refs[-1])
        for cp in cps:
            cp.start()
        for cp in cps:
            cp.wait_recv()
        for cp in cps:
            cp.wait_send()

    return pl.pallas_call(
        body, name=name, in_specs=[ANY] * n_sin, out_specs=[ANY] * n_sout, out_shape=side.out_shape,
        scratch_shapes=[pltpu.SemaphoreType.DMA((side.n,)), pltpu.SemaphoreType.DMA((side.n,))],
        input_output_aliases=dict(side.aliases))(*side.ins)


def _listed(result):
    return list(result) if isinstance(result, (list, tuple)) else [result]


def _reduce_core(dot, steps):
    if steps == 1:
        def core(a_ref, b_ref, o_ref):
            o_ref[...] = dot(a_ref[...], b_ref[...]).astype(o_ref.dtype)
        return core

    def core(a_ref, b_ref, o_ref, acc):
        r = pl.program_id(2)

        @pl.when(r == 0)
        def _():
            acc[...] = jnp.zeros_like(acc)

        acc[...] += dot(a_ref[...], b_ref[...])

        @pl.when(r == steps - 1)
        def _():
            o_ref[...] = acc[...].astype(o_ref.dtype)
    return core


def _acc(steps, shape):
    return [] if steps == 1 else [pltpu.VMEM(shape, F32)]


def _mm_nn(a, w, out_dtype, name, side=None):
    m, k = a.shape
    s, _, n = w.shape
    tm, tn, tk = _tile(m, 1024, 16), _tile(n, 1536, LANES), _tile(k, 2816, LANES)
    nb, nk = n // tn, k // tk
    return _grid_call(
        _reduce_core(_dot_nn, nk), name, (m // tm, s * nb, nk),
        [pl.BlockSpec((tm, tk), lambda i, j, kk: (i, kk)),
         pl.BlockSpec((None, tk, tn), lambda i, j, kk: (j // nb, kk, j % nb))],
        pl.BlockSpec((tm, tn), lambda i, j, kk: (i, j)), jax.ShapeDtypeStruct((m, s * n), out_dtype),
        _acc(nk, (tm, tn)), (a, w), side)


def _mm_nt(dy, w, out_dtype, name, side=None):
    m = dy.shape[0]
    s, k, n = w.shape
    tm, tn, tko = _tile(m, 1024, 16), _tile(n, 2816, LANES), _tile(k, 1536, LANES)
    nb = n // tn
    nr = s * nb
    return _grid_call(
        _reduce_core(_dot_nt, nr), name, (m // tm, k // tko, nr),
        [pl.BlockSpec((tm, tn), lambda i, j, r: (i, r)),
         pl.BlockSpec((None, tko, tn), lambda i, j, r: (r // nb, j, r % nb))],
        pl.BlockSpec((tm, tko), lambda i, j, r: (i, j)), jax.ShapeDtypeStruct((m, k), out_dtype),
        _acc(nr, (tm, tko)), (dy, w), side)


def _mm_tn(x, dy, s, name, side=None):
    t, k = x.shape
    n = dy.shape[1] // s
    tk, tn, tt = _tile(k, 1024, LANES), _tile(n, 1536, LANES), _tile(t, 2048, 16)
    nb, nt = n // tn, t // tt
    return _grid_call(
        _reduce_core(_dot_tn, nt), name, (k // tk, s * nb, nt),
        [pl.BlockSpec((tt, tk), lambda i, j, q: (q, i)),
         pl.BlockSpec((tt, tn), lambda i, j, q: (q, j))],
        pl.BlockSpec((None, tk, tn), lambda i, j, q: (j // nb, i, j % nb)), jax.ShapeDtypeStruct((s, k, n), BF16),
        _acc(nt, (tk, tn)), (x, dy), side)


def _rms(x, g):
    return x * lax.rsqrt(jnp.mean(x * x, axis=-1, keepdims=True) + EPS) * g


def _rms_bwd(x, g, dy):
    r = lax.rsqrt(jnp.mean(x * x, axis=-1, keepdims=True) + EPS)
    gy = dy * g
    dx = r * gy - x * (r * r * r * jnp.mean(x * gy, axis=-1, keepdims=True))
    return dx, dy * (x * r)


def _row_spec(tt, d):
    return pl.BlockSpec((tt, d), lambda i: (i, 0))


def _vec_spec(d):
    return pl.BlockSpec((1, d), lambda i: (0, 0))


def _rms_fwd(x, g, name):
    t, d = x.shape
    tt = _tile(t, 512, 16)

    def body(x_ref, g_ref, o_ref):
        o_ref[...] = _rms(x_ref[...], g_ref[...]).astype(o_ref.dtype)

    return pl.pallas_call(
        body, name=name, grid=(t // tt,),
        in_specs=[_row_spec(tt, d), _vec_spec(d)], out_specs=_row_spec(tt, d),
        out_shape=jax.ShapeDtypeStruct((t, d), BF16), compiler_params=_params("parallel"),
    )(x, g)


def _norm_step(h, m, g_post, g_next, name):
    t, d = h.shape
    tt = _tile(t, 512, 16)

    def body(h_ref, m_ref, gp_ref, gn_ref, ho_ref, xn_ref):
        hn = h_ref[...] + _rms(m_ref[...], gp_ref[...])
        ho_ref[...] = hn
        xn_ref[...] = _rms(hn, gn_ref[...]).astype(xn_ref.dtype)

    return pl.pallas_call(
        body, name=name, grid=(t // tt,),
        in_specs=[_row_spec(tt, d), _row_spec(tt, d), _vec_spec(d), _vec_spec(d)],
        out_specs=[_row_spec(tt, d), _row_spec(tt, d)],
        out_shape=[jax.ShapeDtypeStruct((t, d), F32), jax.ShapeDtypeStruct((t, d), BF16)],
        compiler_params=_params("parallel"),
    )(h, m, g_post, g_next)


def _final_loss(h, f, g_post, target, name):
    t, d = h.shape
    tt = _tile(t, 256, 16)

    def body(h_ref, f_ref, g_ref, tg_ref, loss_ref, dh_ref, df_ref, dg_ref):
        @pl.when(pl.program_id(0) == 0)
        def _():
            loss_ref[...] = jnp.zeros_like(loss_ref)
            dg_ref[...] = jnp.zeros_like(dg_ref)

        fv, g = f_ref[...], g_ref[...]
        err = h_ref[...] + _rms(fv, g) - tg_ref[...]
        per_row = jnp.mean(err * err, axis=-1, keepdims=True)
        loss_ref[...] += 0.5 * jnp.sum(per_row, axis=0, keepdims=True)
        dh = err * (1.0 / d)
        dh_ref[...] = dh
        df, dg = _rms_bwd(fv, g, dh)
        df_ref[...] = df.astype(df_ref.dtype)
        dg_ref[...] += jnp.sum(dg, axis=0, keepdims=True)

    return pl.pallas_call(
        body, name=name, grid=(t // tt,),
        in_specs=[_row_spec(tt, d), _row_spec(tt, d), _vec_spec(d), _row_spec(tt, d)],
        out_specs=[pl.BlockSpec((1, 1), lambda i: (0, 0)), _row_spec(tt, d), _row_spec(tt, d), _vec_spec(d)],
        out_shape=[jax.ShapeDtypeStruct((1, 1), F32), jax.ShapeDtypeStruct((t, d), F32),
                   jax.ShapeDtypeStruct((t, d), BF16), jax.ShapeDtypeStruct((1, d), F32)],
        compiler_params=_params("arbitrary"),
    )(h, f, g_post, target)


def _norm_bwd_step(dh_out, dxn, x, g_pre, prev, name):
    t, d = x.shape
    tt = _tile(t, 256, 16)
    has_prev = prev is not None

    def body(*refs):
        if has_prev:
            dho_ref, dxn_ref, x_ref, g_ref, xa_ref, ga_ref, dh_ref, dg_ref, da_ref, dga_ref = refs
        else:
            dho_ref, dxn_ref, x_ref, g_ref, dh_ref, dg_ref = refs

        @pl.when(pl.program_id(0) == 0)
        def _():
            dg_ref[...] = jnp.zeros_like(dg_ref)
            if has_prev:
                dga_ref[...] = jnp.zeros_like(dga_ref)

        dx, dg = _rms_bwd(x_ref[...], g_ref[...], dxn_ref[...])
        dh = dho_ref[...] + dx
        dh_ref[...] = dh
        dg_ref[...] += jnp.sum(dg, axis=0, keepdims=True)
        if has_prev:
            da, dga = _rms_bwd(xa_ref[...], ga_ref[...], dh)
            da_ref[...] = da.astype(da_ref.dtype)
            dga_ref[...] += jnp.sum(dga, axis=0, keepdims=True)

    ins = [dh_out, dxn, x, g_pre] + (list(prev) if has_prev else [])
    in_specs = [_row_spec(tt, d)] * 3 + [_vec_spec(d)] + ([_row_spec(tt, d), _vec_spec(d)] if has_prev else [])
    out_specs = [_row_spec(tt, d), _vec_spec(d)] + ([_row_spec(tt, d), _vec_spec(d)] if has_prev else [])
    out_shape = [jax.ShapeDtypeStruct((t, d), F32), jax.ShapeDtypeStruct((1, d), F32)]
    if has_prev:
        out_shape += [jax.ShapeDtypeStruct((t, d), BF16), jax.ShapeDtypeStruct((1, d), F32)]
    return pl.pallas_call(
        body, name=name, grid=(t // tt,), in_specs=in_specs, out_specs=out_specs, out_shape=out_shape,
        compiler_params=_params("arbitrary"),
    )(*ins)


def _shift_down(x, k, halo):
    r = pltpu.roll(x, k, 0)
    rh = pltpu.roll(halo, k, 0)
    row = lax.broadcasted_iota(jnp.int32, rh.shape, 0)
    return jnp.concatenate([jnp.where(row < k, rh, r[:HALO]), r[HALO:]], axis=0)


def _shift_up(x_ext, k, n):
    return pltpu.roll(x_ext, x_ext.shape[0] - k, 0)[:n]


def _halo_specs(tt, width, nblk):
    per = tt // HALO
    prev = pl.BlockSpec((HALO, width), lambda i: (jnp.maximum(i * per - 1, 0), 0))
    nxt = pl.BlockSpec((HALO, width), lambda i: (jnp.minimum((i + 1) * per, nblk * per - 1), 0))
    return prev, nxt


def _conv3(x, halo, w0, w1, w2):
    return w0 * _shift_down(x, 2, halo) + w1 * _shift_down(x, 1, halo) + w2 * x


def _sc_fwd(p, wc, name):
    t, c3 = p.shape
    c = c3 // 3
    tt, cc = _tile(t, 256, 16), _tile(c, 512, LANES)
    nblk = t // tt
    prev_spec, _ = _halo_specs(tt, c3, nblk)

    def body(p_ref, pp_ref, w_ref, y_ref):
        first = pl.program_id(0) == 0
        for j in range(c // cc):
            cols = slice(j * cc, (j + 1) * cc)
            gc, hv = p_ref[:, c + j * cc:c + (j + 1) * cc], p_ref[:, 2 * c + j * cc:2 * c + (j + 1) * cc]
            uh = jnp.where(first, 0.0, pp_ref[:, c + j * cc:c + (j + 1) * cc] * pp_ref[:, 2 * c + j * cc:2 * c + (j + 1) * cc])
            cv = _conv3(gc * hv, uh, w_ref[0:1, cols], w_ref[1:2, cols], w_ref[2:3, cols])
            y_ref[:, cols] = (p_ref[:, cols] * cv).astype(y_ref.dtype)

    return pl.pallas_call(
        body, name=name, grid=(nblk,),
        in_specs=[_row_spec(tt, c3), prev_spec, pl.BlockSpec((3, c), lambda i: (0, 0))],
        out_specs=_row_spec(tt, c), out_shape=jax.ShapeDtypeStruct((t, c), BF16),
        compiler_params=_params("parallel"),
    )(p, p, wc)


def _sc_bwd(p, dy, wc, name):
    t, c3 = p.shape
    c = c3 // 3
    tt, cc = _tile(t, 256, 16), _tile(c, 512, LANES)
    nblk = t // tt
    p_prev, p_next = _halo_specs(tt, c3, nblk)
    _, dy_next = _halo_specs(tt, c, nblk)

    def body(p_ref, pp_ref, pn_ref, dy_ref, dyn_ref, w_ref, dp_ref, dw_ref):
        i = pl.program_id(0)
        first, last = i == 0, i == nblk - 1

        @pl.when(first)
        def _():
            dw_ref[...] = jnp.zeros_like(dw_ref)

        for j in range(c // cc):
            a, b, d = slice(j * cc, (j + 1) * cc), slice(c + j * cc, c + (j + 1) * cc), slice(2 * c + j * cc, 2 * c + (j + 1) * cc)
            w0, w1, w2 = w_ref[0:1, a], w_ref[1:2, a], w_ref[2:3, a]
            gb, gc, hv, dyv = p_ref[:, a], p_ref[:, b], p_ref[:, d], dy_ref[:, a]
            u = gc * hv
            uh = jnp.where(first, 0.0, pp_ref[:, b] * pp_ref[:, d])
            u2, u1 = _shift_down(u, 2, uh), _shift_down(u, 1, uh)
            cv = w0 * u2 + w1 * u1 + w2 * u
            dcv = dyv * gb
            dcv_ext = jnp.concatenate([dcv, jnp.where(last, 0.0, dyn_ref[:, a] * pn_ref[:, a])], axis=0)
            du = w2 * dcv + w1 * _shift_up(dcv_ext, 1, tt) + w0 * _shift_up(dcv_ext, 2, tt)
            dp_ref[:, a] = (dyv * cv).astype(dp_ref.dtype)
            dp_ref[:, b] = (du * hv).astype(dp_ref.dtype)
            dp_ref[:, d] = (du * gc).astype(dp_ref.dtype)
            dw_ref[0:1, a] += jnp.sum(dcv * u2, axis=0, keepdims=True)
            dw_ref[1:2, a] += jnp.sum(dcv * u1, axis=0, keepdims=True)
            dw_ref[2:3, a] += jnp.sum(dcv * u, axis=0, keepdims=True)

    return pl.pallas_call(
        body, name=name, grid=(nblk,),
        in_specs=[_row_spec(tt, c3), p_prev, p_next, _row_spec(tt, c), dy_next, pl.BlockSpec((3, c), lambda i: (0, 0))],
        out_specs=[_row_spec(tt, c3), pl.BlockSpec((HALO, c), lambda i: (0, 0))],
        out_shape=[jax.ShapeDtypeStruct((t, c3), BF16), jax.ShapeDtypeStruct((HALO, c), F32)],
        compiler_params=_params("arbitrary"),
    )(p, p, p, dy, dy, wc)


def _ffn_fwd(u, cw, cb, name):
    t, f2 = u.shape
    f = f2 // 2
    tt, cc = _tile(t, 128, 16), _tile(f, 512, LANES)
    nblk = t // tt
    prev_spec, _ = _halo_specs(tt, f2, nblk)

    def body(u_ref, up_ref, w_ref, b_ref, a_ref):
        first = pl.program_id(0) == 0

        def conv(cols):
            halo = jnp.where(first, 0.0, up_ref[:, cols])
            return _conv3(u_ref[:, cols], halo, w_ref[0:1, cols], w_ref[1:2, cols], w_ref[2:3, cols]) + b_ref[:, cols]

        for j in range(f // cc):
            hg, hv = conv(slice(j * cc, (j + 1) * cc)), conv(slice(f + j * cc, f + (j + 1) * cc))
            a_ref[:, j * cc:(j + 1) * cc] = (hg * jax.nn.sigmoid(hg) * hv).astype(a_ref.dtype)

    return pl.pallas_call(
        body, name=name, grid=(nblk,),
        in_specs=[_row_spec(tt, f2), prev_spec, pl.BlockSpec((3, f2), lambda i: (0, 0)), _vec_spec(f2)],
        out_specs=_row_spec(tt, f), out_shape=jax.ShapeDtypeStruct((t, f), BF16),
        compiler_params=_params("parallel"),
    )(u, u, cw, cb)


def _ffn_bwd(u, da, cw, cb, name):
    t, f2 = u.shape
    f = f2 // 2
    tt, cc = _tile(t, 128, 16), _tile(f, 128, LANES)
    nblk = t // tt
    u_prev, u_next = _halo_specs(tt, f2, nblk)
    _, da_next = _halo_specs(tt, f, nblk)

    def body(u_ref, up_ref, un_ref, da_ref, dan_ref, w_ref, b_ref, du_ref, dw_ref):
        i = pl.program_id(0)
        first, last = i == 0, i == nblk - 1

        @pl.when(first)
        def _():
            dw_ref[...] = jnp.zeros_like(dw_ref)

        keep = jnp.where(last, 0.0, 1.0)

        def ext(cols):
            x = jnp.concatenate([u_ref[:, cols], un_ref[:, cols]], axis=0)
            halo = jnp.where(first, 0.0, up_ref[:, cols])
            x2, x1 = _shift_down(x, 2, halo), _shift_down(x, 1, halo)
            h = w_ref[0:1, cols] * x2 + w_ref[1:2, cols] * x1 + w_ref[2:3, cols] * x + b_ref[:, cols]
            return x, x1, x2, h

        def back(cols, dh_ext, x, x1, x2):
            w0, w1, w2 = w_ref[0:1, cols], w_ref[1:2, cols], w_ref[2:3, cols]
            dh = dh_ext[:tt]
            du_ref[:, cols] = (w2 * dh + w1 * _shift_up(dh_ext, 1, tt) + w0 * _shift_up(dh_ext, 2, tt)).astype(du_ref.dtype)
            dw_ref[0:1, cols] += jnp.sum(dh * x2[:tt], axis=0, keepdims=True)
            dw_ref[1:2, cols] += jnp.sum(dh * x1[:tt], axis=0, keepdims=True)
            dw_ref[2:3, cols] += jnp.sum(dh * x[:tt], axis=0, keepdims=True)
            dw_ref[3:4, cols] += jnp.sum(dh, axis=0, keepdims=True)

        row = lax.broadcasted_iota(jnp.int32, (tt + HALO, 1), 0)
        for j in range(f // cc):
            gcols, vcols = slice(j * cc, (j + 1) * cc), slice(f + j * cc, f + (j + 1) * cc)
            xg, xg1, xg2, hg = ext(gcols)
            xv, xv1, xv2, hv = ext(vcols)
            da_ext = jnp.concatenate([da_ref[:, gcols], dan_ref[:, gcols]], axis=0)
            da_ext = jnp.where(row < tt, da_ext, da_ext * keep)
            sg = jax.nn.sigmoid(hg)
            back(gcols, da_ext * hv * (sg * (1.0 + hg * (1.0 - sg))), xg, xg1, xg2)
            back(vcols, da_ext * (hg * sg), xv, xv1, xv2)

    return pl.pallas_call(
        body, name=name, grid=(nblk,),
        in_specs=[_row_spec(tt, f2), u_prev, u_next, _row_spec(tt, f), da_next,
                  pl.BlockSpec((3, f2), lambda i: (0, 0)), _vec_spec(f2)],
        out_specs=[_row_spec(tt, f2), pl.BlockSpec((HALO, f2), lambda i: (0, 0))],
        out_shape=[jax.ShapeDtypeStruct((t, f2), BF16), jax.ShapeDtypeStruct((HALO, f2), F32)],
        compiler_params=_params("arbitrary"),
    )(u, u, u, da, da, cw, cb)


def _gelu(x):
    cdf = 0.5 * (1.0 + lax.erf(x * INV_SQRT2))
    return x * cdf, cdf + x * (jnp.exp(-0.5 * x * x) * INV_SQRT_2PI)


def _sgu_common(p_ref, lg_ref, lb_ref, ws_ref, c, ch, groups):
    u, du_dp = _gelu(p_ref[:, :c])
    v, dv_dp = _gelu(p_ref[:, c:])
    mu = jnp.mean(v, axis=-1, keepdims=True)
    vc = v - mu
    rstd = lax.rsqrt(jnp.mean(vc * vc, axis=-1, keepdims=True) + EPS)
    xhat = vc * rstd
    vn = (xhat * lg_ref[...] + lb_ref[...]).astype(BF16)
    tril = lax.broadcasted_iota(jnp.int32, (ch, ch), 0) >= lax.broadcasted_iota(jnp.int32, (ch, ch), 1)
    wm = [jnp.where(tril, ws_ref[g], 0.0).astype(BF16) for g in range(groups)]
    return u, du_dp, dv_dp, xhat, rstd, vn, wm, tril


def _lane_pick(x, g):
    lane = lax.broadcasted_iota(jnp.int32, x.shape, 1)
    return jnp.sum(jnp.where(lane == g, x, 0.0), axis=1, keepdims=True)


def _sgu_specs(tt, c, ch, groups):
    return [_row_spec(tt, 2 * c), _vec_spec(c), _vec_spec(c),
            pl.BlockSpec((groups, ch, ch), lambda i: (0, 0, 0)), pl.BlockSpec((ch, LANES), lambda i: (0, 0))]


def _sgu_fwd(p, lg, lb, ws, bs_t, name):
    t, c2 = p.shape
    c = c2 // 2
    groups, ch, _ = ws.shape
    gc = c // groups
    tt = _tile(t, 2 * ch, ch)

    def body(p_ref, lg_ref, lb_ref, ws_ref, bs_ref, y_ref):
        u, _, _, _, _, vn, wm, _ = _sgu_common(p_ref, lg_ref, lb_ref, ws_ref, c, ch, groups)
        bs = bs_ref[...]
        for r in range(tt // ch):
            rows = slice(r * ch, (r + 1) * ch)
            for g in range(groups):
                cols = slice(g * gc, (g + 1) * gc)
                mixed = _dot_nn(wm[g], vn[rows, cols]) + _lane_pick(bs, g)
                y_ref[rows, cols] = (u[rows, cols] * mixed).astype(y_ref.dtype)

    return pl.pallas_call(
        body, name=name, grid=(t // tt,), in_specs=_sgu_specs(tt, c, ch, groups),
        out_specs=_row_spec(tt, c), out_shape=jax.ShapeDtypeStruct((t, c), BF16),
        compiler_params=_params("parallel"),
    )(p, lg, lb, ws, bs_t)


def _sgu_bwd(p, dy, lg, lb, ws, bs_t, name):
    t, c2 = p.shape
    c = c2 // 2
    groups, ch, _ = ws.shape
    gc = c // groups
    tt = _tile(t, 2 * ch, ch)

    def body(p_ref, dy_ref, lg_ref, lb_ref, ws_ref, bs_ref, dp_ref, dws_ref, dbs_ref, dlg_ref, dlb_ref, dvn_ref):
        @pl.when(pl.program_id(0) == 0)
        def _():
            dws_ref[...] = jnp.zeros_like(dws_ref)
            dbs_ref[...] = jnp.zeros_like(dbs_ref)
            dlg_ref[...] = jnp.zeros_like(dlg_ref)
            dlb_ref[...] = jnp.zeros_like(dlb_ref)

        u, du_dp, dv_dp, xhat, rstd, vn, wm, tril = _sgu_common(p_ref, lg_ref, lb_ref, ws_ref, c, ch, groups)
        bs = bs_ref[...]
        lane = lax.broadcasted_iota(jnp.int32, (ch, LANES), 1)
        for r in range(tt // ch):
            rows = slice(r * ch, (r + 1) * ch)
            for g in range(groups):
                cols = slice(g * gc, (g + 1) * gc)
                dyv, vng = dy_ref[rows, cols], vn[rows, cols]
                mixed = _dot_nn(wm[g], vng) + _lane_pick(bs, g)
                dp_ref[rows, cols] = (dyv * mixed * du_dp[rows, cols]).astype(dp_ref.dtype)
                dmixed = dyv * u[rows, cols]
                dmb = dmixed.astype(BF16)
                dws_ref[g] += jnp.where(tril, _dot_nt(dmb, vng), 0.0)
                dbs_ref[...] += jnp.where(lane == g, jnp.sum(dmixed, axis=1, keepdims=True), 0.0)
                dvn_ref[rows, cols] = _dot_tn(wm[g], dmb)
        dvn = dvn_ref[...]
        dlg_ref[...] += jnp.sum(dvn * xhat, axis=0, keepdims=True)
        dlb_ref[...] += jnp.sum(dvn, axis=0, keepdims=True)
        dxh = dvn * lg_ref[...]
        dv = rstd * (dxh - jnp.mean(dxh, axis=-1, keepdims=True) - xhat * jnp.mean(dxh * xhat, axis=-1, keepdims=True))
        dp_ref[:, c:] = (dv * dv_dp).astype(dp_ref.dtype)

    specs = _sgu_specs(tt, c, ch, groups)
    return pl.pallas_call(
        body, name=name, grid=(t // tt,),
        in_specs=[specs[0], _row_spec(tt, c)] + specs[1:],
        out_specs=[_row_spec(tt, c2), pl.BlockSpec((groups, ch, ch), lambda i: (0, 0, 0)),
                   pl.BlockSpec((ch, LANES), lambda i: (0, 0)), _vec_spec(c), _vec_spec(c)],
        out_shape=[jax.ShapeDtypeStruct((t, c2), BF16), jax.ShapeDtypeStruct((groups, ch, ch), F32),
                   jax.ShapeDtypeStruct((ch, LANES), F32), jax.ShapeDtypeStruct((1, c), F32), jax.ShapeDtypeStruct((1, c), F32)],
        scratch_shapes=[pltpu.VMEM((tt, c), F32)],
        compiler_params=_params("arbitrary"),
    )(p, dy, lg, lb, ws, bs_t)


def _split(x):
    hi = x.astype(BF16)
    return hi, (x - hi.astype(F32)).astype(BF16)


def _sb_block(q, ks, qpos, kb, tk, scale, r_carry, tri, valid=True):
    z = _dot_nt(q, ks) * scale
    kpos = kb * tk + lax.broadcasted_iota(jnp.int32, (1, tk), 1)
    mask = jnp.logical_and(kpos < qpos, valid)
    e = jnp.exp(-jnp.abs(z))
    lm = jnp.where(mask, -(jnp.maximum(z, 0.0) + jnp.log(1.0 + e)), 0.0)
    hi, lo = _split(lm)
    inc = _dot_nn(hi, tri) + _dot_nn(lo, tri)
    att = jnp.where(mask, jnp.exp(z + lm + (inc - lm + r_carry)), 0.0)
    return z, mask, e, inc, att


def _earlier_blocks(i, step, carry):
    def guarded(_, c):
        return lax.cond(c[-1] > EXP_UNDERFLOW, step, lambda same: same, c)

    return lax.fori_loop(2, i + 1, guarded, carry)


def _suffix_ones(tk):
    return (lax.broadcasted_iota(jnp.int32, (tk, tk), 0) >= lax.broadcasted_iota(jnp.int32, (tk, tk), 1)).astype(BF16)


def _attn_fwd(qkv, name):
    t, d3 = qkv.shape
    d = d3 // 3
    heads = d // HEAD_DIM
    tq = _tile(t, 256, LANES)
    nq = t // tq
    scale = HEAD_DIM ** -0.5

    def body(q_ref, k_ref, v_ref, of_ref, ob_ref):
        i = pl.program_id(1)
        q = q_ref[...]
        qpos = i * tq + lax.broadcasted_iota(jnp.int32, (tq, 1), 0)
        tri = _suffix_ones(tq)

        def step(carry, valid=True):
            n, o, r_carry, _ = carry
            kb = jnp.maximum(i - n, 0)
            rows = pl.ds(pl.multiple_of(kb * tq, tq), tq)
            _, _, _, inc, att = _sb_block(q, k_ref[rows, :], qpos, kb, tq, scale, r_carry, tri, valid)
            ahi, alo = _split(att)
            vs = v_ref[rows, :]
            r_new = r_carry + inc[:, 0:1]
            return n + 1, o + _dot_nn(ahi, vs) + _dot_nn(alo, vs), r_new, jnp.max(r_new)

        init = (jnp.int32(0), jnp.zeros((tq, HEAD_DIM), F32), jnp.zeros((tq, 1), F32), jnp.float32(0.0))
        _, o, _, _ = _earlier_blocks(i, step, step(step(init), i >= 1))
        of_ref[...] = o
        ob_ref[...] = o.astype(ob_ref.dtype)

    return pl.pallas_call(
        body, name=name, grid=(heads, nq),
        in_specs=[pl.BlockSpec((tq, HEAD_DIM), lambda h, i: (i, h)),
                  pl.BlockSpec((t, HEAD_DIM), lambda h, i: (0, heads + h)),
                  pl.BlockSpec((t, HEAD_DIM), lambda h, i: (0, 2 * heads + h))],
        out_specs=[pl.BlockSpec((tq, HEAD_DIM), lambda h, i: (i, h))] * 2,
        out_shape=[jax.ShapeDtypeStruct((t, d), F32), jax.ShapeDtypeStruct((t, d), BF16)],
        compiler_params=_params("parallel", "parallel"),
    )(qkv, qkv, qkv)


def _attn_bwd(qkv, do, of, name):
    t, d3 = qkv.shape
    d = d3 // 3
    heads = d // HEAD_DIM
    tq = _tile(t, 256, LANES)
    nq = t // tq
    scale = HEAD_DIM ** -0.5

    def body(q_ref, k_ref, v_ref, do_ref, of_ref, dq_ref, dk_ref, dv_ref, dk_acc, dv_acc):
        i = pl.program_id(1)

        @pl.when(i == 0)
        def _():
            dk_acc[...] = jnp.zeros_like(dk_acc)
            dv_acc[...] = jnp.zeros_like(dv_acc)

        q, dov = q_ref[...], do_ref[...]
        delta = jnp.sum(dov.astype(F32) * of_ref[...], axis=-1, keepdims=True)
        qpos = i * tq + lax.broadcasted_iota(jnp.int32, (tq, 1), 0)
        tri = _suffix_ones(tq)

        def step(carry, valid=True):
            n, dq, r_carry, g_carry, _ = carry
            kb = jnp.maximum(i - n, 0)
            rows = pl.ds(pl.multiple_of(kb * tq, tq), tq)
            ks, vs = k_ref[rows, :], v_ref[rows, :]
            z, mask, e, inc, att = _sb_block(q, ks, qpos, kb, tq, scale, r_carry, tri, valid)
            g = _dot_nt(dov, vs) * att
            ghi, glo = _split(g)
            ginc = _dot_nn(ghi, tri) + _dot_nn(glo, tri)
            beta = jnp.where(z >= 0.0, 1.0, e) / (1.0 + e)
            dz = jnp.where(mask, g * (1.0 - beta) - (delta - g_carry - ginc) * beta, 0.0) * scale
            dzb = dz.astype(BF16)
            dk_acc[rows, :] += _dot_tn(dzb, q)
            dv_acc[rows, :] += _dot_tn(att.astype(BF16), dov)
            r_new = r_carry + inc[:, 0:1]
            return n + 1, dq + _dot_nn(dzb, ks), r_new, g_carry + ginc[:, 0:1], jnp.max(r_new)

        zero = jnp.zeros((tq, 1), F32)
        init = (jnp.int32(0), jnp.zeros((tq, HEAD_DIM), F32), zero, zero, jnp.float32(0.0))
        _, dq, _, _, _ = _earlier_blocks(i, step, step(step(init), i >= 1))
        dq_ref[...] = dq.astype(dq_ref.dtype)

        @pl.when(i == nq - 1)
        def _():
            dk_ref[...] = dk_acc[...].astype(dk_ref.dtype)
            dv_ref[...] = dv_acc[...].astype(dv_ref.dtype)

    blk = pl.BlockSpec((tq, HEAD_DIM), lambda h, i: (i, h))
    full = pl.BlockSpec((t, HEAD_DIM), lambda h, i: (0, h))
    return pl.pallas_call(
        body, name=name, grid=(heads, nq),
        in_specs=[blk, pl.BlockSpec((t, HEAD_DIM), lambda h, i: (0, heads + h)),
                  pl.BlockSpec((t, HEAD_DIM), lambda h, i: (0, 2 * heads + h)), blk, blk],
        out_specs=[blk, full, full],
        out_shape=[jax.ShapeDtypeStruct((t, d), BF16)] * 3,
        scratch_shapes=[pltpu.VMEM((t, HEAD_DIM), F32), pltpu.VMEM((t, HEAD_DIM), F32)],
        compiler_params=_params("parallel", "arbitrary"),
    )(qkv, qkv, qkv, do, of)


def _pair_add(g, pair, name):
    _, _, rh, c = g.shape
    tr = _tile(rh, 256, 16)
    core = lax.axis_index("c").astype(jnp.int32).reshape(1)

    def body(c_ref, g_ref, p_ref, o_ref):
        o_ref[...] = (g_ref[...].astype(F32) + p_ref[...].astype(F32)).astype(o_ref.dtype)

    return pl.pallas_call(
        body, name=name,
        grid_spec=pltpu.PrefetchScalarGridSpec(
            num_scalar_prefetch=1, grid=(4, rh // tr),
            in_specs=[pl.BlockSpec((None, None, tr, c), lambda j, r, cr: (j, cr[0], r, 0)),
                      pl.BlockSpec((None, tr, c), lambda j, r, cr: (j, r, 0))],
            out_specs=pl.BlockSpec((None, tr, c), lambda j, r, cr: (j, r, 0))),
        out_shape=jax.ShapeDtypeStruct((4, rh, c), BF16),
        compiler_params=_params("parallel", "parallel"),
    )(core, g, pair)


def _sum_slots(x, name):
    n, r, c = x.shape
    tr = _tile(r, 256, 16)

    def body(x_ref, o_ref):
        acc = x_ref[0].astype(F32)
        for k in range(1, n):
            acc = acc + x_ref[k].astype(F32)
        o_ref[...] = acc

    return pl.pallas_call(
        body, name=name, grid=(r // tr,),
        in_specs=[pl.BlockSpec((n, tr, c), lambda i: (0, i, 0))],
        out_specs=pl.BlockSpec((tr, c), lambda i: (i, 0)),
        out_shape=jax.ShapeDtypeStruct((r, c), F32), compiler_params=_params("parallel"),
    )(x)


def _sum_own_slots(sums, slots, name):
    _, rh, c = sums.shape
    tr = _tile(rh, 256, 16)
    chip = (2 * lax.axis_index("x") + lax.axis_index("y")).astype(jnp.int32).reshape(1)

    def body(chip_ref, own_ref, slot_ref, o_ref):
        acc = own_ref[...].astype(F32)
        for k in range(3):
            acc = acc + slot_ref[k].astype(F32)
        o_ref[...] = acc

    return pl.pallas_call(
        body, name=name,
        grid_spec=pltpu.PrefetchScalarGridSpec(
            num_scalar_prefetch=1, grid=(rh // tr,),
            in_specs=[pl.BlockSpec((None, tr, c), lambda r, me: (me[0], r, 0)),
                      pl.BlockSpec((3, tr, c), lambda r, me: (0, r, 0))],
            out_specs=pl.BlockSpec((tr, c), lambda r, me: (r, 0))),
        out_shape=jax.ShapeDtypeStruct((rh, c), F32), compiler_params=_params("parallel"),
    )(chip, sums, slots)


def _adam_update(w, g, m, v):
    mn = ADAM_B1 * m + (1.0 - ADAM_B1) * g
    vn = ADAM_B2 * v + (1.0 - ADAM_B2) * (g * g)
    c1, c2 = 1.0 - ADAM_B1 ** ADAM_STEP, 1.0 - ADAM_B2 ** ADAM_STEP
    return -ADAM_LR * ((mn / c1) / (jnp.sqrt(vn / c2) + ADAM_EPS) + ADAM_WD * w), mn, vn


def _adamw_halves(w, own, recv, m, v, name):
    layers, _, rh, c = w.shape
    tr = _tile(rh, max(8, (1 << 18) // c), 8)
    core = lax.axis_index("c").astype(jnp.int32).reshape(1)

    def body(c_ref, w_ref, own_ref, recv_ref, m_ref, v_ref, g_ref, d_ref, mo_ref, vo_ref):
        g = jnp.where(pl.program_id(1) == c_ref[0], own_ref[...], recv_ref[...])
        g_ref[...] = g
        d_ref[...], mo_ref[...], vo_ref[...] = _adam_update(w_ref[...], g, m_ref[...], v_ref[...])

    full = pl.BlockSpec((None, None, tr, c), lambda l, hf, r, cr: (l, hf, r, 0))
    half = pl.BlockSpec((None, tr, c), lambda l, hf, r, cr: (l, r, 0))
    return pl.pallas_call(
        body, name=name,
        grid_spec=pltpu.PrefetchScalarGridSpec(
            num_scalar_prefetch=1, grid=(layers, 2, rh // tr),
            in_specs=[full, half, half, full, full], out_specs=[full] * 4),
        out_shape=[jax.ShapeDtypeStruct(w.shape, F32)] * 4,
        compiler_params=_params("parallel", "parallel", "parallel"),
    )(core, w, own, recv, m, v)


def _adamw(w, g, m, v, name):
    r, c = w.shape
    tr = _tile(r, max(8, (1 << 18) // c), 8)

    def body(w_ref, g_ref, m_ref, v_ref, d_ref, mo_ref, vo_ref):
        d_ref[...], mo_ref[...], vo_ref[...] = _adam_update(w_ref[...], g_ref[...], m_ref[...], v_ref[...])

    spec = pl.BlockSpec((tr, c), lambda i: (i, 0))
    return pl.pallas_call(
        body, name=name, grid=(r // tr,), in_specs=[spec] * 4, out_specs=[spec] * 3,
        out_shape=[jax.ShapeDtypeStruct((r, c), F32)] * 3, compiler_params=_params("parallel"),
    )(w, g, m, v)


def _place():
    x, y, c = lax.axis_index("x"), lax.axis_index("y"), lax.axis_index("c")
    chips = [(1 - x, y), (x, 1 - y), (1 - x, 1 - y)]
    return x, y, c, chips


def _remote(src, dst, send_sems, recv_sems, k, to):
    return pltpu.make_async_remote_copy(src_ref=src, dst_ref=dst, send_sem=send_sems.at[k], recv_sem=recv_sems.at[k],
                                        device_id=to, device_id_type=MESH)


def _comm_call(body, name, ins, out_shape, n_remote, n_local=0):
    sems = [pltpu.SemaphoreType.DMA((n_remote,)), pltpu.SemaphoreType.DMA((n_remote,))]
    if n_local:
        sems.append(pltpu.SemaphoreType.DMA((n_local,)))
    return pl.pallas_call(
        body, name=name, in_specs=[ANY] * len(ins), out_specs=[ANY] * len(out_shape), out_shape=out_shape,
        scratch_shapes=sems,
    )(*ins)


def _all_gather(shards, name):
    n = len(shards)

    def body(*refs):
        ins, outs = refs[:n], refs[n:2 * n]
        send_sems, recv_sems = refs[2 * n:]
        x, y, c, chips = _place()
        me, sib = 2 * x + y, (x, y, 1 - c)
        started = []
        for a in range(n):
            cp = _remote(ins[a], outs[a].at[me], send_sems, recv_sems, 7 * a + 6, sib)
            cp.start()
            started.append(cp)
            for k, (px, py) in enumerate(chips):
                cp = _remote(ins[a].at[c], outs[a].at[me, c], send_sems, recv_sems, 7 * a + k, (px, py, c))
                cp.start()
                started.append(cp)
        for a in range(n):
            for k, (px, py) in enumerate(chips):
                land = outs[a].at[2 * px + py, c]
                _remote(land, land, send_sems, recv_sems, 7 * a + k, sib).wait_recv()
                cp = _remote(land, land, send_sems, recv_sems, 7 * a + 3 + k, sib)
                cp.start()
                started.append(cp)
        for a in range(n):
            for k, (px, py) in enumerate(chips):
                land = outs[a].at[2 * px + py, 1 - c]
                _remote(land, land, send_sems, recv_sems, 7 * a + 3 + k, sib).wait_recv()
            own = outs[a].at[me]
            _remote(own, own, send_sems, recv_sems, 7 * a + 6, sib).wait_recv()
        for cp in started:
            cp.wait_send()

    out_shape = [jax.ShapeDtypeStruct((4,) + s.shape, s.dtype) for s in shards]
    return _comm_call(body, name, shards, out_shape, 7 * n)


def _pair_exchange(grads, name):
    n = len(grads)

    def body(*refs):
        ins, outs = refs[:n], refs[n:2 * n]
        send_sems, recv_sems = refs[2 * n:]
        x, y, c, _ = _place()
        sib = (x, y, 1 - c)
        started = []
        for a in range(n):
            for j in range(4):
                cp = _remote(ins[a].at[j, 1 - c], outs[a].at[j], send_sems, recv_sems, 4 * a + j, sib)
                cp.start()
                started.append(cp)
        for cp in started:
            cp.wait_recv()
        for cp in started:
            cp.wait_send()

    out_shape = [jax.ShapeDtypeStruct((4,) + g.shape[2:], g.dtype) for g in grads]
    return _comm_call(body, name, grads, out_shape, 4 * n)


def _gather_sides(shards):
    n = len(shards)
    full_shape = [jax.ShapeDtypeStruct((4,) + s.shape, s.dtype) for s in shards]

    def first(sel):
        def ici(ins, outs, send_sems, recv_sems):
            x, y, c, chips = _place()
            return [_remote(ins[a].at[c], outs[a].at[2 * x + y, c], send_sems, recv_sems, 3 * a + k, (px, py, c))
                    for a in range(len(sel)) for k, (px, py) in enumerate(chips)]

        return Side([shards[a] for a in sel], [full_shape[a] for a in sel], 3 * len(sel), ici)

    def d2d(ins, outs, send_sems, recv_sems):
        x, y, c, chips = _place()
        sib, cps = (x, y, 1 - c), []
        for a in range(n):
            cps.append(_remote(ins[a], outs[a].at[2 * x + y], send_sems, recv_sems, 4 * a + 3, sib))
            for k, (px, py) in enumerate(chips):
                land = outs[a].at[2 * px + py, c]
                cps.append(_remote(land, land, send_sems, recv_sems, 4 * a + k, sib))
        return cps

    def second(partial):
        return Side(list(shards) + list(partial), full_shape, 4 * n, d2d, aliases={n + a: a for a in range(n)})

    return first, second


def _chip_exchange_side(sums):
    n = len(sums)

    def copies(ins, outs, send_sems, recv_sems):
        _, _, c, chips = _place()
        return [_remote(ins[a].at[2 * px + py], outs[a].at[k], send_sems, recv_sems, 3 * a + k, (px, py, c))
                for a in range(n) for k, (px, py) in enumerate(chips)]

    return Side(sums, [jax.ShapeDtypeStruct((3,) + s.shape[1:], s.dtype) for s in sums], 3 * n, copies)


def _halves_exchange(own, name):
    n = len(own)

    def body(*refs):
        ins, outs = refs[:n], refs[n:2 * n]
        send_sems, recv_sems = refs[2 * n:]
        x, y, c, _ = _place()
        started = [_remote(ins[a], outs[a], send_sems, recv_sems, a, (x, y, 1 - c)) for a in range(n)]
        for cp in started:
            cp.start()
        for cp in started:
            cp.wait_recv()
        for cp in started:
            cp.wait_send()

    return _comm_call(body, name, own, [jax.ShapeDtypeStruct(t.shape, t.dtype) for t in own], n)


def _broadcast_small(buf, name):
    def body(in_ref, out_ref, send_sems, recv_sems, loc_sems):
        x, y, c, _ = _place()
        me = 4 * x + 2 * y + c
        loc = pltpu.make_async_copy(in_ref, out_ref.at[me], loc_sems.at[0])
        loc.start()
        peers = [(x ^ (k >> 2 & 1), y ^ (k >> 1 & 1), c ^ (k & 1)) for k in range(1, 8)]
        started = []
        for k, to in enumerate(peers):
            cp = _remote(in_ref, out_ref.at[me], send_sems, recv_sems, k, to)
            cp.start()
            started.append(cp)
        for k, (px, py, pc) in enumerate(peers):
            land = out_ref.at[4 * px + 2 * py + pc]
            _remote(land, land, send_sems, recv_sems, k, (px, py, pc)).wait_recv()
        for cp in started:
            cp.wait_send()
        loc.wait()

    return _comm_call(body, name, [buf], [jax.ShapeDtypeStruct((8,) + buf.shape, buf.dtype)], 7, 1)[0]


def _pack(arrays):
    flat = jnp.concatenate([a.reshape(-1).astype(F32) for a in arrays])
    pad = -flat.shape[0] % (256 * LANES)
    return jnp.pad(flat, (0, pad)).reshape(-1, LANES)


def _unpack(buf, shapes):
    flat, out, at = buf.reshape(-1), [], 0
    for s in shapes:
        size = math.prod(s)
        out.append(flat[at:at + size].reshape(s))
        at += size
    return out


def _adamw_any(w, g, m, v, name):
    shape = w.shape
    two = (-1, shape[-1]) if w.ndim >= 2 else (1, -1)
    outs = _adamw(w.reshape(two), g.reshape(two), m.reshape(two), v.reshape(two), name)
    return [o.reshape(shape) for o in outs]


BIG = ("sc_w_in", "sc_w_out", "sg_w_in", "sg_w_out", "sb_w_qkv", "sb_w_out", "ffn_w_up", "ffn_w_down")
ROW_SHARDED = ("sc_w_out", "sg_w_out", "sb_w_out", "ffn_w_down")
MIXER_WEIGHTS = (("sc_w_in", "sc_w_out"), ("sg_w_in", "sg_w_out"), ("sb_w_qkv", "sb_w_out"))
WEIGHTS = ("norm_mix_pre", "norm_mix_post", "norm_ffn_pre", "norm_ffn_post", "sc_w_in", "sc_conv_w", "sc_w_out",
           "sg_w_in", "sg_ln_g", "sg_ln_b", "sg_w_s", "sg_b_s", "sg_w_out", "sb_w_qkv", "sb_w_out",
           "ffn_w_up", "ffn_conv_w", "ffn_conv_b", "ffn_w_down")


def _step(p):
    t, d = p["x"].shape[1:]
    x, target = p["x"].reshape(t, d), p["loss_target"].reshape(t, d)
    depth = p["norm_mix_pre"].shape[0]
    chip = 2 * lax.axis_index("x") + lax.axis_index("y")

    def layer_entities(i):
        name_in, name_out = MIXER_WEIGHTS[i % N_MIXERS]
        return [(name_in, i // N_MIXERS), (name_out, i // N_MIXERS), ("ffn_w_up", i), ("ffn_w_down", i)]

    def shard(e):
        s = p[e[0]][e[1]].astype(BF16)
        return s.reshape(2, s.shape[0] // 2, s.shape[1])

    def as_weight(e, g):
        rows, cols = p[e[0]].shape[1:]
        return g.reshape(1, 4 * rows, cols) if e[0] in ROW_SHARDED else g.reshape(4, rows, cols)

    ents = [e for i in range(depth) for e in layer_entities(i)]
    full = {e: as_weight(e, g) for e, g in zip(layer_entities(0), _all_gather([shard(e) for e in layer_entities(0)],
                                                                              "gather_weights_0"))}

    conv_shapes = [p["sc_conv_w"].shape, p["ffn_conv_w"].shape]
    conv_all = _broadcast_small(_pack([p["sc_conv_w"], p["ffn_conv_w"]]), "gather_conv_taps")[0::2]
    sc_cw, ffn_cw = [jnp.moveaxis(jnp.stack([_unpack(conv_all[j], conv_shapes)[i] for j in range(4)]), 0, 2)
                     .reshape(s[0], s[1], 4 * s[2]) for i, s in enumerate(conv_shapes)]

    ws = p["sg_w_s"]
    groups, ch = ws.shape[1], ws.shape[2]
    bs_t = [jnp.pad(p["sg_b_s"][j].T, ((0, 0), (0, LANES - groups))) for j in range(ws.shape[0])]
    g1, g2, g3, g4 = [[p[k][i:i + 1] for i in range(depth)] for k in WEIGHTS[:4]]

    saved = []
    h, hn = x, _rms_fwd(x, g1[0], "rms_in")
    for i in range(depth):
        kind, j = i % N_MIXERS, i // N_MIXERS
        nxt = layer_entities(i + 1) if i < depth - 1 else []
        first, second = _gather_sides([shard(e) for e in nxt]) if nxt else (lambda sel: None, lambda partial: None)
        w_in, w_out = [full[name, j] for name in MIXER_WEIGHTS[kind]]
        pre, *part_a = _listed(_mm_nn(hn, w_in, BF16 if kind == 2 else F32, f"{('sc', 'sg', 'sb')[kind]}_in_{i}", first([0, 1])))
        extra = None
        if kind == 0:
            y = _sc_fwd(pre, sc_cw[j], f"sc_mix_{i}")
        elif kind == 1:
            y = _sgu_fwd(pre, p["sg_ln_g"][j:j + 1], p["sg_ln_b"][j:j + 1], ws[j], bs_t[j], f"sg_mix_{i}")
        else:
            extra, y = _attn_fwd(pre, f"sb_mix_{i}")
        m = _mm_nn(y, w_out, F32, f"mix_out_{i}")
        h1, fn = _norm_step(h, m, g2[i], g3[i], f"norm_mid_{i}")
        up, *part_b = _listed(_mm_nn(fn, full["ffn_w_up", i], F32, f"ffn_up_{i}", first([2, 3])))
        act = _ffn_fwd(up, ffn_cw[i], p["ffn_conv_b"][i:i + 1], f"ffn_act_{i}")
        f, *gathered = _listed(_mm_nn(act, full["ffn_w_down", i], F32, f"ffn_down_{i}", second(part_a + part_b)))
        full.update({e: as_weight(e, g) for e, g in zip(nxt, gathered)})
        saved.append(dict(h=h, hn=hn, pre=pre, y=y, extra=extra, m=m, h1=h1, fn=fn, up=up, act=act, f=f,
                          w_in=w_in, w_out=w_out))
        if i < depth - 1:
            h, hn = _norm_step(h1, f, g4[i], g1[i + 1], f"norm_end_{i}")
    loss, dh, df, dg4 = _final_loss(h1, f, g4[depth - 1], target, "loss_head")

    gbig, gsm = {}, {k: [None] * depth for k in ("g1", "g2", "g3", "g4", "ffn_cw", "ffn_cb")}
    gsm["g4"][depth - 1] = dg4
    gsm["sc_cw"] = [None] * p["sc_conv_w"].shape[0]
    for k in ("sg_lg", "sg_lb", "sg_ws", "sg_bs"):
        gsm[k] = [None] * ws.shape[0]
    pending, sums, slots = [], {}, {}

    def pair_sums(entities, tag):
        parts = [gbig[e].reshape(4, 2, -1, gbig[e].shape[-1]) for e in entities]
        pair = _pair_exchange(parts, f"grad_pair_exchange_{tag}")
        sums.update({e: _pair_add(g, q, f"grad_pair_add_{tag}_{a}") for a, (e, g, q) in enumerate(zip(entities, parts, pair))})

    def chip_side(entities):
        return _chip_exchange_side([sums[e] for e in entities]) if entities else None

    for i in reversed(range(depth)):
        s = saved[i]
        kind, j = i % N_MIXERS, i // N_MIXERS
        dact = _mm_nt(df, full["ffn_w_down", i], F32, f"d_ffn_act_{i}")
        gbig["ffn_w_down", i], *landed = _listed(_mm_tn(s["act"], df, 1, f"g_ffn_down_{i}", chip_side(pending)))
        slots.update(zip(pending, landed))
        dup, dcw = _ffn_bwd(s["up"], dact, ffn_cw[i], p["ffn_conv_b"][i:i + 1], f"d_ffn_up_{i}")
        gsm["ffn_cw"][i], gsm["ffn_cb"][i] = dcw[0:3], dcw[3:4]
        dfn = _mm_nt(dup, full["ffn_w_up", i], F32, f"d_ffn_in_{i}")
        gbig["ffn_w_up", i] = _mm_tn(s["fn"], dup, 4, f"g_ffn_up_{i}")
        pair_sums([("ffn_w_up", i), ("ffn_w_down", i)], f"ffn_{i}")
        dh1, gsm["g3"][i], dm, gsm["g2"][i] = _norm_bwd_step(dh, dfn, s["h1"], g3[i], (s["m"], g2[i]), f"d_norm_mid_{i}")
        name_in, name_out = MIXER_WEIGHTS[kind]
        gbig[name_out, j] = _mm_tn(s["y"], dm, 1, f"g_mix_out_{i}")
        if kind == 0:
            dy = _mm_nt(dm, s["w_out"], F32, f"d_mix_y_{i}")
            dpre, dwc = _sc_bwd(s["pre"], dy, sc_cw[j], f"d_sc_mix_{i}")
            gsm["sc_cw"][j] = dwc[0:3]
        elif kind == 1:
            dy = _mm_nt(dm, s["w_out"], F32, f"d_mix_y_{i}")
            dpre, dws, dbs, dlg, dlb = _sgu_bwd(s["pre"], dy, p["sg_ln_g"][j:j + 1], p["sg_ln_b"][j:j + 1], ws[j], bs_t[j],
                                                f"d_sg_mix_{i}")
            gsm["sg_lg"][j], gsm["sg_lb"][j], gsm["sg_ws"][j], gsm["sg_bs"][j] = dlg, dlb, dws, dbs[:, :groups].T
        else:
            do = _mm_nt(dm, s["w_out"], BF16, f"d_mix_y_{i}")
            dpre = jnp.concatenate(_attn_bwd(s["pre"], do, s["extra"], f"d_sb_mix_{i}"), axis=1)
        dhn, *landed = _listed(_mm_nt(dpre, s["w_in"], F32, f"d_mix_in_{i}", chip_side([("ffn_w_up", i)])))
        slots["ffn_w_up", i] = landed[0]
        gbig[name_in, j], *landed = _listed(_mm_tn(s["hn"], dpre, 4, f"g_mix_in_{i}", chip_side([("ffn_w_down", i)])))
        slots["ffn_w_down", i] = landed[0]
        if i > 0:
            dh, gsm["g1"][i], df, gsm["g4"][i - 1] = _norm_bwd_step(dh1, dhn, s["h"], g1[i],
                                                                   (saved[i - 1]["f"], g4[i - 1]), f"d_norm_in_{i}")
        else:
            dx, gsm["g1"][0] = _norm_bwd_step(dh1, dhn, s["h"], g1[0], None, "d_norm_in_0")
        pending = [(name_in, j), (name_out, j)]
        pair_sums(pending, f"mix_{i}")
    slots.update(zip(pending, _side_call(chip_side(pending), "grad_chip_exchange_mix_0")))

    halves = {e: _sum_own_slots(sums[e], slots[e], f"grad_chip_sum_{a}") for a, e in enumerate(ents)}
    own = [jnp.stack([halves[name, l] for l in range(p[name].shape[0])]) for name in BIG]
    recv = _halves_exchange(own, "grad_halves_exchange")
    grads, delta, new_m, new_v = {}, {}, {}, {}
    for name, mine, theirs in zip(BIG, own, recv):
        shape = p[name].shape
        view = (shape[0], 2, shape[1] // 2, shape[2])
        outs = _adamw_halves(p[name].reshape(view), mine, theirs, p["m_" + name].reshape(view), p["v_" + name].reshape(view),
                             f"adamw_{name}")
        grads[name], delta[name], new_m[name], new_v[name] = [o.reshape(shape) for o in outs]

    small = [jnp.concatenate(gsm[k]) for k in ("g1", "g2", "g3", "g4", "sg_lg", "sg_lb")] + [
        jnp.stack(gsm["sg_ws"]), jnp.stack(gsm["sg_bs"]),
        jnp.concatenate(gsm["ffn_cb"]), jnp.stack(gsm["sc_cw"]), jnp.stack(gsm["ffn_cw"])]
    small_shapes = [a.shape for a in small]
    total = _sum_slots(_broadcast_small(_pack(small), "small_grad_exchange"), "small_grad_sum")
    sm = _unpack(total, small_shapes)
    for k, name in enumerate(("norm_mix_pre", "norm_mix_post", "norm_ffn_pre", "norm_ffn_post", "sg_ln_g", "sg_ln_b")):
        grads[name] = sm[k].reshape(p[name].shape)
    grads["sg_w_s"], grads["sg_b_s"] = sm[6].reshape(ws.shape), sm[7].reshape(p["sg_b_s"].shape)
    grads["ffn_conv_b"] = sm[8].reshape(p["ffn_conv_b"].shape)
    for name, full_g in (("sc_conv_w", sm[9]), ("ffn_conv_w", sm[10])):
        n = p[name].shape[-1]
        grads[name] = lax.dynamic_slice_in_dim(full_g, chip * n, n, axis=2)

    for name in WEIGHTS:
        if name in BIG:
            continue
        delta[name], new_m[name], new_v[name] = _adamw_any(p[name], grads[name], p["m_" + name], p["v_" + name], f"adamw_{name}")

    loss = lax.psum(loss[0, 0], ("x", "y", "c"))
    return (loss, dx.reshape(p["x"].shape), *[grads[n] for n in WEIGHTS], *[delta[n] for n in WEIGHTS],
            *[new_m[n] for n in WEIGHTS], *[new_v[n] for n in WEIGHTS])


def kernel(x, norm_mix_pre, norm_mix_post, norm_ffn_pre, norm_ffn_post, sc_w_in, sc_conv_w, sc_w_out, sg_w_in, sg_ln_g, sg_ln_b, sg_w_s, sg_b_s, sg_w_out, sb_w_qkv, sb_w_out, ffn_w_up, ffn_conv_w, ffn_conv_b, ffn_w_down, loss_target, m_norm_mix_pre, m_norm_mix_post, m_norm_ffn_pre, m_norm_ffn_post, m_sc_w_in, m_sc_conv_w, m_sc_w_out, m_sg_w_in, m_sg_ln_g, m_sg_ln_b, m_sg_w_s, m_sg_b_s, m_sg_w_out, m_sb_w_qkv, m_sb_w_out, m_ffn_w_up, m_ffn_conv_w, m_ffn_conv_b, m_ffn_w_down, v_norm_mix_pre, v_norm_mix_post, v_norm_ffn_pre, v_norm_ffn_post, v_sc_w_in, v_sc_conv_w, v_sc_w_out, v_sg_w_in, v_sg_ln_g, v_sg_ln_b, v_sg_w_s, v_sg_b_s, v_sg_w_out, v_sb_w_qkv, v_sb_w_out, v_ffn_w_up, v_ffn_conv_w, v_ffn_conv_b, v_ffn_w_down):
    return _step(dict(locals()))
```

```python
import functools
import math

import jax
import jax.numpy as jnp
from jax import lax
from jax.experimental import pallas as pl
from jax.experimental.pallas import tpu as pltpu

F32 = jnp.float32
BF16 = jnp.bfloat16
MESH = pl.DeviceIdType.MESH
ANY = pl.BlockSpec(memory_space=pl.ANY)

EPS = 1e-6
HEAD_DIM = 128
N_MIXERS = 3
ADAM_LR, ADAM_B1, ADAM_B2, ADAM_EPS, ADAM_WD, ADAM_STEP = 0.001, 0.9, 0.999, 1e-08, 0.01, 10
V7X_VMEM_LIMIT = 56 * 1024 * 1024
HALO = 8
LANES = 128
EXP_UNDERFLOW = -104.0
INV_SQRT2 = 1.0 / math.sqrt(2.0)
INV_SQRT_2PI = 1.0 / math.sqrt(2.0 * math.pi)


def _tile(n, target, mult):
    t = min(n, target) // mult * mult
    while t >= mult:
        if n % t == 0:
            return t
        t -= mult
    return n


def _params(*sem):
    return pltpu.CompilerParams(dimension_semantics=sem, vmem_limit_bytes=V7X_VMEM_LIMIT)


def _dot(a, b, dims):
    return lax.dot_general(a, b, (dims, ((), ())), preferred_element_type=F32)


def _dot_nn(a, b):
    return _dot(a, b, ((1,), (0,)))


def _dot_nt(a, b):
    return _dot(a, b, ((1,), (1,)))


def _dot_tn(a, b):
    return _dot(a, b, ((0,), (0,)))


class Side:
    def __init__(self, ins, out_shape, n, copies, aliases=None):
        self.ins, self.out_shape, self.n, self.copies, self.aliases = list(ins), list(out_shape), n, copies, aliases or {}


def _grid_call(core, name, grid, in_specs, out_spec, out_shape, scratch, operands, side):
    if side is None:
        return pl.pallas_call(
            core, name=name, grid=grid, in_specs=in_specs, out_specs=out_spec, out_shape=out_shape, scratch_shapes=scratch,
            compiler_params=_params("parallel", "parallel", "arbitrary"))(*operands)
    n_in, n_sin, n_sout = len(operands), len(side.ins), len(side.out_shape)

    def body(*refs):
        ins, sins = refs[:n_in], refs[n_in:n_in + n_sin]
        out, souts = refs[n_in + n_sin], refs[n_in + n_sin + 1:n_in + n_sin + 1 + n_sout]
        scr, (send_sems, recv_sems) = refs[n_in + n_sin + 1 + n_sout:-2], refs[-2:]
        ids = [pl.program_id(d) for d in range(len(grid))]
        first = functools.reduce(jnp.logical_and, [i == 0 for i in ids])
        last = functools.reduce(jnp.logical_and, [i == g - 1 for i, g in zip(ids, grid)])

        @pl.when(first)
        def _():
            for cp in side.copies(sins, souts, send_sems, recv_sems):
                cp.start()

        core(*ins, out, *scr)

        @pl.when(last)
        def _():
            cps = side.copies(sins, souts, send_sems, recv_sems)
            for cp in cps:
                cp.wait_recv()
            for cp in cps:
                cp.wait_send()

    return pl.pallas_call(
        body, name=name, grid=grid, in_specs=list(in_specs) + [ANY] * n_sin, out_specs=[out_spec] + [ANY] * n_sout,
        out_shape=[out_shape] + side.out_shape,
        scratch_shapes=list(scratch) + [pltpu.SemaphoreType.DMA((side.n,)), pltpu.SemaphoreType.DMA((side.n,))],
        input_output_aliases={n_in + i: 1 + o for i, o in side.aliases.items()},
        compiler_params=_params("arbitrary", "arbitrary", "arbitrary"))(*operands, *side.ins)


def _side_call(side, name):
    n_sin, n_sout = len(side.ins), len(side.out_shape)

    def body(*refs):
        cps = side.copies(refs[:n_sin], refs[n_sin:n_sin + n_sout], refs[-2], refs[-1])
        for cp in cps:
            cp.start()
        for cp in cps:
            cp.wait_recv()
        for cp in cps:
            cp.wait_send()

    return pl.pallas_call(
        body, name=name, in_specs=[ANY] * n_sin, out_specs=[ANY] * n_sout, out_shape=side.out_shape,
        scratch_shapes=[pltpu.SemaphoreType.DMA((side.n,)), pltpu.SemaphoreType.DMA((side.n,))],
        input_output_aliases=dict(side.aliases))(*side.ins)


def _listed(result):
    return list(result) if isinstance(result, (list, tuple)) else [result]


def _reduce_core(dot, steps):
    if steps == 1:
        def core(a_ref, b_ref, o_ref):
            o_ref[...] = dot(a_ref[...], b_ref[...]).astype(o_ref.dtype)
        return core

    def core(a_ref, b_ref, o_ref, acc):
        r = pl.program_id(2)

        @pl.when(r == 0)
        def _():
            acc[...] = jnp.zeros_like(acc)

        acc[...] += dot(a_ref[...], b_ref[...])

        @pl.when(r == steps - 1)
        def _():
            o_ref[...] = acc[...].astype(o_ref.dtype)
    return core


def _acc(steps, shape):
    return [] if steps == 1 else [pltpu.VMEM(shape, F32)]


def _mm_nn(a, w, out_dtype, name, side=None):
    m, k = a.shape
    s, _, n = w.shape
    tm, tn, tk = _tile(m, 1024, 16), _tile(n, 1536, LANES), _tile(k, 2816, LANES)
    nb, nk = n // tn, k // tk
    return _grid_call(
        _reduce_core(_dot_nn, nk), name, (m // tm, s * nb, nk),
        [pl.BlockSpec((tm, tk), lambda i, j, kk: (i, kk)),
         pl.BlockSpec((None, tk, tn), lambda i, j, kk: (j // nb, kk, j % nb))],
        pl.BlockSpec((tm, tn), lambda i, j, kk: (i, j)), jax.ShapeDtypeStruct((m, s * n), out_dtype),
        _acc(nk, (tm, tn)), (a, w), side)


def _mm_nt(dy, w, out_dtype, name, side=None):
    m = dy.shape[0]
    s, k, n = w.shape
    tm, tn, tko = _tile(m, 1024, 16), _tile(n, 2816, LANES), _tile(k, 1536, LANES)
    nb = n // tn
    nr = s * nb
    return _grid_call(
        _reduce_core(_dot_nt, nr), name, (m // tm, k // tko, nr),
        [pl.BlockSpec((tm, tn), lambda i, j, r: (i, r)),
         pl.BlockSpec((None, tko, tn), lambda i, j, r: (r // nb, j, r % nb))],
        pl.BlockSpec((tm, tko), lambda i, j, r: (i, j)), jax.ShapeDtypeStruct((m, k), out_dtype),
        _acc(nr, (tm, tko)), (dy, w), side)


def _mm_tn(x, dy, s, name, side=None):
    t, k = x.shape
    n = dy.shape[1] // s
    tk, tn, tt = _tile(k, 1024, LANES), _tile(n, 1536, LANES), _tile(t, 2048, 16)
    nb, nt = n // tn, t // tt
    return _grid_call(
        _reduce_core(_dot_tn, nt), name, (k // tk, s * nb, nt),
        [pl.BlockSpec((tt, tk), lambda i, j, q: (q, i)),
         pl.BlockSpec((tt, tn), lambda i, j, q: (q, j))],
        pl.BlockSpec((None, tk, tn), lambda i, j, q: (j // nb, i, j % nb)), jax.ShapeDtypeStruct((s, k, n), BF16),
        _acc(nt, (tk, tn)), (x, dy), side)


def _rms(x, g):
    return x * lax.rsqrt(jnp.mean(x * x, axis=-1, keepdims=True) + EPS) * g


def _rms_bwd(x, g, dy):
    r = lax.rsqrt(jnp.mean(x * x, axis=-1, keepdims=True) + EPS)
    gy = dy * g
    dx = r * gy - x * (r * r * r * jnp.mean(x * gy, axis=-1, keepdims=True))
    return dx, dy * (x * r)


def _row_spec(tt, d):
    return pl.BlockSpec((tt, d), lambda i: (i, 0))


def _vec_spec(d):
    return pl.BlockSpec((1, d), lambda i: (0, 0))


def _rms_fwd(x, g, name):
    t, d = x.shape
    tt = _tile(t, 512, 16)

    def body(x_ref, g_ref, o_ref):
        o_ref[...] = _rms(x_ref[...], g_ref[...]).astype(o_ref.dtype)

    return pl.pallas_call(
        body, name=name, grid=(t // tt,),
        in_specs=[_row_spec(tt, d), _vec_spec(d)], out_specs=_row_spec(tt, d),
        out_shape=jax.ShapeDtypeStruct((t, d), BF16), compiler_params=_params("parallel"),
    )(x, g)


def _norm_step(h, m, g_post, g_next, name):
    t, d = h.shape
    tt = _tile(t, 512, 16)

    def body(h_ref, m_ref, gp_ref, gn_ref, ho_ref, xn_ref):
        hn = h_ref[...] + _rms(m_ref[...], gp_ref[...])
        ho_ref[...] = hn
        xn_ref[...] = _rms(hn, gn_ref[...]).astype(xn_ref.dtype)

    return pl.pallas_call(
        body, name=name, grid=(t // tt,),
        in_specs=[_row_spec(tt, d), _row_spec(tt, d), _vec_spec(d), _vec_spec(d)],
        out_specs=[_row_spec(tt, d), _row_spec(tt, d)],
        out_shape=[jax.ShapeDtypeStruct((t, d), F32), jax.ShapeDtypeStruct((t, d), BF16)],
        compiler_params=_params("parallel"),
    )(h, m, g_post, g_next)


def _final_loss(h, f, g_post, target, name):
    t, d = h.shape
    tt = _tile(t, 256, 16)

    def body(h_ref, f_ref, g_ref, tg_ref, loss_ref, dh_ref, df_ref, dg_ref):
        @pl.when(pl.program_id(0) == 0)
        def _():
            loss_ref[...] = jnp.zeros_like(loss_ref)
            dg_ref[...] = jnp.zeros_like(dg_ref)

        fv, g = f_ref[...], g_ref[...]
        err = h_ref[...] + _rms(fv, g) - tg_ref[...]
        per_row = jnp.mean(err * err, axis=-1, keepdims=True)
        loss_ref[...] += 0.5 * jnp.sum(per_row, axis=0, keepdims=True)
        dh = err * (1.0 / d)
        dh_ref[...] = dh
        df, dg = _rms_bwd(fv, g, dh)
        df_ref[...] = df.astype(df_ref.dtype)
        dg_ref[...] += jnp.sum(dg, axis=0, keepdims=True)

    return pl.pallas_call(
        body, name=name, grid=(t // tt,),
        in_specs=[_row_spec(tt, d), _row_spec(tt, d), _vec_spec(d), _row_spec(tt, d)],
        out_specs=[pl.BlockSpec((1, 1), lambda i: (0, 0)), _row_spec(tt, d), _row_spec(tt, d), _vec_spec(d)],
        out_shape=[jax.ShapeDtypeStruct((1, 1), F32), jax.ShapeDtypeStruct((t, d), F32),
                   jax.ShapeDtypeStruct((t, d), BF16), jax.ShapeDtypeStruct((1, d), F32)],
        compiler_params=_params("arbitrary"),
    )(h, f, g_post, target)


def _norm_bwd_step(dh_out, dxn, x, g_pre, prev, name):
    t, d = x.shape
    tt = _tile(t, 256, 16)
    has_prev = prev is not None

    def body(*refs):
        if has_prev:
            dho_ref, dxn_ref, x_ref, g_ref, xa_ref, ga_ref, dh_ref, dg_ref, da_ref, dga_ref = refs
        else:
            dho_ref, dxn_ref, x_ref, g_ref, dh_ref, dg_ref = refs

        @pl.when(pl.program_id(0) == 0)
        def _():
            dg_ref[...] = jnp.zeros_like(dg_ref)
            if has_prev:
                dga_ref[...] = jnp.zeros_like(dga_ref)

        dx, dg = _rms_bwd(x_ref[...], g_ref[...], dxn_ref[...])
        dh = dho_ref[...] + dx
        dh_ref[...] = dh
        dg_ref[...] += jnp.sum(dg, axis=0, keepdims=True)
        if has_prev:
            da, dga = _rms_bwd(xa_ref[...], ga_ref[...], dh)
            da_ref[...] = da.astype(da_ref.dtype)
            dga_ref[...] += jnp.sum(dga, axis=0, keepdims=True)

    ins = [dh_out, dxn, x, g_pre] + (list(prev) if has_prev else [])
    in_specs = [_row_spec(tt, d)] * 3 + [_vec_spec(d)] + ([_row_spec(tt, d), _vec_spec(d)] if has_prev else [])
    out_specs = [_row_spec(tt, d), _vec_spec(d)] + ([_row_spec(tt, d), _vec_spec(d)] if has_prev else [])
    out_shape = [jax.ShapeDtypeStruct((t, d), F32), jax.ShapeDtypeStruct((1, d), F32)]
    if has_prev:
        out_shape += [jax.ShapeDtypeStruct((t, d), BF16), jax.ShapeDtypeStruct((1, d), F32)]
    return pl.pallas_call(
        body, name=name, grid=(t // tt,), in_specs=in_specs, out_specs=out_specs, out_shape=out_shape,
        compiler_params=_params("arbitrary"),
    )(*ins)


def _shift_down(x, k, halo):
    r = pltpu.roll(x, k, 0)
    rh = pltpu.roll(halo, k, 0)
    row = lax.broadcasted_iota(jnp.int32, rh.shape, 0)
    return jnp.concatenate([jnp.where(row < k, rh, r[:HALO]), r[HALO:]], axis=0)


def _shift_up(x_ext, k, n):
    return pltpu.roll(x_ext, x_ext.shape[0] - k, 0)[:n]


def _halo_specs(tt, width, nblk):
    per = tt // HALO
    prev = pl.BlockSpec((HALO, width), lambda i: (jnp.maximum(i * per - 1, 0), 0))
    nxt = pl.BlockSpec((HALO, width), lambda i: (jnp.minimum((i + 1) * per, nblk * per - 1), 0))
    return prev, nxt


def _conv3(x, halo, w0, w1, w2):
    return w0 * _shift_down(x, 2, halo) + w1 * _shift_down(x, 1, halo) + w2 * x


def _sc_fwd(p, wc, name):
    t, c3 = p.shape
    c = c3 // 3
    tt, cc = _tile(t, 256, 16), _tile(c, 128, LANES)
    nblk = t // tt
    prev_spec, _ = _halo_specs(tt, c3, nblk)

    def body(p_ref, pp_ref, w_ref, y_ref):
        first = pl.program_id(0) == 0
        for j in range(c // cc):
            cols = slice(j * cc, (j + 1) * cc)
            gc, hv = p_ref[:, c + j * cc:c + (j + 1) * cc], p_ref[:, 2 * c + j * cc:2 * c + (j + 1) * cc]
            uh = jnp.where(first, 0.0, pp_ref[:, c + j * cc:c + (j + 1) * cc] * pp_ref[:, 2 * c + j * cc:2 * c + (j + 1) * cc])
            cv = _conv3(gc * hv, uh, w_ref[0:1, cols], w_ref[1:2, cols], w_ref[2:3, cols])
            y_ref[:, cols] = (p_ref[:, cols] * cv).astype(y_ref.dtype)

    return pl.pallas_call(
        body, name=name, grid=(nblk,),
        in_specs=[_row_spec(tt, c3), prev_spec, pl.BlockSpec((3, c), lambda i: (0, 0))],
        out_specs=_row_spec(tt, c), out_shape=jax.ShapeDtypeStruct((t, c), BF16),
        compiler_params=_params("parallel"),
    )(p, p, wc)


def _sc_bwd(p, dy, wc, name):
    t, c3 = p.shape
    c = c3 // 3
    tt, cc = _tile(t, 256, 16), _tile(c, 128, LANES)
    nblk = t // tt
    p_prev, p_next = _halo_specs(tt, c3, nblk)
    _, dy_next = _halo_specs(tt, c, nblk)

    def body(p_ref, pp_ref, pn_ref, dy_ref, dyn_ref, w_ref, dp_ref, dw_ref):
        i = pl.program_id(0)
        first, last = i == 0, i == nblk - 1

        @pl.when(first)
        def _():
            dw_ref[...] = jnp.zeros_like(dw_ref)

        for j in range(c // cc):
            a, b, d = slice(j * cc, (j + 1) * cc), slice(c + j * cc, c + (j + 1) * cc), slice(2 * c + j * cc, 2 * c + (j + 1) * cc)
            w0, w1, w2 = w_ref[0:1, a], w_ref[1:2, a], w_ref[2:3, a]
            gb, gc, hv, dyv = p_ref[:, a], p_ref[:, b], p_ref[:, d], dy_ref[:, a]
            u = gc * hv
            uh = jnp.where(first, 0.0, pp_ref[:, b] * pp_ref[:, d])
            u2, u1 = _shift_down(u, 2, uh), _shift_down(u, 1, uh)
            cv = w0 * u2 + w1 * u1 + w2 * u
            dcv = dyv * gb
            dcv_ext = jnp.concatenate([dcv, jnp.where(last, 0.0, dyn_ref[:, a] * pn_ref[:, a])], axis=0)
            du = w2 * dcv + w1 * _shift_up(dcv_ext, 1, tt) + w0 * _shift_up(dcv_ext, 2, tt)
            dp_ref[:, a] = (dyv * cv).astype(dp_ref.dtype)
            dp_ref[:, b] = (du * hv).astype(dp_ref.dtype)
            dp_ref[:, d] = (du * gc).astype(dp_ref.dtype)
            dw_ref[0:1, a] += jnp.sum(dcv * u2, axis=0, keepdims=True)
            dw_ref[1:2, a] += jnp.sum(dcv * u1, axis=0, keepdims=True)
            dw_ref[2:3, a] += jnp.sum(dcv * u, axis=0, keepdims=True)

    return pl.pallas_call(
        body, name=name, grid=(nblk,),
        in_specs=[_row_spec(tt, c3), p_prev, p_next, _row_spec(tt, c), dy_next, pl.BlockSpec((3, c), lambda i: (0, 0))],
        out_specs=[_row_spec(tt, c3), pl.BlockSpec((HALO, c), lambda i: (0, 0))],
        out_shape=[jax.ShapeDtypeStruct((t, c3), BF16), jax.ShapeDtypeStruct((HALO, c), F32)],
        compiler_params=_params("arbitrary"),
    )(p, p, p, dy, dy, wc)


def _ffn_fwd(u, cw, cb, name):
    t, f2 = u.shape
    f = f2 // 2
    tt, cc = _tile(t, 128, 16), _tile(f, 128, LANES)
    nblk = t // tt
    prev_spec, _ = _halo_specs(tt, f2, nblk)

    def body(u_ref, up_ref, w_ref, b_ref, a_ref):
        first = pl.program_id(0) == 0

        def conv(cols):
            halo = jnp.where(first, 0.0, up_ref[:, cols])
            return _conv3(u_ref[:, cols], halo, w_ref[0:1, cols], w_ref[1:2, cols], w_ref[2:3, cols]) + b_ref[:, cols]

        for j in range(f // cc):
            hg, hv = conv(slice(j * cc, (j + 1) * cc)), conv(slice(f + j * cc, f + (j + 1) * cc))
            a_ref[:, j * cc:(j + 1) * cc] = (hg * jax.nn.sigmoid(hg) * hv).astype(a_ref.dtype)

    return pl.pallas_call(
        body, name=name, grid=(nblk,),
        in_specs=[_row_spec(tt, f2), prev_spec, pl.BlockSpec((3, f2), lambda i: (0, 0)), _vec_spec(f2)],
        out_specs=_row_spec(tt, f), out_shape=jax.ShapeDtypeStruct((t, f), BF16),
        compiler_params=_params("parallel"),
    )(u, u, cw, cb)


def _ffn_bwd(u, da, cw, cb, name):
    t, f2 = u.shape
    f = f2 // 2
    tt, cc = _tile(t, 128, 16), _tile(f, 128, LANES)
    nblk = t // tt
    u_prev, u_next = _halo_specs(tt, f2, nblk)
    _, da_next = _halo_specs(tt, f, nblk)

    def body(u_ref, up_ref, un_ref, da_ref, dan_ref, w_ref, b_ref, du_ref, dw_ref):
        i = pl.program_id(0)
        first, last = i == 0, i == nblk - 1

        @pl.when(first)
        def _():
            dw_ref[...] = jnp.zeros_like(dw_ref)

        keep = jnp.where(last, 0.0, 1.0)

        def ext(cols):
            x = jnp.concatenate([u_ref[:, cols], un_ref[:, cols]], axis=0)
            halo = jnp.where(first, 0.0, up_ref[:, cols])
            x2, x1 = _shift_down(x, 2, halo), _shift_down(x, 1, halo)
            h = w_ref[0:1, cols] * x2 + w_ref[1:2, cols] * x1 + w_ref[2:3, cols] * x + b_ref[:, cols]
            return x, x1, x2, h

        def back(cols, dh_ext, x, x1, x2):
            w0, w1, w2 = w_ref[0:1, cols], w_ref[1:2, cols], w_ref[2:3, cols]
            dh = dh_ext[:tt]
            du_ref[:, cols] = (w2 * dh + w1 * _shift_up(dh_ext, 1, tt) + w0 * _shift_up(dh_ext, 2, tt)).astype(du_ref.dtype)
            dw_ref[0:1, cols] += jnp.sum(dh * x2[:tt], axis=0, keepdims=True)
            dw_ref[1:2, cols] += jnp.sum(dh * x1[:tt], axis=0, keepdims=True)
            dw_ref[2:3, cols] += jnp.sum(dh * x[:tt], axis=0, keepdims=True)
            dw_ref[3:4, cols] += jnp.sum(dh, axis=0, keepdims=True)

        row = lax.broadcasted_iota(jnp.int32, (tt + HALO, 1), 0)
        for j in range(f // cc):
            gcols, vcols = slice(j * cc, (j + 1) * cc), slice(f + j * cc, f + (j + 1) * cc)
            xg, xg1, xg2, hg = ext(gcols)
            xv, xv1, xv2, hv = ext(vcols)
            da_ext = jnp.concatenate([da_ref[:, gcols], dan_ref[:, gcols]], axis=0)
            da_ext = jnp.where(row < tt, da_ext, da_ext * keep)
            sg = jax.nn.sigmoid(hg)
            back(gcols, da_ext * hv * (sg * (1.0 + hg * (1.0 - sg))), xg, xg1, xg2)
            back(vcols, da_ext * (hg * sg), xv, xv1, xv2)

    return pl.pallas_call(
        body, name=name, grid=(nblk,),
        in_specs=[_row_spec(tt, f2), u_prev, u_next, _row_spec(tt, f), da_next,
                  pl.BlockSpec((3, f2), lambda i: (0, 0)), _vec_spec(f2)],
        out_specs=[_row_spec(tt, f2), pl.BlockSpec((HALO, f2), lambda i: (0, 0))],
        out_shape=[jax.ShapeDtypeStruct((t, f2), BF16), jax.ShapeDtypeStruct((HALO, f2), F32)],
        compiler_params=_params("arbitrary"),
    )(u, u, u, da, da, cw, cb)


def _gelu(x):
    cdf = 0.5 * (1.0 + lax.erf(x * INV_SQRT2))
    return x * cdf, cdf + x * (jnp.exp(-0.5 * x * x) * INV_SQRT_2PI)


def _sgu_common(p_ref, lg_ref, lb_ref, ws_ref, c, ch, groups):
    u, du_dp = _gelu(p_ref[:, :c])
    v, dv_dp = _gelu(p_ref[:, c:])
    mu = jnp.mean(v, axis=-1, keepdims=True)
    vc = v - mu
    rstd = lax.rsqrt(jnp.mean(vc * vc, axis=-1, keepdims=True) + EPS)
    xhat = vc * rstd
    vn = (xhat * lg_ref[...] + lb_ref[...]).astype(BF16)
    tril = lax.broadcasted_iota(jnp.int32, (ch, ch), 0) >= lax.broadcasted_iota(jnp.int32, (ch, ch), 1)
    wm = [jnp.where(tril, ws_ref[g], 0.0).astype(BF16) for g in range(groups)]
    return u, du_dp, dv_dp, xhat, rstd, vn, wm, tril


def _lane_pick(x, g):
    lane = lax.broadcasted_iota(jnp.int32, x.shape, 1)
    return jnp.sum(jnp.where(lane == g, x, 0.0), axis=1, keepdims=True)


def _sgu_specs(tt, c, ch, groups):
    return [_row_spec(tt, 2 * c), _vec_spec(c), _vec_spec(c),
            pl.BlockSpec((groups, ch, ch), lambda i: (0, 0, 0)), pl.BlockSpec((ch, LANES), lambda i: (0, 0))]


def _sgu_fwd(p, lg, lb, ws, bs_t, name):
    t, c2 = p.shape
    c = c2 // 2
    groups, ch, _ = ws.shape
    gc = c // groups
    tt = _tile(t, 2 * ch, ch)

    def body(p_ref, lg_ref, lb_ref, ws_ref, bs_ref, y_ref):
        u, _, _, _, _, vn, wm, _ = _sgu_common(p_ref, lg_ref, lb_ref, ws_ref, c, ch, groups)
        bs = bs_ref[...]
        for r in range(tt // ch):
            rows = slice(r * ch, (r + 1) * ch)
            for g in range(groups):
                cols = slice(g * gc, (g + 1) * gc)
                mixed = _dot_nn(wm[g], vn[rows, cols]) + _lane_pick(bs, g)
                y_ref[rows, cols] = (u[rows, cols] * mixed).astype(y_ref.dtype)

    return pl.pallas_call(
        body, name=name, grid=(t // tt,), in_specs=_sgu_specs(tt, c, ch, groups),
        out_specs=_row_spec(tt, c), out_shape=jax.ShapeDtypeStruct((t, c), BF16),
        compiler_params=_params("parallel"),
    )(p, lg, lb, ws, bs_t)


def _sgu_bwd(p, dy, lg, lb, ws, bs_t, name):
    t, c2 = p.shape
    c = c2 // 2
    groups, ch, _ = ws.shape
    gc = c // groups
    tt = _tile(t, 2 * ch, ch)

    def body(p_ref, dy_ref, lg_ref, lb_ref, ws_ref, bs_ref, dp_ref, dws_ref, dbs_ref, dlg_ref, dlb_ref, dvn_ref):
        @pl.when(pl.program_id(0) == 0)
        def _():
            dws_ref[...] = jnp.zeros_like(dws_ref)
            dbs_ref[...] = jnp.zeros_like(dbs_ref)
            dlg_ref[...] = jnp.zeros_like(dlg_ref)
            dlb_ref[...] = jnp.zeros_like(dlb_ref)

        u, du_dp, dv_dp, xhat, rstd, vn, wm, tril = _sgu_common(p_ref, lg_ref, lb_ref, ws_ref, c, ch, groups)
        bs = bs_ref[...]
        lane = lax.broadcasted_iota(jnp.int32, (ch, LANES), 1)
        for r in range(tt // ch):
            rows = slice(r * ch, (r + 1) * ch)
            for g in range(groups):
                cols = slice(g * gc, (g + 1) * gc)
                dyv, vng = dy_ref[rows, cols], vn[rows, cols]
                mixed = _dot_nn(wm[g], vng) + _lane_pick(bs, g)
                dp_ref[rows, cols] = (dyv * mixed * du_dp[rows, cols]).astype(dp_ref.dtype)
                dmixed = dyv * u[rows, cols]
                dmb = dmixed.astype(BF16)
                dws_ref[g] += jnp.where(tril, _dot_nt(dmb, vng), 0.0)
                dbs_ref[...] += jnp.where(lane == g, jnp.sum(dmixed, axis=1, keepdims=True), 0.0)
                dvn_ref[rows, cols] = _dot_tn(wm[g], dmb)
        dvn = dvn_ref[...]
        dlg_ref[...] += jnp.sum(dvn * xhat, axis=0, keepdims=True)
        dlb_ref[...] += jnp.sum(dvn, axis=0, keepdims=True)
        dxh = dvn * lg_ref[...]
        dv = rstd * (dxh - jnp.mean(dxh, axis=-1, keepdims=True) - xhat * jnp.mean(dxh * xhat, axis=-1, keepdims=True))
        dp_ref[:, c:] = (dv * dv_dp).astype(dp_ref.dtype)

    specs = _sgu_specs(tt, c, ch, groups)
    return pl.pallas_call(
        body, name=name, grid=(t // tt,),
        in_specs=[specs[0], _row_spec(tt, c)] + specs[1:],
        out_specs=[_row_spec(tt, c2), pl.BlockSpec((groups, ch, ch), lambda i: (0, 0, 0)),
                   pl.BlockSpec((ch, LANES), lambda i: (0, 0)), _vec_spec(c), _vec_spec(c)],
        out_shape=[jax.ShapeDtypeStruct((t, c2), BF16), jax.ShapeDtypeStruct((groups, ch, ch), F32),
                   jax.ShapeDtypeStruct((ch, LANES), F32), jax.ShapeDtypeStruct((1, c), F32), jax.ShapeDtypeStruct((1, c), F32)],
        scratch_shapes=[pltpu.VMEM((tt, c), F32)],
        compiler_params=_params("arbitrary"),
    )(p, dy, lg, lb, ws, bs_t)


def _split(x):
    hi = x.astype(BF16)
    return hi, (x - hi.astype(F32)).astype(BF16)


def _sb_block(q, ks, qpos, kb, tk, scale, r_carry, tri):
    z = _dot_nt(q, ks) * scale
    kpos = kb * tk + lax.broadcasted_iota(jnp.int32, (1, tk), 1)
    mask = kpos < qpos
    e = jnp.exp(-jnp.abs(z))
    lm = jnp.where(mask, -(jnp.maximum(z, 0.0) + jnp.log(1.0 + e)), 0.0)
    hi, lo = _split(lm)
    inc = _dot_nn(hi, tri) + _dot_nn(lo, tri)
    att = jnp.where(mask, jnp.exp(z + lm + (inc - lm + r_carry)), 0.0)
    return z, mask, e, inc, att


def _key_blocks(i, step, carry):
    def guarded(_, c):
        return lax.cond(c[-1] > EXP_UNDERFLOW, step, lambda same: same, c)

    carry = lax.fori_loop(0, jnp.minimum(i + 1, 2), lambda _, c: step(c), carry)
    return lax.cond(carry[-1] > EXP_UNDERFLOW, lambda c: lax.fori_loop(2, i + 1, guarded, c), lambda same: same, carry)


def _suffix_ones(tk):
    return (lax.broadcasted_iota(jnp.int32, (tk, tk), 0) >= lax.broadcasted_iota(jnp.int32, (tk, tk), 1)).astype(BF16)


def _attn_fwd(qkv, name):
    t, d3 = qkv.shape
    d = d3 // 3
    heads = d // HEAD_DIM
    tq = _tile(t, 256, LANES)
    nq = t // tq
    scale = HEAD_DIM ** -0.5

    def body(q_ref, k_ref, v_ref, of_ref, ob_ref):
        i = pl.program_id(1)
        q = q_ref[...]
        qpos = i * tq + lax.broadcasted_iota(jnp.int32, (tq, 1), 0)
        tri = _suffix_ones(tq)

        def step(carry):
            n, o, r_carry, _ = carry
            kb = i - n
            rows = pl.ds(pl.multiple_of(kb * tq, tq), tq)
            _, _, _, inc, att = _sb_block(q, k_ref[rows, :], qpos, kb, tq, scale, r_carry, tri)
            ahi, alo = _split(att)
            vs = v_ref[rows, :]
            r_new = r_carry + inc[:, 0:1]
            return n + 1, o + _dot_nn(ahi, vs) + _dot_nn(alo, vs), r_new, jnp.max(r_new)

        init = (jnp.int32(0), jnp.zeros((tq, HEAD_DIM), F32), jnp.zeros((tq, 1), F32), jnp.float32(0.0))
        _, o, _, _ = _key_blocks(i, step, init)
        of_ref[...] = o
        ob_ref[...] = o.astype(ob_ref.dtype)

    return pl.pallas_call(
        body, name=name, grid=(heads, nq),
        in_specs=[pl.BlockSpec((tq, HEAD_DIM), lambda h, i: (i, h)),
                  pl.BlockSpec((t, HEAD_DIM), lambda h, i: (0, heads + h)),
                  pl.BlockSpec((t, HEAD_DIM), lambda h, i: (0, 2 * heads + h))],
        out_specs=[pl.BlockSpec((tq, HEAD_DIM), lambda h, i: (i, h))] * 2,
        out_shape=[jax.ShapeDtypeStruct((t, d), F32), jax.ShapeDtypeStruct((t, d), BF16)],
        compiler_params=_params("parallel", "parallel"),
    )(qkv, qkv, qkv)


def _attn_bwd(qkv, do, of, name):
    t, d3 = qkv.shape
    d = d3 // 3
    heads = d // HEAD_DIM
    tq = _tile(t, 256, LANES)
    nq = t // tq
    scale = HEAD_DIM ** -0.5

    def body(q_ref, k_ref, v_ref, do_ref, of_ref, dq_ref, dk_ref, dv_ref, dk_acc, dv_acc):
        i = pl.program_id(1)

        @pl.when(i == 0)
        def _():
            dk_acc[...] = jnp.zeros_like(dk_acc)
            dv_acc[...] = jnp.zeros_like(dv_acc)

        q, dov = q_ref[...], do_ref[...]
        delta = jnp.sum(dov.astype(F32) * of_ref[...], axis=-1, keepdims=True)
        qpos = i * tq + lax.broadcasted_iota(jnp.int32, (tq, 1), 0)
        tri = _suffix_ones(tq)

        def step(carry):
            n, dq, r_carry, g_carry, _ = carry
            kb = i - n
            rows = pl.ds(pl.multiple_of(kb * tq, tq), tq)
            ks, vs = k_ref[rows, :], v_ref[rows, :]
            z, mask, e, inc, att = _sb_block(q, ks, qpos, kb, tq, scale, r_carry, tri)
            g = _dot_nt(dov, vs) * att
            ghi, glo = _split(g)
            ginc = _dot_nn(ghi, tri) + _dot_nn(glo, tri)
            beta = jnp.where(z >= 0.0, 1.0, e) / (1.0 + e)
            dz = jnp.where(mask, g * (1.0 - beta) - (delta - g_carry - ginc) * beta, 0.0) * scale
            dzb = dz.astype(BF16)
            dk_acc[rows, :] += _dot_tn(dzb, q)
            dv_acc[rows, :] += _dot_tn(att.astype(BF16), dov)
            r_new = r_carry + inc[:, 0:1]
            return n + 1, dq + _dot_nn(dzb, ks), r_new, g_carry + ginc[:, 0:1], jnp.max(r_new)

        zero = jnp.zeros((tq, 1), F32)
        init = (jnp.int32(0), jnp.zeros((tq, HEAD_DIM), F32), zero, zero, jnp.float32(0.0))
        _, dq, _, _, _ = _key_blocks(i, step, init)
        dq_ref[...] = dq.astype(dq_ref.dtype)

        @pl.when(i == nq - 1)
        def _():
            dk_ref[...] = dk_acc[...].astype(dk_ref.dtype)
            dv_ref[...] = dv_acc[...].astype(dv_ref.dtype)

    blk = pl.BlockSpec((tq, HEAD_DIM), lambda h, i: (i, h))
    full = pl.BlockSpec((t, HEAD_DIM), lambda h, i: (0, h))
    return pl.pallas_call(
        body, name=name, grid=(heads, nq),
        in_specs=[blk, pl.BlockSpec((t, HEAD_DIM), lambda h, i: (0, heads + h)),
                  pl.BlockSpec((t, HEAD_DIM), lambda h, i: (0, 2 * heads + h)), blk, blk],
        out_specs=[blk, full, full],
        out_shape=[jax.ShapeDtypeStruct((t, d), BF16)] * 3,
        scratch_shapes=[pltpu.VMEM((t, HEAD_DIM), F32), pltpu.VMEM((t, HEAD_DIM), F32)],
        compiler_params=_params("parallel", "arbitrary"),
    )(qkv, qkv, qkv, do, of)


def _pair_add(g, pair, name):
    _, _, rh, c = g.shape
    tr = _tile(rh, 256, 16)
    core = lax.axis_index("c").astype(jnp.int32).reshape(1)

    def body(c_ref, g_ref, p_ref, o_ref):
        o_ref[...] = (g_ref[...].astype(F32) + p_ref[...].astype(F32)).astype(o_ref.dtype)

    return pl.pallas_call(
        body, name=name,
        grid_spec=pltpu.PrefetchScalarGridSpec(
            num_scalar_prefetch=1, grid=(4, rh // tr),
            in_specs=[pl.BlockSpec((None, None, tr, c), lambda j, r, cr: (j, cr[0], r, 0)),
                      pl.BlockSpec((None, tr, c), lambda j, r, cr: (j, r, 0))],
            out_specs=pl.BlockSpec((None, tr, c), lambda j, r, cr: (j, r, 0))),
        out_shape=jax.ShapeDtypeStruct((4, rh, c), BF16),
        compiler_params=_params("parallel", "parallel"),
    )(core, g, pair)


def _sum_slots(x, name):
    n, r, c = x.shape
    tr = _tile(r, 256, 16)

    def body(x_ref, o_ref):
        acc = x_ref[0].astype(F32)
        for k in range(1, n):
            acc = acc + x_ref[k].astype(F32)
        o_ref[...] = acc

    return pl.pallas_call(
        body, name=name, grid=(r // tr,),
        in_specs=[pl.BlockSpec((n, tr, c), lambda i: (0, i, 0))],
        out_specs=pl.BlockSpec((tr, c), lambda i: (i, 0)),
        out_shape=jax.ShapeDtypeStruct((r, c), F32), compiler_params=_params("parallel"),
    )(x)


def _sum_own_slots(sums, slots, name):
    _, rh, c = sums.shape
    tr = _tile(rh, 256, 16)
    chip = (2 * lax.axis_index("x") + lax.axis_index("y")).astype(jnp.int32).reshape(1)

    def body(chip_ref, own_ref, slot_ref, o_ref):
        acc = own_ref[...].astype(F32)
        for k in range(3):
            acc = acc + slot_ref[k].astype(F32)
        o_ref[...] = acc

    return pl.pallas_call(
        body, name=name,
        grid_spec=pltpu.PrefetchScalarGridSpec(
            num_scalar_prefetch=1, grid=(rh // tr,),
            in_specs=[pl.BlockSpec((None, tr, c), lambda r, me: (me[0], r, 0)),
                      pl.BlockSpec((3, tr, c), lambda r, me: (0, r, 0))],
            out_specs=pl.BlockSpec((tr, c), lambda r, me: (r, 0))),
        out_shape=jax.ShapeDtypeStruct((rh, c), F32), compiler_params=_params("parallel"),
    )(chip, sums, slots)


def _adam_update(w, g, m, v):
    mn = ADAM_B1 * m + (1.0 - ADAM_B1) * g
    vn = ADAM_B2 * v + (1.0 - ADAM_B2) * (g * g)
    c1, c2 = 1.0 - ADAM_B1 ** ADAM_STEP, 1.0 - ADAM_B2 ** ADAM_STEP
    return -ADAM_LR * ((mn / c1) / (jnp.sqrt(vn / c2) + ADAM_EPS) + ADAM_WD * w), mn, vn


def _adamw_halves(w, own, recv, m, v, name):
    layers, _, rh, c = w.shape
    tr = _tile(rh, max(8, (1 << 18) // c), 8)
    core = lax.axis_index("c").astype(jnp.int32).reshape(1)

    def body(c_ref, w_ref, own_ref, recv_ref, m_ref, v_ref, g_ref, d_ref, mo_ref, vo_ref):
        g = jnp.where(pl.program_id(1) == c_ref[0], own_ref[...], recv_ref[...])
        g_ref[...] = g
        d_ref[...], mo_ref[...], vo_ref[...] = _adam_update(w_ref[...], g, m_ref[...], v_ref[...])

    full = pl.BlockSpec((None, None, tr, c), lambda l, hf, r, cr: (l, hf, r, 0))
    half = pl.BlockSpec((None, tr, c), lambda l, hf, r, cr: (l, r, 0))
    return pl.pallas_call(
        body, name=name,
        grid_spec=pltpu.PrefetchScalarGridSpec(
            num_scalar_prefetch=1, grid=(layers, 2, rh // tr),
            in_specs=[full, half, half, full, full], out_specs=[full] * 4),
        out_shape=[jax.ShapeDtypeStruct(w.shape, F32)] * 4,
        compiler_params=_params("parallel", "parallel", "parallel"),
    )(core, w, own, recv, m, v)


def _adamw(w, g, m, v, name):
    r, c = w.shape
    tr = _tile(r, max(8, (1 << 18) // c), 8)

    def body(w_ref, g_ref, m_ref, v_ref, d_ref, mo_ref, vo_ref):
        d_ref[...], mo_ref[...], vo_ref[...] = _adam_update(w_ref[...], g_ref[...], m_ref[...], v_ref[...])

    spec = pl.BlockSpec((tr, c), lambda i: (i, 0))
    return pl.pallas_call(
        body, name=name, grid=(r // tr,), in_specs=[spec] * 4, out_specs=[spec] * 3,
        out_shape=[jax.ShapeDtypeStruct((r, c), F32)] * 3, compiler_params=_params("parallel"),
    )(w, g, m, v)


def _place():
    x, y, c = lax.axis_index("x"), lax.axis_index("y"), lax.axis_index("c")
    chips = [(1 - x, y), (x, 1 - y), (1 - x, 1 - y)]
    return x, y, c, chips


def _remote(src, dst, send_sems, recv_sems, k, to):
    return pltpu.make_async_remote_copy(src_ref=src, dst_ref=dst, send_sem=send_sems.at[k], recv_sem=recv_sems.at[k],
                                        device_id=to, device_id_type=MESH)


def _comm_call(body, name, ins, out_shape, n_remote, n_local=0):
    sems = [pltpu.SemaphoreType.DMA((n_remote,)), pltpu.SemaphoreType.DMA((n_remote,))]
    if n_local:
        sems.append(pltpu.SemaphoreType.DMA((n_local,)))
    return pl.pallas_call(
        body, name=name, in_specs=[ANY] * len(ins), out_specs=[ANY] * len(out_shape), out_shape=out_shape,
        scratch_shapes=sems,
    )(*ins)


def _all_gather(shards, name):
    n = len(shards)

    def body(*refs):
        ins, outs = refs[:n], refs[n:2 * n]
        send_sems, recv_sems = refs[2 * n:]
        x, y, c, chips = _place()
        me, sib = 2 * x + y, (x, y, 1 - c)
        started = []
        for a in range(n):
            cp = _remote(ins[a], outs[a].at[me], send_sems, recv_sems, 7 * a + 6, sib)
            cp.start()
            started.append(cp)
            for k, (px, py) in enumerate(chips):
                cp = _remote(ins[a].at[c], outs[a].at[me, c], send_sems, recv_sems, 7 * a + k, (px, py, c))
                cp.start()
                started.append(cp)
        for a in range(n):
            for k, (px, py) in enumerate(chips):
                land = outs[a].at[2 * px + py, c]
                _remote(land, land, send_sems, recv_sems, 7 * a + k, sib).wait_recv()
                cp = _remote(land, land, send_sems, recv_sems, 7 * a + 3 + k, sib)
                cp.start()
                started.append(cp)
        for a in range(n):
            for k, (px, py) in enumerate(chips):
                land = outs[a].at[2 * px + py, 1 - c]
                _remote(land, land, send_sems, recv_sems, 7 * a + 3 + k, sib).wait_recv()
            own = outs[a].at[me]
            _remote(own, own, send_sems, recv_sems, 7 * a + 6, sib).wait_recv()
        for cp in started:
            cp.wait_send()

    out_shape = [jax.ShapeDtypeStruct((4,) + s.shape, s.dtype) for s in shards]
    return _comm_call(body, name, shards, out_shape, 7 * n)


def _pair_exchange(grads, name):
    n = len(grads)

    def body(*refs):
        ins, outs = refs[:n], refs[n:2 * n]
        send_sems, recv_sems = refs[2 * n:]
        x, y, c, _ = _place()
        sib = (x, y, 1 - c)
        started = []
        for a in range(n):
            for j in range(4):
                cp = _remote(ins[a].at[j, 1 - c], outs[a].at[j], send_sems, recv_sems, 4 * a + j, sib)
                cp.start()
                started.append(cp)
        for cp in started:
            cp.wait_recv()
        for cp in started:
            cp.wait_send()

    out_shape = [jax.ShapeDtypeStruct((4,) + g.shape[2:], g.dtype) for g in grads]
    return _comm_call(body, name, grads, out_shape, 4 * n)


def _gather_sides(shards):
    n = len(shards)
    full_shape = [jax.ShapeDtypeStruct((4,) + s.shape, s.dtype) for s in shards]

    def first(sel):
        def ici(ins, outs, send_sems, recv_sems):
            x, y, c, chips = _place()
            return [_remote(ins[a].at[c], outs[a].at[2 * x + y, c], send_sems, recv_sems, 3 * a + k, (px, py, c))
                    for a in range(len(sel)) for k, (px, py) in enumerate(chips)]

        return Side([shards[a] for a in sel], [full_shape[a] for a in sel], 3 * len(sel), ici)

    def d2d(ins, outs, send_sems, recv_sems):
        x, y, c, chips = _place()
        sib, cps = (x, y, 1 - c), []
        for a in range(n):
            cps.append(_remote(ins[a], outs[a].at[2 * x + y], send_sems, recv_sems, 4 * a + 3, sib))
            for k, (px, py) in enumerate(chips):
                land = outs[a].at[2 * px + py, c]
                cps.append(_remote(land, land, send_sems, recv_sems, 4 * a + k, sib))
        return cps

    def second(partial):
        return Side(list(shards) + list(partial), full_shape, 4 * n, d2d, aliases={n + a: a for a in range(n)})

    return first, second


def _chip_exchange_side(sums):
    n = len(sums)

    def copies(ins, outs, send_sems, recv_sems):
        _, _, c, chips = _place()
        return [_remote(ins[a].at[2 * px + py], outs[a].at[k], send_sems, recv_sems, 3 * a + k, (px, py, c))
                for a in range(n) for k, (px, py) in enumerate(chips)]

    return Side(sums, [jax.ShapeDtypeStruct((3,) + s.shape[1:], s.dtype) for s in sums], 3 * n, copies)


def _halves_exchange(own, name):
    n = len(own)

    def body(*refs):
        ins, outs = refs[:n], refs[n:2 * n]
        send_sems, recv_sems = refs[2 * n:]
        x, y, c, _ = _place()
        started = [_remote(ins[a], outs[a], send_sems, recv_sems, a, (x, y, 1 - c)) for a in range(n)]
        for cp in started:
            cp.start()
        for cp in started:
            cp.wait_recv()
        for cp in started:
            cp.wait_send()

    return _comm_call(body, name, own, [jax.ShapeDtypeStruct(t.shape, t.dtype) for t in own], n)


def _broadcast_small(buf, name):
    def body(in_ref, out_ref, send_sems, recv_sems, loc_sems):
        x, y, c, _ = _place()
        me = 4 * x + 2 * y + c
        loc = pltpu.make_async_copy(in_ref, out_ref.at[me], loc_sems.at[0])
        loc.start()
        peers = [(x ^ (k >> 2 & 1), y ^ (k >> 1 & 1), c ^ (k & 1)) for k in range(1, 8)]
        started = []
        for k, to in enumerate(peers):
            cp = _remote(in_ref, out_ref.at[me], send_sems, recv_sems, k, to)
            cp.start()
            started.append(cp)
        for k, (px, py, pc) in enumerate(peers):
            land = out_ref.at[4 * px + 2 * py + pc]
            _remote(land, land, send_sems, recv_sems, k, (px, py, pc)).wait_recv()
        for cp in started:
            cp.wait_send()
        loc.wait()

    return _comm_call(body, name, [buf], [jax.ShapeDtypeStruct((8,) + buf.shape, buf.dtype)], 7, 1)[0]


def _pack(arrays):
    flat = jnp.concatenate([a.reshape(-1).astype(F32) for a in arrays])
    pad = -flat.shape[0] % (256 * LANES)
    return jnp.pad(flat, (0, pad)).reshape(-1, LANES)


def _unpack(buf, shapes):
    flat, out, at = buf.reshape(-1), [], 0
    for s in shapes:
        size = math.prod(s)
        out.append(flat[at:at + size].reshape(s))
        at += size
    return out


def _adamw_any(w, g, m, v, name):
    shape = w.shape
    two = (-1, shape[-1]) if w.ndim >= 2 else (1, -1)
    outs = _adamw(w.reshape(two), g.reshape(two), m.reshape(two), v.reshape(two), name)
    return [o.reshape(shape) for o in outs]


BIG = ("sc_w_in", "sc_w_out", "sg_w_in", "sg_w_out", "sb_w_qkv", "sb_w_out", "ffn_w_up", "ffn_w_down")
ROW_SHARDED = ("sc_w_out", "sg_w_out", "sb_w_out", "ffn_w_down")
MIXER_WEIGHTS = (("sc_w_in", "sc_w_out"), ("sg_w_in", "sg_w_out"), ("sb_w_qkv", "sb_w_out"))
WEIGHTS = ("norm_mix_pre", "norm_mix_post", "norm_ffn_pre", "norm_ffn_post", "sc_w_in", "sc_conv_w", "sc_w_out",
           "sg_w_in", "sg_ln_g", "sg_ln_b", "sg_w_s", "sg_b_s", "sg_w_out", "sb_w_qkv", "sb_w_out",
           "ffn_w_up", "ffn_conv_w", "ffn_conv_b", "ffn_w_down")


def _step(p):
    t, d = p["x"].shape[1:]
    x, target = p["x"].reshape(t, d), p["loss_target"].reshape(t, d)
    depth = p["norm_mix_pre"].shape[0]
    chip = 2 * lax.axis_index("x") + lax.axis_index("y")

    def layer_entities(i):
        name_in, name_out = MIXER_WEIGHTS[i % N_MIXERS]
        return [(name_in, i // N_MIXERS), (name_out, i // N_MIXERS), ("ffn_w_up", i), ("ffn_w_down", i)]

    def shard(e):
        s = p[e[0]][e[1]].astype(BF16)
        return s.reshape(2, s.shape[0] // 2, s.shape[1])

    def as_weight(e, g):
        rows, cols = p[e[0]].shape[1:]
        return g.reshape(1, 4 * rows, cols) if e[0] in ROW_SHARDED else g.reshape(4, rows, cols)

    ents = [e for i in range(depth) for e in layer_entities(i)]
    full = {e: as_weight(e, g) for e, g in zip(layer_entities(0), _all_gather([shard(e) for e in layer_entities(0)],
                                                                              "gather_weights_0"))}

    conv_shapes = [p["sc_conv_w"].shape, p["ffn_conv_w"].shape]
    conv_all = _broadcast_small(_pack([p["sc_conv_w"], p["ffn_conv_w"]]), "gather_conv_taps")[0::2]
    sc_cw, ffn_cw = [jnp.moveaxis(jnp.stack([_unpack(conv_all[j], conv_shapes)[i] for j in range(4)]), 0, 2)
                     .reshape(s[0], s[1], 4 * s[2]) for i, s in enumerate(conv_shapes)]

    ws = p["sg_w_s"]
    groups, ch = ws.shape[1], ws.shape[2]
    bs_t = [jnp.pad(p["sg_b_s"][j].T, ((0, 0), (0, LANES - groups))) for j in range(ws.shape[0])]
    g1, g2, g3, g4 = [[p[k][i:i + 1] for i in range(depth)] for k in WEIGHTS[:4]]

    saved = []
    h, hn = x, _rms_fwd(x, g1[0], "rms_in")
    for i in range(depth):
        kind, j = i % N_MIXERS, i // N_MIXERS
        nxt = layer_entities(i + 1) if i < depth - 1 else []
        first, second = _gather_sides([shard(e) for e in nxt]) if nxt else (lambda sel: None, lambda partial: None)
        w_in, w_out = [full[name, j] for name in MIXER_WEIGHTS[kind]]
        pre, *part_a = _listed(_mm_nn(hn, w_in, BF16 if kind == 2 else F32, f"{('sc', 'sg', 'sb')[kind]}_in_{i}", first([0, 1])))
        extra = None
        if kind == 0:
            y = _sc_fwd(pre, sc_cw[j], f"sc_mix_{i}")
        elif kind == 1:
            y = _sgu_fwd(pre, p["sg_ln_g"][j:j + 1], p["sg_ln_b"][j:j + 1], ws[j], bs_t[j], f"sg_mix_{i}")
        else:
            extra, y = _attn_fwd(pre, f"sb_mix_{i}")
        m = _mm_nn(y, w_out, F32, f"mix_out_{i}")
        h1, fn = _norm_step(h, m, g2[i], g3[i], f"norm_mid_{i}")
        up, *part_b = _listed(_mm_nn(fn, full["ffn_w_up", i], F32, f"ffn_up_{i}", first([2, 3])))
        act = _ffn_fwd(up, ffn_cw[i], p["ffn_conv_b"][i:i + 1], f"ffn_act_{i}")
        f, *gathered = _listed(_mm_nn(act, full["ffn_w_down", i], F32, f"ffn_down_{i}", second(part_a + part_b)))
        full.update({e: as_weight(e, g) for e, g in zip(nxt, gathered)})
        saved.append(dict(h=h, hn=hn, pre=pre, y=y, extra=extra, m=m, h1=h1, fn=fn, up=up, act=act, f=f,
                          w_in=w_in, w_out=w_out))
        if i < depth - 1:
            h, hn = _norm_step(h1, f, g4[i], g1[i + 1], f"norm_end_{i}")
    loss, dh, df, dg4 = _final_loss(h1, f, g4[depth - 1], target, "loss_head")

    gbig, gsm = {}, {k: [None] * depth for k in ("g1", "g2", "g3", "g4", "ffn_cw", "ffn_cb")}
    gsm["g4"][depth - 1] = dg4
    gsm["sc_cw"] = [None] * p["sc_conv_w"].shape[0]
    for k in ("sg_lg", "sg_lb", "sg_ws", "sg_bs"):
        gsm[k] = [None] * ws.shape[0]
    pending, sums, slots = [], {}, {}

    def pair_sums(entities, tag):
        parts = [gbig[e].reshape(4, 2, -1, gbig[e].shape[-1]) for e in entities]
        pair = _pair_exchange(parts, f"grad_pair_exchange_{tag}")
        sums.update({e: _pair_add(g, q, f"grad_pair_add_{tag}_{a}") for a, (e, g, q) in enumerate(zip(entities, parts, pair))})

    def chip_side(entities):
        return _chip_exchange_side([sums[e] for e in entities]) if entities else None

    for i in reversed(range(depth)):
        s = saved[i]
        kind, j = i % N_MIXERS, i // N_MIXERS
        dact = _mm_nt(df, full["ffn_w_down", i], F32, f"d_ffn_act_{i}")
        gbig["ffn_w_down", i], *landed = _listed(_mm_tn(s["act"], df, 1, f"g_ffn_down_{i}", chip_side(pending)))
        slots.update(zip(pending, landed))
        dup, dcw = _ffn_bwd(s["up"], dact, ffn_cw[i], p["ffn_conv_b"][i:i + 1], f"d_ffn_up_{i}")
        gsm["ffn_cw"][i], gsm["ffn_cb"][i] = dcw[0:3], dcw[3:4]
        dfn = _mm_nt(dup, full["ffn_w_up", i], F32, f"d_ffn_in_{i}")
        gbig["ffn_w_up", i] = _mm_tn(s["fn"], dup, 4, f"g_ffn_up_{i}")
        pair_sums([("ffn_w_up", i), ("ffn_w_down", i)], f"ffn_{i}")
        dh1, gsm["g3"][i], dm, gsm["g2"][i] = _norm_bwd_step(dh, dfn, s["h1"], g3[i], (s["m"], g2[i]), f"d_norm_mid_{i}")
        name_in, name_out = MIXER_WEIGHTS[kind]
        gbig[name_out, j] = _mm_tn(s["y"], dm, 1, f"g_mix_out_{i}")
        if kind == 0:
            dy = _mm_nt(dm, s["w_out"], F32, f"d_mix_y_{i}")
            dpre, dwc = _sc_bwd(s["pre"], dy, sc_cw[j], f"d_sc_mix_{i}")
            gsm["sc_cw"][j] = dwc[0:3]
        elif kind == 1:
            dy = _mm_nt(dm, s["w_out"], F32, f"d_mix_y_{i}")
            dpre, dws, dbs, dlg, dlb = _sgu_bwd(s["pre"], dy, p["sg_ln_g"][j:j + 1], p["sg_ln_b"][j:j + 1], ws[j], bs_t[j],
                                                f"d_sg_mix_{i}")
            gsm["sg_lg"][j], gsm["sg_lb"][j], gsm["sg_ws"][j], gsm["sg_bs"][j] = dlg, dlb, dws, dbs[:, :groups].T
        else:
            do = _mm_nt(dm, s["w_out"], BF16, f"d_mix_y_{i}")
            dpre = jnp.concatenate(_attn_bwd(s["pre"], do, s["extra"], f"d_sb_mix_{i}"), axis=1)
        dhn, *landed = _listed(_mm_nt(dpre, s["w_in"], F32, f"d_mix_in_{i}", chip_side([("ffn_w_up", i)])))
        slots["ffn_w_up", i] = landed[0]
        gbig[name_in, j], *landed = _listed(_mm_tn(s["hn"], dpre, 4, f"g_mix_in_{i}", chip_side([("ffn_w_down", i)])))
        slots["ffn_w_down", i] = landed[0]
        if i > 0:
            dh, gsm["g1"][i], df, gsm["g4"][i - 1] = _norm_bwd_step(dh1, dhn, s["h"], g1[i],
                                                                   (saved[i - 1]["f"], g4[i - 1]), f"d_norm_in_{i}")
        else:
            dx, gsm["g1"][0] = _norm_bwd_step(dh1, dhn, s["h"], g1[0], None, "d_norm_in_0")
        pending = [(name_in, j), (name_out, j)]
        pair_sums(pending, f"mix_{i}")
    slots.update(zip(pending, _side_call(chip_side(pending), "grad_chip_exchange_mix_0")))

    halves = {e: _sum_own_slots(sums[e], slots[e], f"grad_chip_sum_{a}") for a, e in enumerate(ents)}
    own = [jnp.stack([halves[name, l] for l in range(p[name].shape[0])]) for name in BIG]
    recv = _halves_exchange(own, "grad_halves_exchange")
    grads, delta, new_m, new_v = {}, {}, {}, {}
    for name, mine, theirs in zip(BIG, own, recv):
        shape = p[name].shape
        view = (shape[0], 2, shape[1] // 2, shape[2])
        outs = _adamw_halves(p[name].reshape(view), mine, theirs, p["m_" + name].reshape(view), p["v_" + name].reshape(view),
                             f"adamw_{name}")
        grads[name], delta[name], new_m[name], new_v[name] = [o.reshape(shape) for o in outs]

    small = [jnp.concatenate(gsm[k]) for k in ("g1", "g2", "g3", "g4", "sg_lg", "sg_lb")] + [
        jnp.stack(gsm["sg_ws"]), jnp.stack(gsm["sg_bs"]),
        jnp.concatenate(gsm["ffn_cb"]), jnp.stack(gsm["sc_cw"]), jnp.stack(gsm["ffn_cw"])]
    small_shapes = [a.shape for a in small]
    total = _sum_slots(_broadcast_small(_pack(small), "small_grad_exchange"), "small_grad_sum")
    sm = _unpack(total, small_shapes)
    for k, name in enumerate(("norm_mix_pre", "norm_mix_post", "norm_ffn_pre", "norm_ffn_post", "sg_ln_g", "sg_ln_b")):
        grads[name] = sm[k].reshape(p[name].shape)
    grads["sg_w_s"], grads["sg_b_s"] = sm[6].reshape(ws.shape), sm[7].reshape(p["sg_b_s"].shape)
    grads["ffn_conv_b"] = sm[8].reshape(p["ffn_conv_b"].shape)
    for name, full_g in (("sc_conv_w", sm[9]), ("ffn_conv_w", sm[10])):
        n = p[name].shape[-1]
        grads[name] = lax.dynamic_slice_in_dim(full_g, chip * n, n, axis=2)

    for name in WEIGHTS:
        if name in BIG:
            continue
        delta[name], new_m[name], new_v[name] = _adamw_any(p[name], grads[name], p["m_" + name], p["v_" + name], f"adamw_{name}")

    loss = lax.psum(loss[0, 0], ("x", "y", "c"))
    return (loss, dx.reshape(p["x"].shape), *[grads[n] for n in WEIGHTS], *[delta[n] for n in WEIGHTS],
            *[new_m[n] for n in WEIGHTS], *[new_v[n] for n in WEIGHTS])


def kernel(x, norm_mix_pre, norm_mix_post, norm_ffn_pre, norm_ffn_post, sc_w_in, sc_conv_w, sc_w_out, sg_w_in, sg_ln_g, sg_ln_b, sg_w_s, sg_b_s, sg_w_out, sb_w_qkv, sb_w_out, ffn_w_up, ffn_conv_w, ffn_conv_b, ffn_w_down, loss_target, m_norm_mix_pre, m_norm_mix_post, m_norm_ffn_pre, m_norm_ffn_post, m_sc_w_in, m_sc_conv_w, m_sc_w_out, m_sg_w_in, m_sg_ln_g, m_sg_ln_b, m_sg_w_s, m_sg_b_s, m_sg_w_out, m_sb_w_qkv, m_sb_w_out, m_ffn_w_up, m_ffn_conv_w, m_ffn_conv_b, m_ffn_w_down, v_norm_mix_pre, v_norm_mix_post, v_norm_ffn_pre, v_norm_ffn_post, v_sc_w_in, v_sc_conv_w, v_sc_w_out, v_sg_w_in, v_sg_ln_g, v_sg_ln_b, v_sg_w_s, v_sg_b_s, v_sg_w_out, v_sb_w_qkv, v_sb_w_out, v_ffn_w_up, v_ffn_conv_w, v_ffn_conv_b, v_ffn_w_down):
    return _step(dict(locals()))
```

```python
import functools
import math

import jax
import jax.numpy as jnp
from jax import lax
from jax.experimental import pallas as pl
from jax.experimental.pallas import tpu as pltpu

F32 = jnp.float32
BF16 = jnp.bfloat16
MESH = pl.DeviceIdType.MESH
ANY = pl.BlockSpec(memory_space=pl.ANY)

EPS = 1e-6
HEAD_DIM = 128
N_MIXERS = 3
ADAM_LR, ADAM_B1, ADAM_B2, ADAM_EPS, ADAM_WD, ADAM_STEP = 0.001, 0.9, 0.999, 1e-08, 0.01, 10
V7X_VMEM_LIMIT = 56 * 1024 * 1024
HALO = 8
LANES = 128
EXP_UNDERFLOW = -104.0
INV_SQRT2 = 1.0 / math.sqrt(2.0)
INV_SQRT_2PI = 1.0 / math.sqrt(2.0 * math.pi)


def _tile(n, target, mult):
    t = min(n, target) // mult * mult
    while t >= mult:
        if n % t == 0:
            return t
        t -= mult
    return n


def _params(*sem):
    return pltpu.CompilerParams(dimension_semantics=sem, vmem_limit_bytes=V7X_VMEM_LIMIT)


def _dot(a, b, dims):
    return lax.dot_general(a, b, (dims, ((), ())), preferred_element_type=F32)


def _dot_nn(a, b):
    return _dot(a, b, ((1,), (0,)))


def _dot_nt(a, b):
    return _dot(a, b, ((1,), (1,)))


def _dot_tn(a, b):
    return _dot(a, b, ((0,), (0,)))


class Side:
    def __init__(self, ins, out_shape, n, copies, aliases=None):
        self.ins, self.out_shape, self.n, self.copies, self.aliases = list(ins), list(out_shape), n, copies, aliases or {}


def _grid_call(core, name, grid, in_specs, out_spec, out_shape, scratch, operands, side):
    if side is None:
        return pl.pallas_call(
            core, name=name, grid=grid, in_specs=in_specs, out_specs=out_spec, out_shape=out_shape, scratch_shapes=scratch,
            compiler_params=_params("parallel", "parallel", "arbitrary"))(*operands)
    n_in, n_sin, n_sout = len(operands), len(side.ins), len(side.out_shape)

    def body(*refs):
        ins, sins = refs[:n_in], refs[n_in:n_in + n_sin]
        out, souts = refs[n_in + n_sin], refs[n_in + n_sin + 1:n_in + n_sin + 1 + n_sout]
        scr, (send_sems, recv_sems) = refs[n_in + n_sin + 1 + n_sout:-2], refs[-2:]
        ids = [pl.program_id(d) for d in range(len(grid))]
        first = functools.reduce(jnp.logical_and, [i == 0 for i in ids])
        last = functools.reduce(jnp.logical_and, [i == g - 1 for i, g in zip(ids, grid)])

        @pl.when(first)
        def _():
            for cp in side.copies(sins, souts, send_sems, recv_sems):
                cp.start()

        core(*ins, out, *scr)

        @pl.when(last)
        def _():
            cps = side.copies(sins, souts, send_sems, recv_sems)
            for cp in cps:
                cp.wait_recv()
            for cp in cps:
                cp.wait_send()

    return pl.pallas_call(
        body, name=name, grid=grid, in_specs=list(in_specs) + [ANY] * n_sin, out_specs=[out_spec] + [ANY] * n_sout,
        out_shape=[out_shape] + side.out_shape,
        scratch_shapes=list(scratch) + [pltpu.SemaphoreType.DMA((side.n,)), pltpu.SemaphoreType.DMA((side.n,))],
        input_output_aliases={n_in + i: 1 + o for i, o in side.aliases.items()},
        compiler_params=_params("arbitrary", "arbitrary", "arbitrary"))(*operands, *side.ins)


def _side_call(side, name):
    n_sin, n_sout = len(side.ins), len(side.out_shape)

    def body(*refs):
        cps = side.copies(refs[:n_sin], refs[n_sin:n_sin + n_sout], refs[-2], refs[-1])
        for cp in cps:
            cp.start()
        for cp in cps:
            cp.wait_recv()
        for cp in cps:
            cp.wait_send()

    return pl.pallas_call(
        body, name=name, in_specs=[ANY] * n_sin, out_specs=[ANY] * n_sout, out_shape=side.out_shape,
        scratch_shapes=[pltpu.SemaphoreType.DMA((side.n,)), pltpu.SemaphoreType.DMA((side.n,))],
        input_output_aliases=dict(side.aliases))(*side.ins)


def _listed(result):
    return list(result) if isinstance(result, (list, tuple)) else [result]


def _reduce_core(dot, steps):
    if steps == 1:
        def core(a_ref, b_ref, o_ref):
            o_ref[...] = dot(a_ref[...], b_ref[...]).astype(o_ref.dtype)
        return core

    def core(a_ref, b_ref, o_ref, acc):
        r = pl.program_id(2)

        @pl.when(r == 0)
        def _():
            acc[...] = jnp.zeros_like(acc)

        acc[...] += dot(a_ref[...], b_ref[...])

        @pl.when(r == steps - 1)
        def _():
            o_ref[...] = acc[...].astype(o_ref.dtype)
    return core


def _acc(steps, shape):
    return [] if steps == 1 else [pltpu.VMEM(shape, F32)]


def _mm_nn(a, w, out_dtype, name, side=None):
    m, k = a.shape
    s, _, n = w.shape
    tm, tn, tk = _tile(m, 1024, 16), _tile(n, 1536, LANES), _tile(k, 2816, LANES)
    nb, nk = n // tn, k // tk
    return _grid_call(
        _reduce_core(_dot_nn, nk), name, (m // tm, s * nb, nk),
        [pl.BlockSpec((tm, tk), lambda i, j, kk: (i, kk)),
         pl.BlockSpec((None, tk, tn), lambda i, j, kk: (j // nb, kk, j % nb))],
        pl.BlockSpec((tm, tn), lambda i, j, kk: (i, j)), jax.ShapeDtypeStruct((m, s * n), out_dtype),
        _acc(nk, (tm, tn)), (a, w), side)


def _mm_nt(dy, w, out_dtype, name, side=None):
    m = dy.shape[0]
    s, k, n = w.shape
    tm, tn, tko = _tile(m, 1024, 16), _tile(n, 2816, LANES), _tile(k, 1536, LANES)
    nb = n // tn
    nr = s * nb
    return _grid_call(
        _reduce_core(_dot_nt, nr), name, (m // tm, k // tko, nr),
        [pl.BlockSpec((tm, tn), lambda i, j, r: (i, r)),
         pl.BlockSpec((None, tko, tn), lambda i, j, r: (r // nb, j, r % nb))],
        pl.BlockSpec((tm, tko), lambda i, j, r: (i, j)), jax.ShapeDtypeStruct((m, k), out_dtype),
        _acc(nr, (tm, tko)), (dy, w), side)


def _mm_tn(x, dy, s, name, side=None):
    t, k = x.shape
    n = dy.shape[1] // s
    tk, tn, tt = _tile(k, 1408, LANES), _tile(n, 1536, LANES), _tile(t, 2048, 16)
    nb, nt = n // tn, t // tt
    return _grid_call(
        _reduce_core(_dot_tn, nt), name, (k // tk, s * nb, nt),
        [pl.BlockSpec((tt, tk), lambda i, j, q: (q, i)),
         pl.BlockSpec((tt, tn), lambda i, j, q: (q, j))],
        pl.BlockSpec((None, tk, tn), lambda i, j, q: (j // nb, i, j % nb)), jax.ShapeDtypeStruct((s, k, n), BF16),
        _acc(nt, (tk, tn)), (x, dy), side)


def _rms(x, g):
    return x * lax.rsqrt(jnp.mean(x * x, axis=-1, keepdims=True) + EPS) * g


def _rms_bwd(x, g, dy):
    r = lax.rsqrt(jnp.mean(x * x, axis=-1, keepdims=True) + EPS)
    gy = dy * g
    dx = r * gy - x * (r * r * r * jnp.mean(x * gy, axis=-1, keepdims=True))
    return dx, dy * (x * r)


def _row_spec(tt, d):
    return pl.BlockSpec((tt, d), lambda i: (i, 0))


def _vec_spec(d):
    return pl.BlockSpec((1, d), lambda i: (0, 0))


def _rms_fwd(x, g, name):
    t, d = x.shape
    tt = _tile(t, 512, 16)

    def body(x_ref, g_ref, o_ref):
        o_ref[...] = _rms(x_ref[...], g_ref[...]).astype(o_ref.dtype)

    return pl.pallas_call(
        body, name=name, grid=(t // tt,),
        in_specs=[_row_spec(tt, d), _vec_spec(d)], out_specs=_row_spec(tt, d),
        out_shape=jax.ShapeDtypeStruct((t, d), BF16), compiler_params=_params("parallel"),
    )(x, g)


def _norm_step(h, m, g_post, g_next, name):
    t, d = h.shape
    tt = _tile(t, 512, 16)

    def body(h_ref, m_ref, gp_ref, gn_ref, ho_ref, xn_ref):
        hn = h_ref[...] + _rms(m_ref[...], gp_ref[...])
        ho_ref[...] = hn
        xn_ref[...] = _rms(hn, gn_ref[...]).astype(xn_ref.dtype)

    return pl.pallas_call(
        body, name=name, grid=(t // tt,),
        in_specs=[_row_spec(tt, d), _row_spec(tt, d), _vec_spec(d), _vec_spec(d)],
        out_specs=[_row_spec(tt, d), _row_spec(tt, d)],
        out_shape=[jax.ShapeDtypeStruct((t, d), F32), jax.ShapeDtypeStruct((t, d), BF16)],
        compiler_params=_params("parallel"),
    )(h, m, g_post, g_next)


def _final_loss(h, f, g_post, target, name):
    t, d = h.shape
    tt = _tile(t, 256, 16)

    def body(h_ref, f_ref, g_ref, tg_ref, loss_ref, dh_ref, df_ref, dg_ref):
        @pl.when(pl.program_id(0) == 0)
        def _():
            loss_ref[...] = jnp.zeros_like(loss_ref)
            dg_ref[...] = jnp.zeros_like(dg_ref)

        fv, g = f_ref[...], g_ref[...]
        err = h_ref[...] + _rms(fv, g) - tg_ref[...]
        per_row = jnp.mean(err * err, axis=-1, keepdims=True)
        loss_ref[...] += 0.5 * jnp.sum(per_row, axis=0, keepdims=True)
        dh = err * (1.0 / d)
        dh_ref[...] = dh
        df, dg = _rms_bwd(fv, g, dh)
        df_ref[...] = df.astype(df_ref.dtype)
        dg_ref[...] += jnp.sum(dg, axis=0, keepdims=True)

    return pl.pallas_call(
        body, name=name, grid=(t // tt,),
        in_specs=[_row_spec(tt, d), _row_spec(tt, d), _vec_spec(d), _row_spec(tt, d)],
        out_specs=[pl.BlockSpec((1, 1), lambda i: (0, 0)), _row_spec(tt, d), _row_spec(tt, d), _vec_spec(d)],
        out_shape=[jax.ShapeDtypeStruct((1, 1), F32), jax.ShapeDtypeStruct((t, d), F32),
                   jax.ShapeDtypeStruct((t, d), BF16), jax.ShapeDtypeStruct((1, d), F32)],
        compiler_params=_params("arbitrary"),
    )(h, f, g_post, target)


def _norm_bwd_step(dh_out, dxn, x, g_pre, prev, name):
    t, d = x.shape
    tt = _tile(t, 256, 16)
    has_prev = prev is not None

    def body(*refs):
        if has_prev:
            dho_ref, dxn_ref, x_ref, g_ref, xa_ref, ga_ref, dh_ref, dg_ref, da_ref, dga_ref = refs
        else:
            dho_ref, dxn_ref, x_ref, g_ref, dh_ref, dg_ref = refs

        @pl.when(pl.program_id(0) == 0)
        def _():
            dg_ref[...] = jnp.zeros_like(dg_ref)
            if has_prev:
                dga_ref[...] = jnp.zeros_like(dga_ref)

        dx, dg = _rms_bwd(x_ref[...], g_ref[...], dxn_ref[...])
        dh = dho_ref[...] + dx
        dh_ref[...] = dh
        dg_ref[...] += jnp.sum(dg, axis=0, keepdims=True)
        if has_prev:
            da, dga = _rms_bwd(xa_ref[...], ga_ref[...], dh)
            da_ref[...] = da.astype(da_ref.dtype)
            dga_ref[...] += jnp.sum(dga, axis=0, keepdims=True)

    ins = [dh_out, dxn, x, g_pre] + (list(prev) if has_prev else [])
    in_specs = [_row_spec(tt, d)] * 3 + [_vec_spec(d)] + ([_row_spec(tt, d), _vec_spec(d)] if has_prev else [])
    out_specs = [_row_spec(tt, d), _vec_spec(d)] + ([_row_spec(tt, d), _vec_spec(d)] if has_prev else [])
    out_shape = [jax.ShapeDtypeStruct((t, d), F32), jax.ShapeDtypeStruct((1, d), F32)]
    if has_prev:
        out_shape += [jax.ShapeDtypeStruct((t, d), BF16), jax.ShapeDtypeStruct((1, d), F32)]
    return pl.pallas_call(
        body, name=name, grid=(t // tt,), in_specs=in_specs, out_specs=out_specs, out_shape=out_shape,
        compiler_params=_params("arbitrary"),
    )(*ins)


def _shift_down(x, k, halo):
    r = pltpu.roll(x, k, 0)
    rh = pltpu.roll(halo, k, 0)
    row = lax.broadcasted_iota(jnp.int32, rh.shape, 0)
    return jnp.concatenate([jnp.where(row < k, rh, r[:HALO]), r[HALO:]], axis=0)


def _shift_up(x_ext, k, n):
    return pltpu.roll(x_ext, x_ext.shape[0] - k, 0)[:n]


def _halo_specs(tt, width, nblk):
    per = tt // HALO
    prev = pl.BlockSpec((HALO, width), lambda i: (jnp.maximum(i * per - 1, 0), 0))
    nxt = pl.BlockSpec((HALO, width), lambda i: (jnp.minimum((i + 1) * per, nblk * per - 1), 0))
    return prev, nxt


def _conv3(x, halo, w0, w1, w2):
    return w0 * _shift_down(x, 2, halo) + w1 * _shift_down(x, 1, halo) + w2 * x


def _sc_fwd(p, wc, name):
    t, c3 = p.shape
    c = c3 // 3
    tt, cc = _tile(t, 256, 16), _tile(c, 128, LANES)
    nblk = t // tt
    prev_spec, _ = _halo_specs(tt, c3, nblk)

    def body(p_ref, pp_ref, w_ref, y_ref):
        first = pl.program_id(0) == 0
        for j in range(c // cc):
            cols = slice(j * cc, (j + 1) * cc)
            gc, hv = p_ref[:, c + j * cc:c + (j + 1) * cc], p_ref[:, 2 * c + j * cc:2 * c + (j + 1) * cc]
            uh = jnp.where(first, 0.0, pp_ref[:, c + j * cc:c + (j + 1) * cc] * pp_ref[:, 2 * c + j * cc:2 * c + (j + 1) * cc])
            cv = _conv3(gc * hv, uh, w_ref[0:1, cols], w_ref[1:2, cols], w_ref[2:3, cols])
            y_ref[:, cols] = (p_ref[:, cols] * cv).astype(y_ref.dtype)

    return pl.pallas_call(
        body, name=name, grid=(nblk,),
        in_specs=[_row_spec(tt, c3), prev_spec, pl.BlockSpec((3, c), lambda i: (0, 0))],
        out_specs=_row_spec(tt, c), out_shape=jax.ShapeDtypeStruct((t, c), BF16),
        compiler_params=_params("parallel"),
    )(p, p, wc)


def _sc_bwd(p, dy, wc, name):
    t, c3 = p.shape
    c = c3 // 3
    tt, cc = _tile(t, 256, 16), _tile(c, 128, LANES)
    nblk = t // tt
    p_prev, p_next = _halo_specs(tt, c3, nblk)
    _, dy_next = _halo_specs(tt, c, nblk)

    def body(p_ref, pp_ref, pn_ref, dy_ref, dyn_ref, w_ref, dp_ref, dw_ref):
        i = pl.program_id(0)
        first, last = i == 0, i == nblk - 1

        @pl.when(first)
        def _():
            dw_ref[...] = jnp.zeros_like(dw_ref)

        for j in range(c // cc):
            a, b, d = slice(j * cc, (j + 1) * cc), slice(c + j * cc, c + (j + 1) * cc), slice(2 * c + j * cc, 2 * c + (j + 1) * cc)
            w0, w1, w2 = w_ref[0:1, a], w_ref[1:2, a], w_ref[2:3, a]
            gb, gc, hv, dyv = p_ref[:, a], p_ref[:, b], p_ref[:, d], dy_ref[:, a]
            u = gc * hv
            uh = jnp.where(first, 0.0, pp_ref[:, b] * pp_ref[:, d])
            u2, u1 = _shift_down(u, 2, uh), _shift_down(u, 1, uh)
            cv = w0 * u2 + w1 * u1 + w2 * u
            dcv = dyv * gb
            dcv_ext = jnp.concatenate([dcv, jnp.where(last, 0.0, dyn_ref[:, a] * pn_ref[:, a])], axis=0)
            du = w2 * dcv + w1 * _shift_up(dcv_ext, 1, tt) + w0 * _shift_up(dcv_ext, 2, tt)
            dp_ref[:, a] = (dyv * cv).astype(dp_ref.dtype)
            dp_ref[:, b] = (du * hv).astype(dp_ref.dtype)
            dp_ref[:, d] = (du * gc).astype(dp_ref.dtype)
            dw_ref[0:1, a] += jnp.sum(dcv * u2, axis=0, keepdims=True)
            dw_ref[1:2, a] += jnp.sum(dcv * u1, axis=0, keepdims=True)
            dw_ref[2:3, a] += jnp.sum(dcv * u, axis=0, keepdims=True)

    return pl.pallas_call(
        body, name=name, grid=(nblk,),
        in_specs=[_row_spec(tt, c3), p_prev, p_next, _row_spec(tt, c), dy_next, pl.BlockSpec((3, c), lambda i: (0, 0))],
        out_specs=[_row_spec(tt, c3), pl.BlockSpec((HALO, c), lambda i: (0, 0))],
        out_shape=[jax.ShapeDtypeStruct((t, c3), BF16), jax.ShapeDtypeStruct((HALO, c), F32)],
        compiler_params=_params("arbitrary"),
    )(p, p, p, dy, dy, wc)


def _ffn_fwd(u, cw, cb, name):
    t, f2 = u.shape
    f = f2 // 2
    tt, cc = _tile(t, 128, 16), _tile(f, 128, LANES)
    nblk = t // tt
    prev_spec, _ = _halo_specs(tt, f2, nblk)

    def body(u_ref, up_ref, w_ref, b_ref, a_ref):
        first = pl.program_id(0) == 0

        def conv(cols):
            halo = jnp.where(first, 0.0, up_ref[:, cols])
            return _conv3(u_ref[:, cols], halo, w_ref[0:1, cols], w_ref[1:2, cols], w_ref[2:3, cols]) + b_ref[:, cols]

        for j in range(f // cc):
            hg, hv = conv(slice(j * cc, (j + 1) * cc)), conv(slice(f + j * cc, f + (j + 1) * cc))
            a_ref[:, j * cc:(j + 1) * cc] = (hg * jax.nn.sigmoid(hg) * hv).astype(a_ref.dtype)

    return pl.pallas_call(
        body, name=name, grid=(nblk,),
        in_specs=[_row_spec(tt, f2), prev_spec, pl.BlockSpec((3, f2), lambda i: (0, 0)), _vec_spec(f2)],
        out_specs=_row_spec(tt, f), out_shape=jax.ShapeDtypeStruct((t, f), BF16),
        compiler_params=_params("parallel"),
    )(u, u, cw, cb)


def _ffn_bwd(u, da, cw, cb, name):
    t, f2 = u.shape
    f = f2 // 2
    tt, cc = _tile(t, 128, 16), _tile(f, 128, LANES)
    nblk = t // tt
    u_prev, u_next = _halo_specs(tt, f2, nblk)
    _, da_next = _halo_specs(tt, f, nblk)

    def body(u_ref, up_ref, un_ref, da_ref, dan_ref, w_ref, b_ref, du_ref, dw_ref):
        i = pl.program_id(0)
        first, last = i == 0, i == nblk - 1

        @pl.when(first)
        def _():
            dw_ref[...] = jnp.zeros_like(dw_ref)

        keep = jnp.where(last, 0.0, 1.0)

        def ext(cols):
            x = jnp.concatenate([u_ref[:, cols], un_ref[:, cols]], axis=0)
            halo = jnp.where(first, 0.0, up_ref[:, cols])
            x2, x1 = _shift_down(x, 2, halo), _shift_down(x, 1, halo)
            h = w_ref[0:1, cols] * x2 + w_ref[1:2, cols] * x1 + w_ref[2:3, cols] * x + b_ref[:, cols]
            return x, x1, x2, h

        def back(cols, dh_ext, x, x1, x2):
            w0, w1, w2 = w_ref[0:1, cols], w_ref[1:2, cols], w_ref[2:3, cols]
            dh = dh_ext[:tt]
            du_ref[:, cols] = (w2 * dh + w1 * _shift_up(dh_ext, 1, tt) + w0 * _shift_up(dh_ext, 2, tt)).astype(du_ref.dtype)
            dw_ref[0:1, cols] += jnp.sum(dh * x2[:tt], axis=0, keepdims=True)
            dw_ref[1:2, cols] += jnp.sum(dh * x1[:tt], axis=0, keepdims=True)
            dw_ref[2:3, cols] += jnp.sum(dh * x[:tt], axis=0, keepdims=True)
            dw_ref[3:4, cols] += jnp.sum(dh, axis=0, keepdims=True)

        row = lax.broadcasted_iota(jnp.int32, (tt + HALO, 1), 0)
        for j in range(f // cc):
            gcols, vcols = slice(j * cc, (j + 1) * cc), slice(f + j * cc, f + (j + 1) * cc)
            xg, xg1, xg2, hg = ext(gcols)
            xv, xv1, xv2, hv = ext(vcols)
            da_ext = jnp.concatenate([da_ref[:, gcols], dan_ref[:, gcols]], axis=0)
            da_ext = jnp.where(row < tt, da_ext, da_ext * keep)
            sg = jax.nn.sigmoid(hg)
            back(gcols, da_ext * hv * (sg * (1.0 + hg * (1.0 - sg))), xg, xg1, xg2)
            back(vcols, da_ext * (hg * sg), xv, xv1, xv2)

    return pl.pallas_call(
        body, name=name, grid=(nblk,),
        in_specs=[_row_spec(tt, f2), u_prev, u_next, _row_spec(tt, f), da_next,
                  pl.BlockSpec((3, f2), lambda i: (0, 0)), _vec_spec(f2)],
        out_specs=[_row_spec(tt, f2), pl.BlockSpec((HALO, f2), lambda i: (0, 0))],
        out_shape=[jax.ShapeDtypeStruct((t, f2), BF16), jax.ShapeDtypeStruct((HALO, f2), F32)],
        compiler_params=_params("arbitrary"),
    )(u, u, u, da, da, cw, cb)


def _gelu(x):
    cdf = 0.5 * (1.0 + lax.erf(x * INV_SQRT2))
    return x * cdf, cdf + x * (jnp.exp(-0.5 * x * x) * INV_SQRT_2PI)


def _sgu_common(p_ref, lg_ref, lb_ref, ws_ref, c, ch, groups):
    u, du_dp = _gelu(p_ref[:, :c])
    v, dv_dp = _gelu(p_ref[:, c:])
    mu = jnp.mean(v, axis=-1, keepdims=True)
    vc = v - mu
    rstd = lax.rsqrt(jnp.mean(vc * vc, axis=-1, keepdims=True) + EPS)
    xhat = vc * rstd
    vn = (xhat * lg_ref[...] + lb_ref[...]).astype(BF16)
    tril = lax.broadcasted_iota(jnp.int32, (ch, ch), 0) >= lax.broadcasted_iota(jnp.int32, (ch, ch), 1)
    wm = [jnp.where(tril, ws_ref[g], 0.0).astype(BF16) for g in range(groups)]
    return u, du_dp, dv_dp, xhat, rstd, vn, wm, tril


def _lane_pick(x, g):
    lane = lax.broadcasted_iota(jnp.int32, x.shape, 1)
    return jnp.sum(jnp.where(lane == g, x, 0.0), axis=1, keepdims=True)


def _sgu_specs(tt, c, ch, groups):
    return [_row_spec(tt, 2 * c), _vec_spec(c), _vec_spec(c),
            pl.BlockSpec((groups, ch, ch), lambda i: (0, 0, 0)), pl.BlockSpec((ch, LANES), lambda i: (0, 0))]


def _sgu_fwd(p, lg, lb, ws, bs_t, name):
    t, c2 = p.shape
    c = c2 // 2
    groups, ch, _ = ws.shape
    gc = c // groups
    tt = _tile(t, 2 * ch, ch)

    def body(p_ref, lg_ref, lb_ref, ws_ref, bs_ref, y_ref):
        u, _, _, _, _, vn, wm, _ = _sgu_common(p_ref, lg_ref, lb_ref, ws_ref, c, ch, groups)
        bs = bs_ref[...]
        for r in range(tt // ch):
            rows = slice(r * ch, (r + 1) * ch)
            for g in range(groups):
                cols = slice(g * gc, (g + 1) * gc)
                mixed = _dot_nn(wm[g], vn[rows, cols]) + _lane_pick(bs, g)
                y_ref[rows, cols] = (u[rows, cols] * mixed).astype(y_ref.dtype)

    return pl.pallas_call(
        body, name=name, grid=(t // tt,), in_specs=_sgu_specs(tt, c, ch, groups),
        out_specs=_row_spec(tt, c), out_shape=jax.ShapeDtypeStruct((t, c), BF16),
        compiler_params=_params("parallel"),
    )(p, lg, lb, ws, bs_t)


def _sgu_bwd(p, dy, lg, lb, ws, bs_t, name):
    t, c2 = p.shape
    c = c2 // 2
    groups, ch, _ = ws.shape
    gc = c // groups
    tt = _tile(t, 2 * ch, ch)

    def body(p_ref, dy_ref, lg_ref, lb_ref, ws_ref, bs_ref, dp_ref, dws_ref, dbs_ref, dlg_ref, dlb_ref, dvn_ref):
        @pl.when(pl.program_id(0) == 0)
        def _():
            dws_ref[...] = jnp.zeros_like(dws_ref)
            dbs_ref[...] = jnp.zeros_like(dbs_ref)
            dlg_ref[...] = jnp.zeros_like(dlg_ref)
            dlb_ref[...] = jnp.zeros_like(dlb_ref)

        u, du_dp, dv_dp, xhat, rstd, vn, wm, tril = _sgu_common(p_ref, lg_ref, lb_ref, ws_ref, c, ch, groups)
        bs = bs_ref[...]
        lane = lax.broadcasted_iota(jnp.int32, (ch, LANES), 1)
        for r in range(tt // ch):
            rows = slice(r * ch, (r + 1) * ch)
            for g in range(groups):
                cols = slice(g * gc, (g + 1) * gc)
                dyv, vng = dy_ref[rows, cols], vn[rows, cols]
                mixed = _dot_nn(wm[g], vng) + _lane_pick(bs, g)
                dp_ref[rows, cols] = (dyv * mixed * du_dp[rows, cols]).astype(dp_ref.dtype)
                dmixed = dyv * u[rows, cols]
                dmb = dmixed.astype(BF16)
                dws_ref[g] += jnp.where(tril, _dot_nt(dmb, vng), 0.0)
                dbs_ref[...] += jnp.where(lane == g, jnp.sum(dmixed, axis=1, keepdims=True), 0.0)
                dvn_ref[rows, cols] = _dot_tn(wm[g], dmb)
        dvn = dvn_ref[...]
        dlg_ref[...] += jnp.sum(dvn * xhat, axis=0, keepdims=True)
        dlb_ref[...] += jnp.sum(dvn, axis=0, keepdims=True)
        dxh = dvn * lg_ref[...]
        dv = rstd * (dxh - jnp.mean(dxh, axis=-1, keepdims=True) - xhat * jnp.mean(dxh * xhat, axis=-1, keepdims=True))
        dp_ref[:, c:] = (dv * dv_dp).astype(dp_ref.dtype)

    specs = _sgu_specs(tt, c, ch, groups)
    return pl.pallas_call(
        body, name=name, grid=(t // tt,),
        in_specs=[specs[0], _row_spec(tt, c)] + specs[1:],
        out_specs=[_row_spec(tt, c2), pl.BlockSpec((groups, ch, ch), lambda i: (0, 0, 0)),
                   pl.BlockSpec((ch, LANES), lambda i: (0, 0)), _vec_spec(c), _vec_spec(c)],
        out_shape=[jax.ShapeDtypeStruct((t, c2), BF16), jax.ShapeDtypeStruct((groups, ch, ch), F32),
                   jax.ShapeDtypeStruct((ch, LANES), F32), jax.ShapeDtypeStruct((1, c), F32), jax.ShapeDtypeStruct((1, c), F32)],
        scratch_shapes=[pltpu.VMEM((tt, c), F32)],
        compiler_params=_params("arbitrary"),
    )(p, dy, lg, lb, ws, bs_t)


def _split(x):
    hi = x.astype(BF16)
    return hi, (x - hi.astype(F32)).astype(BF16)


def _sb_block(q, ks, qpos, kb, tk, scale, r_carry, tri):
    z = _dot_nt(q, ks) * scale
    kpos = kb * tk + lax.broadcasted_iota(jnp.int32, (1, tk), 1)
    mask = kpos < qpos
    e = jnp.exp(-jnp.abs(z))
    lm = jnp.where(mask, -(jnp.maximum(z, 0.0) + jnp.log(1.0 + e)), 0.0)
    hi, lo = _split(lm)
    inc = _dot_nn(hi, tri) + _dot_nn(lo, tri)
    att = jnp.where(mask, jnp.exp(z + lm + (inc - lm + r_carry)), 0.0)
    return z, mask, e, inc, att


def _key_blocks(i, step, carry):
    def guarded(_, c):
        return lax.cond(c[-1] > EXP_UNDERFLOW, step, lambda same: same, c)

    carry = lax.fori_loop(0, jnp.minimum(i + 1, 2), lambda _, c: step(c), carry)
    return lax.cond(carry[-1] > EXP_UNDERFLOW, lambda c: lax.fori_loop(2, i + 1, guarded, c), lambda same: same, carry)


def _suffix_ones(tk):
    return (lax.broadcasted_iota(jnp.int32, (tk, tk), 0) >= lax.broadcasted_iota(jnp.int32, (tk, tk), 1)).astype(BF16)


def _attn_fwd(qkv, name):
    t, d3 = qkv.shape
    d = d3 // 3
    heads = d // HEAD_DIM
    tq = _tile(t, 256, LANES)
    nq = t // tq
    scale = HEAD_DIM ** -0.5

    def body(q_ref, k_ref, v_ref, of_ref, ob_ref):
        i = pl.program_id(1)
        q = q_ref[...]
        qpos = i * tq + lax.broadcasted_iota(jnp.int32, (tq, 1), 0)
        tri = _suffix_ones(tq)

        def step(carry):
            n, o, r_carry, _ = carry
            kb = i - n
            rows = pl.ds(pl.multiple_of(kb * tq, tq), tq)
            _, _, _, inc, att = _sb_block(q, k_ref[rows, :], qpos, kb, tq, scale, r_carry, tri)
            ahi, alo = _split(att)
            vs = v_ref[rows, :]
            r_new = r_carry + inc[:, 0:1]
            return n + 1, o + _dot_nn(ahi, vs) + _dot_nn(alo, vs), r_new, jnp.max(r_new)

        init = (jnp.int32(0), jnp.zeros((tq, HEAD_DIM), F32), jnp.zeros((tq, 1), F32), jnp.float32(0.0))
        _, o, _, _ = _key_blocks(i, step, init)
        of_ref[...] = o
        ob_ref[...] = o.astype(ob_ref.dtype)

    return pl.pallas_call(
        body, name=name, grid=(heads, nq),
        in_specs=[pl.BlockSpec((tq, HEAD_DIM), lambda h, i: (i, h)),
                  pl.BlockSpec((t, HEAD_DIM), lambda h, i: (0, heads + h)),
                  pl.BlockSpec((t, HEAD_DIM), lambda h, i: (0, 2 * heads + h))],
        out_specs=[pl.BlockSpec((tq, HEAD_DIM), lambda h, i: (i, h))] * 2,
        out_shape=[jax.ShapeDtypeStruct((t, d), F32), jax.ShapeDtypeStruct((t, d), BF16)],
        compiler_params=_params("parallel", "parallel"),
    )(qkv, qkv, qkv)


def _attn_bwd(qkv, do, of, name):
    t, d3 = qkv.shape
    d = d3 // 3
    heads = d // HEAD_DIM
    tq = _tile(t, 256, LANES)
    nq = t // tq
    scale = HEAD_DIM ** -0.5

    def body(q_ref, k_ref, v_ref, do_ref, of_ref, dq_ref, dk_ref, dv_ref, dk_acc, dv_acc):
        i = pl.program_id(1)

        @pl.when(i == 0)
        def _():
            dk_acc[...] = jnp.zeros_like(dk_acc)
            dv_acc[...] = jnp.zeros_like(dv_acc)

        q, dov = q_ref[...], do_ref[...]
        delta = jnp.sum(dov.astype(F32) * of_ref[...], axis=-1, keepdims=True)
        qpos = i * tq + lax.broadcasted_iota(jnp.int32, (tq, 1), 0)
        tri = _suffix_ones(tq)

        def step(carry):
            n, dq, r_carry, g_carry, _ = carry
            kb = i - n
            rows = pl.ds(pl.multiple_of(kb * tq, tq), tq)
            ks, vs = k_ref[rows, :], v_ref[rows, :]
            z, mask, e, inc, att = _sb_block(q, ks, qpos, kb, tq, scale, r_carry, tri)
            g = _dot_nt(dov, vs) * att
            ghi, glo = _split(g)
            ginc = _dot_nn(ghi, tri) + _dot_nn(glo, tri)
            beta = jnp.where(z >= 0.0, 1.0, e) / (1.0 + e)
            dz = jnp.where(mask, g * (1.0 - beta) - (delta - g_carry - ginc) * beta, 0.0) * scale
            dzb = dz.astype(BF16)
            dk_acc[rows, :] += _dot_tn(dzb, q)
            dv_acc[rows, :] += _dot_tn(att.astype(BF16), dov)
            r_new = r_carry + inc[:, 0:1]
            return n + 1, dq + _dot_nn(dzb, ks), r_new, g_carry + ginc[:, 0:1], jnp.max(r_new)

        zero = jnp.zeros((tq, 1), F32)
        init = (jnp.int32(0), jnp.zeros((tq, HEAD_DIM), F32), zero, zero, jnp.float32(0.0))
        _, dq, _, _, _ = _key_blocks(i, step, init)
        dq_ref[...] = dq.astype(dq_ref.dtype)

        @pl.when(i == nq - 1)
        def _():
            dk_ref[...] = dk_acc[...].astype(dk_ref.dtype)
            dv_ref[...] = dv_acc[...].astype(dv_ref.dtype)

    blk = pl.BlockSpec((tq, HEAD_DIM), lambda h, i: (i, h))
    full = pl.BlockSpec((t, HEAD_DIM), lambda h, i: (0, h))
    return pl.pallas_call(
        body, name=name, grid=(heads, nq),
        in_specs=[blk, pl.BlockSpec((t, HEAD_DIM), lambda h, i: (0, heads + h)),
                  pl.BlockSpec((t, HEAD_DIM), lambda h, i: (0, 2 * heads + h)), blk, blk],
        out_specs=[blk, full, full],
        out_shape=[jax.ShapeDtypeStruct((t, d), BF16)] * 3,
        scratch_shapes=[pltpu.VMEM((t, HEAD_DIM), F32), pltpu.VMEM((t, HEAD_DIM), F32)],
        compiler_params=_params("parallel", "arbitrary"),
    )(qkv, qkv, qkv, do, of)


def _pair_add(g, pair, name):
    _, _, rh, c = g.shape
    tr = _tile(rh, 256, 16)
    core = lax.axis_index("c").astype(jnp.int32).reshape(1)

    def body(c_ref, g_ref, p_ref, o_ref):
        o_ref[...] = (g_ref[...].astype(F32) + p_ref[...].astype(F32)).astype(o_ref.dtype)

    return pl.pallas_call(
        body, name=name,
        grid_spec=pltpu.PrefetchScalarGridSpec(
            num_scalar_prefetch=1, grid=(4, rh // tr),
            in_specs=[pl.BlockSpec((None, None, tr, c), lambda j, r, cr: (j, cr[0], r, 0)),
                      pl.BlockSpec((None, tr, c), lambda j, r, cr: (j, r, 0))],
            out_specs=pl.BlockSpec((None, tr, c), lambda j, r, cr: (j, r, 0))),
        out_shape=jax.ShapeDtypeStruct((4, rh, c), BF16),
        compiler_params=_params("parallel", "parallel"),
    )(core, g, pair)


def _sum_slots(x, name):
    n, r, c = x.shape
    tr = _tile(r, 256, 16)

    def body(x_ref, o_ref):
        acc = x_ref[0].astype(F32)
        for k in range(1, n):
            acc = acc + x_ref[k].astype(F32)
        o_ref[...] = acc

    return pl.pallas_call(
        body, name=name, grid=(r // tr,),
        in_specs=[pl.BlockSpec((n, tr, c), lambda i: (0, i, 0))],
        out_specs=pl.BlockSpec((tr, c), lambda i: (i, 0)),
        out_shape=jax.ShapeDtypeStruct((r, c), F32), compiler_params=_params("parallel"),
    )(x)


def _sum_own_slots(sums, slots, name):
    _, rh, c = sums.shape
    tr = _tile(rh, 256, 16)
    chip = (2 * lax.axis_index("x") + lax.axis_index("y")).astype(jnp.int32).reshape(1)

    def body(chip_ref, own_ref, slot_ref, o_ref):
        acc = own_ref[...].astype(F32)
        for k in range(3):
            acc = acc + slot_ref[k].astype(F32)
        o_ref[...] = acc

    return pl.pallas_call(
        body, name=name,
        grid_spec=pltpu.PrefetchScalarGridSpec(
            num_scalar_prefetch=1, grid=(rh // tr,),
            in_specs=[pl.BlockSpec((None, tr, c), lambda r, me: (me[0], r, 0)),
                      pl.BlockSpec((3, tr, c), lambda r, me: (0, r, 0))],
            out_specs=pl.BlockSpec((tr, c), lambda r, me: (r, 0))),
        out_shape=jax.ShapeDtypeStruct((rh, c), F32), compiler_params=_params("parallel"),
    )(chip, sums, slots)


def _adam_update(w, g, m, v):
    mn = ADAM_B1 * m + (1.0 - ADAM_B1) * g
    vn = ADAM_B2 * v + (1.0 - ADAM_B2) * (g * g)
    c1, c2 = 1.0 - ADAM_B1 ** ADAM_STEP, 1.0 - ADAM_B2 ** ADAM_STEP
    return -ADAM_LR * ((mn / c1) / (jnp.sqrt(vn / c2) + ADAM_EPS) + ADAM_WD * w), mn, vn


def _adamw_halves(w, own, recv, m, v, name):
    layers, _, rh, c = w.shape
    tr = _tile(rh, max(8, (1 << 18) // c), 8)
    core = lax.axis_index("c").astype(jnp.int32).reshape(1)

    def body(c_ref, w_ref, own_ref, recv_ref, m_ref, v_ref, g_ref, d_ref, mo_ref, vo_ref):
        g = jnp.where(pl.program_id(1) == c_ref[0], own_ref[...], recv_ref[...])
        g_ref[...] = g
        d_ref[...], mo_ref[...], vo_ref[...] = _adam_update(w_ref[...], g, m_ref[...], v_ref[...])

    full = pl.BlockSpec((None, None, tr, c), lambda l, hf, r, cr: (l, hf, r, 0))
    half = pl.BlockSpec((None, tr, c), lambda l, hf, r, cr: (l, r, 0))
    return pl.pallas_call(
        body, name=name,
        grid_spec=pltpu.PrefetchScalarGridSpec(
            num_scalar_prefetch=1, grid=(layers, 2, rh // tr),
            in_specs=[full, half, half, full, full], out_specs=[full] * 4),
        out_shape=[jax.ShapeDtypeStruct(w.shape, F32)] * 4,
        compiler_params=_params("parallel", "parallel", "parallel"),
    )(core, w, own, recv, m, v)


def _adamw(w, g, m, v, name):
    r, c = w.shape
    tr = _tile(r, max(8, (1 << 18) // c), 8)

    def body(w_ref, g_ref, m_ref, v_ref, d_ref, mo_ref, vo_ref):
        d_ref[...], mo_ref[...], vo_ref[...] = _adam_update(w_ref[...], g_ref[...], m_ref[...], v_ref[...])

    spec = pl.BlockSpec((tr, c), lambda i: (i, 0))
    return pl.pallas_call(
        body, name=name, grid=(r // tr,), in_specs=[spec] * 4, out_specs=[spec] * 3,
        out_shape=[jax.ShapeDtypeStruct((r, c), F32)] * 3, compiler_params=_params("parallel"),
    )(w, g, m, v)


def _place():
    x, y, c = lax.axis_index("x"), lax.axis_index("y"), lax.axis_index("c")
    chips = [(1 - x, y), (x, 1 - y), (1 - x, 1 - y)]
    return x, y, c, chips


def _remote(src, dst, send_sems, recv_sems, k, to):
    return pltpu.make_async_remote_copy(src_ref=src, dst_ref=dst, send_sem=send_sems.at[k], recv_sem=recv_sems.at[k],
                                        device_id=to, device_id_type=MESH)


def _comm_call(body, name, ins, out_shape, n_remote, n_local=0):
    sems = [pltpu.SemaphoreType.DMA((n_remote,)), pltpu.SemaphoreType.DMA((n_remote,))]
    if n_local:
        sems.append(pltpu.SemaphoreType.DMA((n_local,)))
    return pl.pallas_call(
        body, name=name, in_specs=[ANY] * len(ins), out_specs=[ANY] * len(out_shape), out_shape=out_shape,
        scratch_shapes=sems,
    )(*ins)


def _all_gather(shards, name):
    n = len(shards)

    def body(*refs):
        ins, outs = refs[:n], refs[n:2 * n]
        send_sems, recv_sems = refs[2 * n:]
        x, y, c, chips = _place()
        me, sib = 2 * x + y, (x, y, 1 - c)
        started = []
        for a in range(n):
            cp = _remote(ins[a], outs[a].at[me], send_sems, recv_sems, 7 * a + 6, sib)
            cp.start()
            started.append(cp)
            for k, (px, py) in enumerate(chips):
                cp = _remote(ins[a].at[c], outs[a].at[me, c], send_sems, recv_sems, 7 * a + k, (px, py, c))
                cp.start()
                started.append(cp)
        for a in range(n):
            for k, (px, py) in enumerate(chips):
                land = outs[a].at[2 * px + py, c]
                _remote(land, land, send_sems, recv_sems, 7 * a + k, sib).wait_recv()
                cp = _remote(land, land, send_sems, recv_sems, 7 * a + 3 + k, sib)
                cp.start()
                started.append(cp)
        for a in range(n):
            for k, (px, py) in enumerate(chips):
                land = outs[a].at[2 * px + py, 1 - c]
                _remote(land, land, send_sems, recv_sems, 7 * a + 3 + k, sib).wait_recv()
            own = outs[a].at[me]
            _remote(own, own, send_sems, recv_sems, 7 * a + 6, sib).wait_recv()
        for cp in started:
            cp.wait_send()

    out_shape = [jax.ShapeDtypeStruct((4,) + s.shape, s.dtype) for s in shards]
    return _comm_call(body, name, shards, out_shape, 7 * n)


def _pair_exchange_side(grads):
    n = len(grads)

    def copies(ins, outs, send_sems, recv_sems):
        x, y, c, _ = _place()
        return [_remote(ins[a].at[j, 1 - c], outs[a].at[j], send_sems, recv_sems, 4 * a + j, (x, y, 1 - c))
                for a in range(n) for j in range(4)]

    return Side(grads, [jax.ShapeDtypeStruct((4,) + g.shape[2:], g.dtype) for g in grads], 4 * n, copies)


def _gather_sides(shards):
    n = len(shards)
    full_shape = [jax.ShapeDtypeStruct((4,) + s.shape, s.dtype) for s in shards]

    def first(sel):
        def ici(ins, outs, send_sems, recv_sems):
            x, y, c, chips = _place()
            return [_remote(ins[a].at[c], outs[a].at[2 * x + y, c], send_sems, recv_sems, 3 * a + k, (px, py, c))
                    for a in range(len(sel)) for k, (px, py) in enumerate(chips)]

        return Side([shards[a] for a in sel], [full_shape[a] for a in sel], 3 * len(sel), ici)

    def d2d(ins, outs, send_sems, recv_sems):
        x, y, c, chips = _place()
        sib, cps = (x, y, 1 - c), []
        for a in range(n):
            cps.append(_remote(ins[a], outs[a].at[2 * x + y], send_sems, recv_sems, 4 * a + 3, sib))
            for k, (px, py) in enumerate(chips):
                land = outs[a].at[2 * px + py, c]
                cps.append(_remote(land, land, send_sems, recv_sems, 4 * a + k, sib))
        return cps

    def second(partial):
        return Side(list(shards) + list(partial), full_shape, 4 * n, d2d, aliases={n + a: a for a in range(n)})

    return first, second


def _chip_exchange_side(sums):
    n = len(sums)

    def copies(ins, outs, send_sems, recv_sems):
        _, _, c, chips = _place()
        return [_remote(ins[a].at[2 * px + py], outs[a].at[k], send_sems, recv_sems, 3 * a + k, (px, py, c))
                for a in range(n) for k, (px, py) in enumerate(chips)]

    return Side(sums, [jax.ShapeDtypeStruct((3,) + s.shape[1:], s.dtype) for s in sums], 3 * n, copies)


def _halves_exchange(own, name):
    n = len(own)

    def body(*refs):
        ins, outs = refs[:n], refs[n:2 * n]
        send_sems, recv_sems = refs[2 * n:]
        x, y, c, _ = _place()
        started = [_remote(ins[a], outs[a], send_sems, recv_sems, a, (x, y, 1 - c)) for a in range(n)]
        for cp in started:
            cp.start()
        for cp in started:
            cp.wait_recv()
        for cp in started:
            cp.wait_send()

    return _comm_call(body, name, own, [jax.ShapeDtypeStruct(t.shape, t.dtype) for t in own], n)


def _broadcast_small(buf, name):
    def body(in_ref, out_ref, send_sems, recv_sems, loc_sems):
        x, y, c, _ = _place()
        me = 4 * x + 2 * y + c
        loc = pltpu.make_async_copy(in_ref, out_ref.at[me], loc_sems.at[0])
        loc.start()
        peers = [(x ^ (k >> 2 & 1), y ^ (k >> 1 & 1), c ^ (k & 1)) for k in range(1, 8)]
        started = []
        for k, to in enumerate(peers):
            cp = _remote(in_ref, out_ref.at[me], send_sems, recv_sems, k, to)
            cp.start()
            started.append(cp)
        for k, (px, py, pc) in enumerate(peers):
            land = out_ref.at[4 * px + 2 * py + pc]
            _remote(land, land, send_sems, recv_sems, k, (px, py, pc)).wait_recv()
        for cp in started:
            cp.wait_send()
        loc.wait()

    return _comm_call(body, name, [buf], [jax.ShapeDtypeStruct((8,) + buf.shape, buf.dtype)], 7, 1)[0]


def _pack(arrays):
    flat = jnp.concatenate([a.reshape(-1).astype(F32) for a in arrays])
    pad = -flat.shape[0] % (256 * LANES)
    return jnp.pad(flat, (0, pad)).reshape(-1, LANES)


def _unpack(buf, shapes):
    flat, out, at = buf.reshape(-1), [], 0
    for s in shapes:
        size = math.prod(s)
        out.append(flat[at:at + size].reshape(s))
        at += size
    return out


def _adamw_any(w, g, m, v, name):
    shape = w.shape
    two = (-1, shape[-1]) if w.ndim >= 2 else (1, -1)
    outs = _adamw(w.reshape(two), g.reshape(two), m.reshape(two), v.reshape(two), name)
    return [o.reshape(shape) for o in outs]


BIG = ("sc_w_in", "sc_w_out", "sg_w_in", "sg_w_out", "sb_w_qkv", "sb_w_out", "ffn_w_up", "ffn_w_down")
ROW_SHARDED = ("sc_w_out", "sg_w_out", "sb_w_out", "ffn_w_down")
MIXER_WEIGHTS = (("sc_w_in", "sc_w_out"), ("sg_w_in", "sg_w_out"), ("sb_w_qkv", "sb_w_out"))
WEIGHTS = ("norm_mix_pre", "norm_mix_post", "norm_ffn_pre", "norm_ffn_post", "sc_w_in", "sc_conv_w", "sc_w_out",
           "sg_w_in", "sg_ln_g", "sg_ln_b", "sg_w_s", "sg_b_s", "sg_w_out", "sb_w_qkv", "sb_w_out",
           "ffn_w_up", "ffn_conv_w", "ffn_conv_b", "ffn_w_down")


def _step(p):
    t, d = p["x"].shape[1:]
    x, target = p["x"].reshape(t, d), p["loss_target"].reshape(t, d)
    depth = p["norm_mix_pre"].shape[0]
    chip = 2 * lax.axis_index("x") + lax.axis_index("y")

    def layer_entities(i):
        name_in, name_out = MIXER_WEIGHTS[i % N_MIXERS]
        return [(name_in, i // N_MIXERS), (name_out, i // N_MIXERS), ("ffn_w_up", i), ("ffn_w_down", i)]

    def shard(e):
        s = p[e[0]][e[1]].astype(BF16)
        return s.reshape(2, s.shape[0] // 2, s.shape[1])

    def as_weight(e, g):
        rows, cols = p[e[0]].shape[1:]
        return g.reshape(1, 4 * rows, cols) if e[0] in ROW_SHARDED else g.reshape(4, rows, cols)

    ents = [e for i in range(depth) for e in layer_entities(i)]
    full = {e: as_weight(e, g) for e, g in zip(layer_entities(0), _all_gather([shard(e) for e in layer_entities(0)],
                                                                              "gather_weights_0"))}

    conv_shapes = [p["sc_conv_w"].shape, p["ffn_conv_w"].shape]
    conv_all = _broadcast_small(_pack([p["sc_conv_w"], p["ffn_conv_w"]]), "gather_conv_taps")[0::2]
    sc_cw, ffn_cw = [jnp.moveaxis(jnp.stack([_unpack(conv_all[j], conv_shapes)[i] for j in range(4)]), 0, 2)
                     .reshape(s[0], s[1], 4 * s[2]) for i, s in enumerate(conv_shapes)]

    ws = p["sg_w_s"]
    groups, ch = ws.shape[1], ws.shape[2]
    bs_t = [jnp.pad(p["sg_b_s"][j].T, ((0, 0), (0, LANES - groups))) for j in range(ws.shape[0])]
    g1, g2, g3, g4 = [[p[k][i:i + 1] for i in range(depth)] for k in WEIGHTS[:4]]

    saved = []
    h, hn = x, _rms_fwd(x, g1[0], "rms_in")
    for i in range(depth):
        kind, j = i % N_MIXERS, i // N_MIXERS
        nxt = layer_entities(i + 1) if i < depth - 1 else []
        first, second = _gather_sides([shard(e) for e in nxt]) if nxt else (lambda sel: None, lambda partial: None)
        w_in, w_out = [full[name, j] for name in MIXER_WEIGHTS[kind]]
        pre, *part_a = _listed(_mm_nn(hn, w_in, BF16 if kind == 2 else F32, f"{('sc', 'sg', 'sb')[kind]}_in_{i}", first([0, 1])))
        extra = None
        if kind == 0:
            y = _sc_fwd(pre, sc_cw[j], f"sc_mix_{i}")
        elif kind == 1:
            y = _sgu_fwd(pre, p["sg_ln_g"][j:j + 1], p["sg_ln_b"][j:j + 1], ws[j], bs_t[j], f"sg_mix_{i}")
        else:
            extra, y = _attn_fwd(pre, f"sb_mix_{i}")
        m = _mm_nn(y, w_out, F32, f"mix_out_{i}")
        h1, fn = _norm_step(h, m, g2[i], g3[i], f"norm_mid_{i}")
        up, *part_b = _listed(_mm_nn(fn, full["ffn_w_up", i], F32, f"ffn_up_{i}", first([2, 3])))
        act = _ffn_fwd(up, ffn_cw[i], p["ffn_conv_b"][i:i + 1], f"ffn_act_{i}")
        f, *gathered = _listed(_mm_nn(act, full["ffn_w_down", i], F32, f"ffn_down_{i}", second(part_a + part_b)))
        full.update({e: as_weight(e, g) for e, g in zip(nxt, gathered)})
        saved.append(dict(h=h, hn=hn, pre=pre, y=y, extra=extra, m=m, h1=h1, fn=fn, up=up, act=act, f=f,
                          w_in=w_in, w_out=w_out))
        if i < depth - 1:
            h, hn = _norm_step(h1, f, g4[i], g1[i + 1], f"norm_end_{i}")
    loss, dh, df, dg4 = _final_loss(h1, f, g4[depth - 1], target, "loss_head")

    gbig, gsm = {}, {k: [None] * depth for k in ("g1", "g2", "g3", "g4", "ffn_cw", "ffn_cb")}
    gsm["g4"][depth - 1] = dg4
    gsm["sc_cw"] = [None] * p["sc_conv_w"].shape[0]
    for k in ("sg_lg", "sg_lb", "sg_ws", "sg_bs"):
        gsm[k] = [None] * ws.shape[0]
    pending, sums, slots = [], {}, {}

    def parts(e):
        return gbig[e].reshape(4, 2, -1, gbig[e].shape[-1])

    def pair_side(entities):
        return _pair_exchange_side([parts(e) for e in entities]) if entities else None

    def pair_sums(entities, pair, tag):
        sums.update({e: _pair_add(parts(e), q, f"grad_pair_add_{tag}_{a}") for a, (e, q) in enumerate(zip(entities, pair))})

    def chip_side(entities):
        return _chip_exchange_side([sums[e] for e in entities]) if entities else None

    for i in reversed(range(depth)):
        s = saved[i]
        kind, j = i % N_MIXERS, i // N_MIXERS
        ffn_ents = [("ffn_w_up", i), ("ffn_w_down", i)]
        dact, *pair = _listed(_mm_nt(df, full["ffn_w_down", i], F32, f"d_ffn_act_{i}", pair_side(pending)))
        pair_sums(pending, pair, f"mix_{i + 1}")
        gbig["ffn_w_down", i], *landed = _listed(_mm_tn(s["act"], df, 1, f"g_ffn_down_{i}", chip_side(pending)))
        slots.update(zip(pending, landed))
        dup, dcw = _ffn_bwd(s["up"], dact, ffn_cw[i], p["ffn_conv_b"][i:i + 1], f"d_ffn_up_{i}")
        gsm["ffn_cw"][i], gsm["ffn_cb"][i] = dcw[0:3], dcw[3:4]
        dfn = _mm_nt(dup, full["ffn_w_up", i], F32, f"d_ffn_in_{i}")
        gbig["ffn_w_up", i] = _mm_tn(s["fn"], dup, 4, f"g_ffn_up_{i}")
        dh1, gsm["g3"][i], dm, gsm["g2"][i] = _norm_bwd_step(dh, dfn, s["h1"], g3[i], (s["m"], g2[i]), f"d_norm_mid_{i}")
        name_in, name_out = MIXER_WEIGHTS[kind]
        gbig[name_out, j], *pair = _listed(_mm_tn(s["y"], dm, 1, f"g_mix_out_{i}", pair_side(ffn_ents)))
        pair_sums(ffn_ents, pair, f"ffn_{i}")
        if kind == 0:
            dy = _mm_nt(dm, s["w_out"], F32, f"d_mix_y_{i}")
            dpre, dwc = _sc_bwd(s["pre"], dy, sc_cw[j], f"d_sc_mix_{i}")
            gsm["sc_cw"][j] = dwc[0:3]
        elif kind == 1:
            dy = _mm_nt(dm, s["w_out"], F32, f"d_mix_y_{i}")
            dpre, dws, dbs, dlg, dlb = _sgu_bwd(s["pre"], dy, p["sg_ln_g"][j:j + 1], p["sg_ln_b"][j:j + 1], ws[j], bs_t[j],
                                                f"d_sg_mix_{i}")
            gsm["sg_lg"][j], gsm["sg_lb"][j], gsm["sg_ws"][j], gsm["sg_bs"][j] = dlg, dlb, dws, dbs[:, :groups].T
        else:
            do = _mm_nt(dm, s["w_out"], BF16, f"d_mix_y_{i}")
            dpre = jnp.concatenate(_attn_bwd(s["pre"], do, s["extra"], f"d_sb_mix_{i}"), axis=1)
        dhn, *landed = _listed(_mm_nt(dpre, s["w_in"], F32, f"d_mix_in_{i}", chip_side([("ffn_w_up", i)])))
        slots["ffn_w_up", i] = landed[0]
        gbig[name_in, j], *landed = _listed(_mm_tn(s["hn"], dpre, 4, f"g_mix_in_{i}", chip_side([("ffn_w_down", i)])))
        slots["ffn_w_down", i] = landed[0]
        if i > 0:
            dh, gsm["g1"][i], df, gsm["g4"][i - 1] = _norm_bwd_step(dh1, dhn, s["h"], g1[i],
                                                                   (saved[i - 1]["f"], g4[i - 1]), f"d_norm_in_{i}")
        else:
            dx, gsm["g1"][0] = _norm_bwd_step(dh1, dhn, s["h"], g1[0], None, "d_norm_in_0")
        pending = [(name_in, j), (name_out, j)]
    pair_sums(pending, _side_call(pair_side(pending), "grad_pair_exchange_mix_0"), "mix_0")
    slots.update(zip(pending, _side_call(chip_side(pending), "grad_chip_exchange_mix_0")))

    halves = {e: _sum_own_slots(sums[e], slots[e], f"grad_chip_sum_{a}") for a, e in enumerate(ents)}
    own = [jnp.stack([halves[name, l] for l in range(p[name].shape[0])]) for name in BIG]
    recv = _halves_exchange(own, "grad_halves_exchange")
    grads, delta, new_m, new_v = {}, {}, {}, {}
    for name, mine, theirs in zip(BIG, own, recv):
        shape = p[name].shape
        view = (shape[0], 2, shape[1] // 2, shape[2])
        outs = _adamw_halves(p[name].reshape(view), mine, theirs, p["m_" + name].reshape(view), p["v_" + name].reshape(view),
                             f"adamw_{name}")
        grads[name], delta[name], new_m[name], new_v[name] = [o.reshape(shape) for o in outs]

    small = [jnp.concatenate(gsm[k]) for k in ("g1", "g2", "g3", "g4", "sg_lg", "sg_lb")] + [
        jnp.stack(gsm["sg_ws"]), jnp.stack(gsm["sg_bs"]),
        jnp.concatenate(gsm["ffn_cb"]), jnp.stack(gsm["sc_cw"]), jnp.stack(gsm["ffn_cw"])]
    small_shapes = [a.shape for a in small]
    total = _sum_slots(_broadcast_small(_pack(small), "small_grad_exchange"), "small_grad_sum")
    sm = _unpack(total, small_shapes)
    for k, name in enumerate(("norm_mix_pre", "norm_mix_post", "norm_ffn_pre", "norm_ffn_post", "sg_ln_g", "sg_ln_b")):
        grads[name] = sm[k].reshape(p[name].shape)
    grads["sg_w_s"], grads["sg_b_s"] = sm[6].reshape(ws.shape), sm[7].reshape(p["sg_b_s"].shape)
    grads["ffn_conv_b"] = sm[8].reshape(p["ffn_conv_b"].shape)
    for name, full_g in (("sc_conv_w", sm[9]), ("ffn_conv_w", sm[10])):
        n = p[name].shape[-1]
        grads[name] = lax.dynamic_slice_in_dim(full_g, chip * n, n, axis=2)

    for name in WEIGHTS:
        if name in BIG:
            continue
        delta[name], new_m[name], new_v[name] = _adamw_any(p[name], grads[name], p["m_" + name], p["v_" + name], f"adamw_{name}")

    loss = lax.psum(loss[0, 0], ("x", "y", "c"))
    return (loss, dx.reshape(p["x"].shape), *[grads[n] for n in WEIGHTS], *[delta[n] for n in WEIGHTS],
            *[new_m[n] for n in WEIGHTS], *[new_v[n] for n in WEIGHTS])


def kernel(x, norm_mix_pre, norm_mix_post, norm_ffn_pre, norm_ffn_post, sc_w_in, sc_conv_w, sc_w_out, sg_w_in, sg_ln_g, sg_ln_b, sg_w_s, sg_b_s, sg_w_out, sb_w_qkv, sb_w_out, ffn_w_up, ffn_conv_w, ffn_conv_b, ffn_w_down, loss_target, m_norm_mix_pre, m_norm_mix_post, m_norm_ffn_pre, m_norm_ffn_post, m_sc_w_in, m_sc_conv_w, m_sc_w_out, m_sg_w_in, m_sg_ln_g, m_sg_ln_b, m_sg_w_s, m_sg_b_s, m_sg_w_out, m_sb_w_qkv, m_sb_w_out, m_ffn_w_up, m_ffn_conv_w, m_ffn_conv_b, m_ffn_w_down, v_norm_mix_pre, v_norm_mix_post, v_norm_ffn_pre, v_norm_ffn_post, v_sc_w_in, v_sc_conv_w, v_sc_w_out, v_sg_w_in, v_sg_ln_g, v_sg_ln_b, v_sg_w_s, v_sg_b_s, v_sg_w_out, v_sb_w_qkv, v_sb_w_out, v_ffn_w_up, v_ffn_conv_w, v_ffn_conv_b, v_ffn_w_down):
    return _step(dict(locals()))
```

```python
import functools
import math

import jax
import jax.numpy as jnp
from jax import lax
from jax.experimental import pallas as pl
from jax.experimental.pallas import tpu as pltpu

F32 = jnp.float32
BF16 = jnp.bfloat16
MESH = pl.DeviceIdType.MESH
ANY = pl.BlockSpec(memory_space=pl.ANY)

EPS = 1e-6
HEAD_DIM = 128
N_MIXERS = 3
ADAM_LR, ADAM_B1, ADAM_B2, ADAM_EPS, ADAM_WD, ADAM_STEP = 0.001, 0.9, 0.999, 1e-08, 0.01, 10
V7X_VMEM_LIMIT = 56 * 1024 * 1024
HALO = 8
LANES = 128
EXP_UNDERFLOW = -104.0
INV_SQRT2 = 1.0 / math.sqrt(2.0)
INV_SQRT_2PI = 1.0 / math.sqrt(2.0 * math.pi)


def _tile(n, target, mult):
    t = min(n, target) // mult * mult
    while t >= mult:
        if n % t == 0:
            return t
        t -= mult
    return n


def _params(*sem):
    return pltpu.CompilerParams(dimension_semantics=sem, vmem_limit_bytes=V7X_VMEM_LIMIT)


def _dot(a, b, dims):
    return lax.dot_general(a, b, (dims, ((), ())), preferred_element_type=F32)


def _dot_nn(a, b):
    return _dot(a, b, ((1,), (0,)))


def _dot_nt(a, b):
    return _dot(a, b, ((1,), (1,)))


def _dot_tn(a, b):
    return _dot(a, b, ((0,), (0,)))


class Side:
    def __init__(self, ins, out_shape, n, copies, aliases=None):
        self.ins, self.out_shape, self.n, self.copies, self.aliases = list(ins), list(out_shape), n, copies, aliases or {}


def _grid_call(core, name, grid, in_specs, out_spec, out_shape, scratch, operands, side):
    if side is None:
        return pl.pallas_call(
            core, name=name, grid=grid, in_specs=in_specs, out_specs=out_spec, out_shape=out_shape, scratch_shapes=scratch,
            compiler_params=_params("parallel", "parallel", "arbitrary"))(*operands)
    n_in, n_sin, n_sout = len(operands), len(side.ins), len(side.out_shape)

    def body(*refs):
        ins, sins = refs[:n_in], refs[n_in:n_in + n_sin]
        out, souts = refs[n_in + n_sin], refs[n_in + n_sin + 1:n_in + n_sin + 1 + n_sout]
        scr, (send_sems, recv_sems) = refs[n_in + n_sin + 1 + n_sout:-2], refs[-2:]
        ids = [pl.program_id(d) for d in range(len(grid))]
        first = functools.reduce(jnp.logical_and, [i == 0 for i in ids])
        last = functools.reduce(jnp.logical_and, [i == g - 1 for i, g in zip(ids, grid)])

        @pl.when(first)
        def _():
            for cp in side.copies(sins, souts, send_sems, recv_sems):
                cp.start()

        core(*ins, out, *scr)

        @pl.when(last)
        def _():
            cps = side.copies(sins, souts, send_sems, recv_sems)
            for cp in cps:
                cp.wait_recv()
            for cp in cps:
                cp.wait_send()

    return pl.pallas_call(
        body, name=name, grid=grid, in_specs=list(in_specs) + [ANY] * n_sin, out_specs=[out_spec] + [ANY] * n_sout,
        out_shape=[out_shape] + side.out_shape,
        scratch_shapes=list(scratch) + [pltpu.SemaphoreType.DMA((side.n,)), pltpu.SemaphoreType.DMA((side.n,))],
        input_output_aliases={n_in + i: 1 + o for i, o in side.aliases.items()},
        compiler_params=_params("arbitrary", "arbitrary", "arbitrary"))(*operands, *side.ins)


def _side_call(side, name):
    n_sin, n_sout = len(side.ins), len(side.out_shape)

    def body(*refs):
        cps = side.copies(refs[:n_sin], refs[n_sin:n_sin + n_sout], refs[-2], refs[-1])
        for cp in cps:
            cp.start()
        for cp in cps:
            cp.wait_recv()
        for cp in cps:
            cp.wait_send()

    return pl.pallas_call(
        body, name=name, in_specs=[ANY] * n_sin, out_specs=[ANY] * n_sout, out_shape=side.out_shape,
        scratch_shapes=[pltpu.SemaphoreType.DMA((side.n,)), pltpu.SemaphoreType.DMA((side.n,))],
        input_output_aliases=dict(side.aliases))(*side.ins)


def _listed(result):
    return list(result) if isinstance(result, (list, tuple)) else [result]


def _reduce_core(dot, steps):
    if steps == 1:
        def core(a_ref, b_ref, o_ref):
            o_ref[...] = dot(a_ref[...], b_ref[...]).astype(o_ref.dtype)
        return core

    def core(a_ref, b_ref, o_ref, acc):
        r = pl.program_id(2)

        @pl.when(r == 0)
        def _():
            acc[...] = jnp.zeros_like(acc)

        acc[...] += dot(a_ref[...], b_ref[...])

        @pl.when(r == steps - 1)
        def _():
            o_ref[...] = acc[...].astype(o_ref.dtype)
    return core


def _acc(steps, shape):
    return [] if steps == 1 else [pltpu.VMEM(shape, F32)]


def _mm_nn(a, w, out_dtype, name, side=None):
    m, k = a.shape
    s, _, n = w.shape
    tm, tn, tk = _tile(m, 1024, 16), _tile(n, 1536, LANES), _tile(k, 2816, LANES)
    nb, nk = n // tn, k // tk
    return _grid_call(
        _reduce_core(_dot_nn, nk), name, (m // tm, s * nb, nk),
        [pl.BlockSpec((tm, tk), lambda i, j, kk: (i, kk)),
         pl.BlockSpec((None, tk, tn), lambda i, j, kk: (j // nb, kk, j % nb))],
        pl.BlockSpec((tm, tn), lambda i, j, kk: (i, j)), jax.ShapeDtypeStruct((m, s * n), out_dtype),
        _acc(nk, (tm, tn)), (a, w), side)


def _mm_nt(dy, w, out_dtype, name, side=None):
    m = dy.shape[0]
    s, k, n = w.shape
    tm, tn, tko = _tile(m, 1024, 16), _tile(n, 2816, LANES), _tile(k, 1536, LANES)
    nb = n // tn
    nr = s * nb
    return _grid_call(
        _reduce_core(_dot_nt, nr), name, (m // tm, k // tko, nr),
        [pl.BlockSpec((tm, tn), lambda i, j, r: (i, r)),
         pl.BlockSpec((None, tko, tn), lambda i, j, r: (r // nb, j, r % nb))],
        pl.BlockSpec((tm, tko), lambda i, j, r: (i, j)), jax.ShapeDtypeStruct((m, k), out_dtype),
        _acc(nr, (tm, tko)), (dy, w), side)


def _mm_tn(x, dy, s, name, side=None):
    t, k = x.shape
    n = dy.shape[1] // s
    tk, tn, tt = _tile(k, 1408, LANES), _tile(n, 1536, LANES), _tile(t, 2048, 16)
    nb, nt = n // tn, t // tt
    return _grid_call(
        _reduce_core(_dot_tn, nt), name, (k // tk, s * nb, nt),
        [pl.BlockSpec((tt, tk), lambda i, j, q: (q, i)),
         pl.BlockSpec((tt, tn), lambda i, j, q: (q, j))],
        pl.BlockSpec((None, tk, tn), lambda i, j, q: (j // nb, i, j % nb)), jax.ShapeDtypeStruct((s, k, n), BF16),
        _acc(nt, (tk, tn)), (x, dy), side)


def _rms(x, g):
    return x * lax.rsqrt(jnp.mean(x * x, axis=-1, keepdims=True) + EPS) * g


def _rms_bwd(x, g, dy):
    r = lax.rsqrt(jnp.mean(x * x, axis=-1, keepdims=True) + EPS)
    gy = dy * g
    dx = r * gy - x * (r * r * r * jnp.mean(x * gy, axis=-1, keepdims=True))
    return dx, dy * (x * r)


def _row_spec(tt, d):
    return pl.BlockSpec((tt, d), lambda i: (i, 0))


def _vec_spec(d):
    return pl.BlockSpec((1, d), lambda i: (0, 0))


def _rms_fwd(x, g, name):
    t, d = x.shape
    tt = _tile(t, 512, 16)

    def body(x_ref, g_ref, o_ref):
        o_ref[...] = _rms(x_ref[...], g_ref[...]).astype(o_ref.dtype)

    return pl.pallas_call(
        body, name=name, grid=(t // tt,),
        in_specs=[_row_spec(tt, d), _vec_spec(d)], out_specs=_row_spec(tt, d),
        out_shape=jax.ShapeDtypeStruct((t, d), BF16), compiler_params=_params("parallel"),
    )(x, g)


def _norm_step(h, m, g_post, g_next, name):
    t, d = h.shape
    tt = _tile(t, 512, 16)

    def body(h_ref, m_ref, gp_ref, gn_ref, ho_ref, xn_ref):
        hn = h_ref[...] + _rms(m_ref[...], gp_ref[...])
        ho_ref[...] = hn
        xn_ref[...] = _rms(hn, gn_ref[...]).astype(xn_ref.dtype)

    return pl.pallas_call(
        body, name=name, grid=(t // tt,),
        in_specs=[_row_spec(tt, d), _row_spec(tt, d), _vec_spec(d), _vec_spec(d)],
        out_specs=[_row_spec(tt, d), _row_spec(tt, d)],
        out_shape=[jax.ShapeDtypeStruct((t, d), F32), jax.ShapeDtypeStruct((t, d), BF16)],
        compiler_params=_params("parallel"),
    )(h, m, g_post, g_next)


def _final_loss(h, f, g_post, target, name):
    t, d = h.shape
    tt = _tile(t, 256, 16)

    def body(h_ref, f_ref, g_ref, tg_ref, loss_ref, dh_ref, df_ref, dg_ref):
        @pl.when(pl.program_id(0) == 0)
        def _():
            loss_ref[...] = jnp.zeros_like(loss_ref)
            dg_ref[...] = jnp.zeros_like(dg_ref)

        fv, g = f_ref[...], g_ref[...]
        err = h_ref[...] + _rms(fv, g) - tg_ref[...]
        per_row = jnp.mean(err * err, axis=-1, keepdims=True)
        loss_ref[...] += 0.5 * jnp.sum(per_row, axis=0, keepdims=True)
        dh = err * (1.0 / d)
        dh_ref[...] = dh
        df, dg = _rms_bwd(fv, g, dh)
        df_ref[...] = df.astype(df_ref.dtype)
        dg_ref[...] += jnp.sum(dg, axis=0, keepdims=True)

    return pl.pallas_call(
        body, name=name, grid=(t // tt,),
        in_specs=[_row_spec(tt, d), _row_spec(tt, d), _vec_spec(d), _row_spec(tt, d)],
        out_specs=[pl.BlockSpec((1, 1), lambda i: (0, 0)), _row_spec(tt, d), _row_spec(tt, d), _vec_spec(d)],
        out_shape=[jax.ShapeDtypeStruct((1, 1), F32), jax.ShapeDtypeStruct((t, d), F32),
                   jax.ShapeDtypeStruct((t, d), BF16), jax.ShapeDtypeStruct((1, d), F32)],
        compiler_params=_params("arbitrary"),
    )(h, f, g_post, target)


def _norm_bwd_step(dh_out, dxn, x, g_pre, prev, name):
    t, d = x.shape
    tt = _tile(t, 256, 16)
    has_prev = prev is not None

    def body(*refs):
        if has_prev:
            dho_ref, dxn_ref, x_ref, g_ref, xa_ref, ga_ref, dh_ref, dg_ref, da_ref, dga_ref = refs
        else:
            dho_ref, dxn_ref, x_ref, g_ref, dh_ref, dg_ref = refs

        @pl.when(pl.program_id(0) == 0)
        def _():
            dg_ref[...] = jnp.zeros_like(dg_ref)
            if has_prev:
                dga_ref[...] = jnp.zeros_like(dga_ref)

        dx, dg = _rms_bwd(x_ref[...], g_ref[...], dxn_ref[...])
        dh = dho_ref[...] + dx
        dh_ref[...] = dh
        dg_ref[...] += jnp.sum(dg, axis=0, keepdims=True)
        if has_prev:
            da, dga = _rms_bwd(xa_ref[...], ga_ref[...], dh)
            da_ref[...] = da.astype(da_ref.dtype)
            dga_ref[...] += jnp.sum(dga, axis=0, keepdims=True)

    ins = [dh_out, dxn, x, g_pre] + (list(prev) if has_prev else [])
    in_specs = [_row_spec(tt, d)] * 3 + [_vec_spec(d)] + ([_row_spec(tt, d), _vec_spec(d)] if has_prev else [])
    out_specs = [_row_spec(tt, d), _vec_spec(d)] + ([_row_spec(tt, d), _vec_spec(d)] if has_prev else [])
    out_shape = [jax.ShapeDtypeStruct((t, d), F32), jax.ShapeDtypeStruct((1, d), F32)]
    if has_prev:
        out_shape += [jax.ShapeDtypeStruct((t, d), BF16), jax.ShapeDtypeStruct((1, d), F32)]
    return pl.pallas_call(
        body, name=name, grid=(t // tt,), in_specs=in_specs, out_specs=out_specs, out_shape=out_shape,
        compiler_params=_params("arbitrary"),
    )(*ins)


def _shift_down(x, k, halo):
    r = pltpu.roll(x, k, 0)
    rh = pltpu.roll(halo, k, 0)
    row = lax.broadcasted_iota(jnp.int32, rh.shape, 0)
    return jnp.concatenate([jnp.where(row < k, rh, r[:HALO]), r[HALO:]], axis=0)


def _shift_up(x_ext, k, n):
    return pltpu.roll(x_ext, x_ext.shape[0] - k, 0)[:n]


def _halo_specs(tt, width, nblk):
    per = tt // HALO
    prev = pl.BlockSpec((HALO, width), lambda i: (jnp.maximum(i * per - 1, 0), 0))
    nxt = pl.BlockSpec((HALO, width), lambda i: (jnp.minimum((i + 1) * per, nblk * per - 1), 0))
    return prev, nxt


def _conv3(x, halo, w0, w1, w2):
    return w0 * _shift_down(x, 2, halo) + w1 * _shift_down(x, 1, halo) + w2 * x


def _sc_fwd(p, wc, name):
    t, c3 = p.shape
    c = c3 // 3
    tt, cc = _tile(t, 256, 16), _tile(c, 128, LANES)
    nblk = t // tt
    prev_spec, _ = _halo_specs(tt, c3, nblk)

    def body(p_ref, pp_ref, w_ref, y_ref):
        first = pl.program_id(0) == 0
        for j in range(c // cc):
            cols = slice(j * cc, (j + 1) * cc)
            gc, hv = p_ref[:, c + j * cc:c + (j + 1) * cc], p_ref[:, 2 * c + j * cc:2 * c + (j + 1) * cc]
            uh = jnp.where(first, 0.0, pp_ref[:, c + j * cc:c + (j + 1) * cc] * pp_ref[:, 2 * c + j * cc:2 * c + (j + 1) * cc])
            cv = _conv3(gc * hv, uh, w_ref[0:1, cols], w_ref[1:2, cols], w_ref[2:3, cols])
            y_ref[:, cols] = (p_ref[:, cols] * cv).astype(y_ref.dtype)

    return pl.pallas_call(
        body, name=name, grid=(nblk,),
        in_specs=[_row_spec(tt, c3), prev_spec, pl.BlockSpec((3, c), lambda i: (0, 0))],
        out_specs=_row_spec(tt, c), out_shape=jax.ShapeDtypeStruct((t, c), BF16),
        compiler_params=_params("parallel"),
    )(p, p, wc)


def _sc_bwd(p, dy, wc, name):
    t, c3 = p.shape
    c = c3 // 3
    tt, cc = _tile(t, 256, 16), _tile(c, 128, LANES)
    nblk = t // tt
    p_prev, p_next = _halo_specs(tt, c3, nblk)
    _, dy_next = _halo_specs(tt, c, nblk)

    def body(p_ref, pp_ref, pn_ref, dy_ref, dyn_ref, w_ref, dp_ref, dw_ref):
        i = pl.program_id(0)
        first, last = i == 0, i == nblk - 1

        @pl.when(first)
        def _():
            dw_ref[...] = jnp.zeros_like(dw_ref)

        for j in range(c // cc):
            a, b, d = slice(j * cc, (j + 1) * cc), slice(c + j * cc, c + (j + 1) * cc), slice(2 * c + j * cc, 2 * c + (j + 1) * cc)
            w0, w1, w2 = w_ref[0:1, a], w_ref[1:2, a], w_ref[2:3, a]
            gb, gc, hv, dyv = p_ref[:, a], p_ref[:, b], p_ref[:, d], dy_ref[:, a]
            u = gc * hv
            uh = jnp.where(first, 0.0, pp_ref[:, b] * pp_ref[:, d])
            u2, u1 = _shift_down(u, 2, uh), _shift_down(u, 1, uh)
            cv = w0 * u2 + w1 * u1 + w2 * u
            dcv = dyv * gb
            dcv_ext = jnp.concatenate([dcv, jnp.where(last, 0.0, dyn_ref[:, a] * pn_ref[:, a])], axis=0)
            du = w2 * dcv + w1 * _shift_up(dcv_ext, 1, tt) + w0 * _shift_up(dcv_ext, 2, tt)
            dp_ref[:, a] = (dyv * cv).astype(dp_ref.dtype)
            dp_ref[:, b] = (du * hv).astype(dp_ref.dtype)
            dp_ref[:, d] = (du * gc).astype(dp_ref.dtype)
            dw_ref[0:1, a] += jnp.sum(dcv * u2, axis=0, keepdims=True)
            dw_ref[1:2, a] += jnp.sum(dcv * u1, axis=0, keepdims=True)
            dw_ref[2:3, a] += jnp.sum(dcv * u, axis=0, keepdims=True)

    return pl.pallas_call(
        body, name=name, grid=(nblk,),
        in_specs=[_row_spec(tt, c3), p_prev, p_next, _row_spec(tt, c), dy_next, pl.BlockSpec((3, c), lambda i: (0, 0))],
        out_specs=[_row_spec(tt, c3), pl.BlockSpec((HALO, c), lambda i: (0, 0))],
        out_shape=[jax.ShapeDtypeStruct((t, c3), BF16), jax.ShapeDtypeStruct((HALO, c), F32)],
        compiler_params=_params("arbitrary"),
    )(p, p, p, dy, dy, wc)


def _ffn_fwd(u, cw, cb, name):
    t, f2 = u.shape
    f = f2 // 2
    tt, cc = _tile(t, 128, 16), _tile(f, 128, LANES)
    nblk = t // tt
    prev_spec, _ = _halo_specs(tt, f2, nblk)

    def body(u_ref, up_ref, w_ref, b_ref, a_ref):
        first = pl.program_id(0) == 0

        def conv(cols):
            halo = jnp.where(first, 0.0, up_ref[:, cols])
            return _conv3(u_ref[:, cols], halo, w_ref[0:1, cols], w_ref[1:2, cols], w_ref[2:3, cols]) + b_ref[:, cols]

        for j in range(f // cc):
            hg, hv = conv(slice(j * cc, (j + 1) * cc)), conv(slice(f + j * cc, f + (j + 1) * cc))
            a_ref[:, j * cc:(j + 1) * cc] = (hg * jax.nn.sigmoid(hg) * hv).astype(a_ref.dtype)

    return pl.pallas_call(
        body, name=name, grid=(nblk,),
        in_specs=[_row_spec(tt, f2), prev_spec, pl.BlockSpec((3, f2), lambda i: (0, 0)), _vec_spec(f2)],
        out_specs=_row_spec(tt, f), out_shape=jax.ShapeDtypeStruct((t, f), BF16),
        compiler_params=_params("parallel"),
    )(u, u, cw, cb)


def _ffn_bwd(u, da, cw, cb, name):
    t, f2 = u.shape
    f = f2 // 2
    tt, cc = _tile(t, 128, 16), _tile(f, 128, LANES)
    nblk = t // tt
    u_prev, u_next = _halo_specs(tt, f2, nblk)
    _, da_next = _halo_specs(tt, f, nblk)

    def body(u_ref, up_ref, un_ref, da_ref, dan_ref, w_ref, b_ref, du_ref, dw_ref):
        i = pl.program_id(0)
        first, last = i == 0, i == nblk - 1

        @pl.when(first)
        def _():
            dw_ref[...] = jnp.zeros_like(dw_ref)

        keep = jnp.where(last, 0.0, 1.0)

        def ext(cols):
            x = jnp.concatenate([u_ref[:, cols], un_ref[:, cols]], axis=0)
            halo = jnp.where(first, 0.0, up_ref[:, cols])
            x2, x1 = _shift_down(x, 2, halo), _shift_down(x, 1, halo)
            h = w_ref[0:1, cols] * x2 + w_ref[1:2, cols] * x1 + w_ref[2:3, cols] * x + b_ref[:, cols]
            return x, x1, x2, h

        def back(cols, dh_ext, x, x1, x2):
            w0, w1, w2 = w_ref[0:1, cols], w_ref[1:2, cols], w_ref[2:3, cols]
            dh = dh_ext[:tt]
            du_ref[:, cols] = (w2 * dh + w1 * _shift_up(dh_ext, 1, tt) + w0 * _shift_up(dh_ext, 2, tt)).astype(du_ref.dtype)
            dw_ref[0:1, cols] += jnp.sum(dh * x2[:tt], axis=0, keepdims=True)
            dw_ref[1:2, cols] += jnp.sum(dh * x1[:tt], axis=0, keepdims=True)
            dw_ref[2:3, cols] += jnp.sum(dh * x[:tt], axis=0, keepdims=True)
            dw_ref[3:4, cols] += jnp.sum(dh, axis=0, keepdims=True)

        row = lax.broadcasted_iota(jnp.int32, (tt + HALO, 1), 0)
        for j in range(f // cc):
            gcols, vcols = slice(j * cc, (j + 1) * cc), slice(f + j * cc, f + (j + 1) * cc)
            xg, xg1, xg2, hg = ext(gcols)
            xv, xv1, xv2, hv = ext(vcols)
            da_ext = jnp.concatenate([da_ref[:, gcols], dan_ref[:, gcols]], axis=0)
            da_ext = jnp.where(row < tt, da_ext, da_ext * keep)
            sg = jax.nn.sigmoid(hg)
            back(gcols, da_ext * hv * (sg * (1.0 + hg * (1.0 - sg))), xg, xg1, xg2)
            back(vcols, da_ext * (hg * sg), xv, xv1, xv2)

    return pl.pallas_call(
        body, name=name, grid=(nblk,),
        in_specs=[_row_spec(tt, f2), u_prev, u_next, _row_spec(tt, f), da_next,
                  pl.BlockSpec((3, f2), lambda i: (0, 0)), _vec_spec(f2)],
        out_specs=[_row_spec(tt, f2), pl.BlockSpec((HALO, f2), lambda i: (0, 0))],
        out_shape=[jax.ShapeDtypeStruct((t, f2), BF16), jax.ShapeDtypeStruct((HALO, f2), F32)],
        compiler_params=_params("arbitrary"),
    )(u, u, u, da, da, cw, cb)


def _gelu(x):
    cdf = 0.5 * (1.0 + lax.erf(x * INV_SQRT2))
    return x * cdf, cdf + x * (jnp.exp(-0.5 * x * x) * INV_SQRT_2PI)


def _sgu_common(p_ref, lg_ref, lb_ref, ws_ref, c, ch, groups):
    u, du_dp = _gelu(p_ref[:, :c])
    v, dv_dp = _gelu(p_ref[:, c:])
    mu = jnp.mean(v, axis=-1, keepdims=True)
    vc = v - mu
    rstd = lax.rsqrt(jnp.mean(vc * vc, axis=-1, keepdims=True) + EPS)
    xhat = vc * rstd
    vn = (xhat * lg_ref[...] + lb_ref[...]).astype(BF16)
    tril = lax.broadcasted_iota(jnp.int32, (ch, ch), 0) >= lax.broadcasted_iota(jnp.int32, (ch, ch), 1)
    wm = [jnp.where(tril, ws_ref[g], 0.0).astype(BF16) for g in range(groups)]
    return u, du_dp, dv_dp, xhat, rstd, vn, wm, tril


def _lane_pick(x, g):
    lane = lax.broadcasted_iota(jnp.int32, x.shape, 1)
    return jnp.sum(jnp.where(lane == g, x, 0.0), axis=1, keepdims=True)


def _sgu_specs(tt, c, ch, groups):
    return [_row_spec(tt, 2 * c), _vec_spec(c), _vec_spec(c),
            pl.BlockSpec((groups, ch, ch), lambda i: (0, 0, 0)), pl.BlockSpec((ch, LANES), lambda i: (0, 0))]


def _sgu_fwd(p, lg, lb, ws, bs_t, name):
    t, c2 = p.shape
    c = c2 // 2
    groups, ch, _ = ws.shape
    gc = c // groups
    tt = _tile(t, 2 * ch, ch)

    def body(p_ref, lg_ref, lb_ref, ws_ref, bs_ref, y_ref):
        u, _, _, _, _, vn, wm, _ = _sgu_common(p_ref, lg_ref, lb_ref, ws_ref, c, ch, groups)
        bs = bs_ref[...]
        for r in range(tt // ch):
            rows = slice(r * ch, (r + 1) * ch)
            for g in range(groups):
                cols = slice(g * gc, (g + 1) * gc)
                mixed = _dot_nn(wm[g], vn[rows, cols]) + _lane_pick(bs, g)
                y_ref[rows, cols] = (u[rows, cols] * mixed).astype(y_ref.dtype)

    return pl.pallas_call(
        body, name=name, grid=(t // tt,), in_specs=_sgu_specs(tt, c, ch, groups),
        out_specs=_row_spec(tt, c), out_shape=jax.ShapeDtypeStruct((t, c), BF16),
        compiler_params=_params("parallel"),
    )(p, lg, lb, ws, bs_t)


def _sgu_bwd(p, dy, lg, lb, ws, bs_t, name):
    t, c2 = p.shape
    c = c2 // 2
    groups, ch, _ = ws.shape
    gc = c // groups
    tt = _tile(t, 2 * ch, ch)

    def body(p_ref, dy_ref, lg_ref, lb_ref, ws_ref, bs_ref, dp_ref, dws_ref, dbs_ref, dlg_ref, dlb_ref, dvn_ref):
        @pl.when(pl.program_id(0) == 0)
        def _():
            dws_ref[...] = jnp.zeros_like(dws_ref)
            dbs_ref[...] = jnp.zeros_like(dbs_ref)
            dlg_ref[...] = jnp.zeros_like(dlg_ref)
            dlb_ref[...] = jnp.zeros_like(dlb_ref)

        u, du_dp, dv_dp, xhat, rstd, vn, wm, tril = _sgu_common(p_ref, lg_ref, lb_ref, ws_ref, c, ch, groups)
        bs = bs_ref[...]
        lane = lax.broadcasted_iota(jnp.int32, (ch, LANES), 1)
        for r in range(tt // ch):
            rows = slice(r * ch, (r + 1) * ch)
            for g in range(groups):
                cols = slice(g * gc, (g + 1) * gc)
                dyv, vng = dy_ref[rows, cols], vn[rows, cols]
                mixed = _dot_nn(wm[g], vng) + _lane_pick(bs, g)
                dp_ref[rows, cols] = (dyv * mixed * du_dp[rows, cols]).astype(dp_ref.dtype)
                dmixed = dyv * u[rows, cols]
                dmb = dmixed.astype(BF16)
                dws_ref[g] += jnp.where(tril, _dot_nt(dmb, vng), 0.0)
                dbs_ref[...] += jnp.where(lane == g, jnp.sum(dmixed, axis=1, keepdims=True), 0.0)
                dvn_ref[rows, cols] = _dot_tn(wm[g], dmb)
        dvn = dvn_ref[...]
        dlg_ref[...] += jnp.sum(dvn * xhat, axis=0, keepdims=True)
        dlb_ref[...] += jnp.sum(dvn, axis=0, keepdims=True)
        dxh = dvn * lg_ref[...]
        dv = rstd * (dxh - jnp.mean(dxh, axis=-1, keepdims=True) - xhat * jnp.mean(dxh * xhat, axis=-1, keepdims=True))
        dp_ref[:, c:] = (dv * dv_dp).astype(dp_ref.dtype)

    specs = _sgu_specs(tt, c, ch, groups)
    return pl.pallas_call(
        body, name=name, grid=(t // tt,),
        in_specs=[specs[0], _row_spec(tt, c)] + specs[1:],
        out_specs=[_row_spec(tt, c2), pl.BlockSpec((groups, ch, ch), lambda i: (0, 0, 0)),
                   pl.BlockSpec((ch, LANES), lambda i: (0, 0)), _vec_spec(c), _vec_spec(c)],
        out_shape=[jax.ShapeDtypeStruct((t, c2), BF16), jax.ShapeDtypeStruct((groups, ch, ch), F32),
                   jax.ShapeDtypeStruct((ch, LANES), F32), jax.ShapeDtypeStruct((1, c), F32), jax.ShapeDtypeStruct((1, c), F32)],
        scratch_shapes=[pltpu.VMEM((tt, c), F32)],
        compiler_params=_params("arbitrary"),
    )(p, dy, lg, lb, ws, bs_t)


def _split(x):
    hi = x.astype(BF16)
    return hi, (x - hi.astype(F32)).astype(BF16)


def _sb_block(q, ks, qpos, kb, tk, scale, r_carry, tri):
    z = _dot_nt(q, ks) * scale
    kpos = kb * tk + lax.broadcasted_iota(jnp.int32, (1, tk), 1)
    mask = kpos < qpos
    e = jnp.exp(-jnp.abs(z))
    lm = jnp.where(mask, -(jnp.maximum(z, 0.0) + jnp.log(1.0 + e)), 0.0)
    hi, lo = _split(lm)
    inc = _dot_nn(hi, tri) + _dot_nn(lo, tri)
    att = jnp.where(mask, jnp.exp(z + lm + (inc - lm + r_carry)), 0.0)
    return z, mask, e, inc, att


def _key_blocks(i, step, carry):
    def guarded(_, c):
        return lax.cond(c[-1] > EXP_UNDERFLOW, step, lambda same: same, c)

    carry = lax.fori_loop(0, jnp.minimum(i + 1, 2), lambda _, c: step(c), carry)
    return lax.cond(carry[-1] > EXP_UNDERFLOW, lambda c: lax.fori_loop(2, i + 1, guarded, c), lambda same: same, carry)


def _suffix_ones(tk):
    return (lax.broadcasted_iota(jnp.int32, (tk, tk), 0) >= lax.broadcasted_iota(jnp.int32, (tk, tk), 1)).astype(BF16)


def _attn_fwd(qkv, name):
    t, d3 = qkv.shape
    d = d3 // 3
    heads = d // HEAD_DIM
    tq = _tile(t, 256, LANES)
    nq = t // tq
    scale = HEAD_DIM ** -0.5

    def body(q_ref, k_ref, v_ref, of_ref, ob_ref):
        i = pl.program_id(1)
        q = q_ref[...]
        qpos = i * tq + lax.broadcasted_iota(jnp.int32, (tq, 1), 0)
        tri = _suffix_ones(tq)

        def step(carry):
            n, o, r_carry, _ = carry
            kb = i - n
            rows = pl.ds(pl.multiple_of(kb * tq, tq), tq)
            _, _, _, inc, att = _sb_block(q, k_ref[rows, :], qpos, kb, tq, scale, r_carry, tri)
            ahi, alo = _split(att)
            vs = v_ref[rows, :]
            r_new = r_carry + inc[:, 0:1]
            return n + 1, o + _dot_nn(ahi, vs) + _dot_nn(alo, vs), r_new, jnp.max(r_new)

        init = (jnp.int32(0), jnp.zeros((tq, HEAD_DIM), F32), jnp.zeros((tq, 1), F32), jnp.float32(0.0))
        _, o, _, _ = _key_blocks(i, step, init)
        of_ref[...] = o
        ob_ref[...] = o.astype(ob_ref.dtype)

    return pl.pallas_call(
        body, name=name, grid=(heads, nq),
        in_specs=[pl.BlockSpec((tq, HEAD_DIM), lambda h, i: (i, h)),
                  pl.BlockSpec((t, HEAD_DIM), lambda h, i: (0, heads + h)),
                  pl.BlockSpec((t, HEAD_DIM), lambda h, i: (0, 2 * heads + h))],
        out_specs=[pl.BlockSpec((tq, HEAD_DIM), lambda h, i: (i, h))] * 2,
        out_shape=[jax.ShapeDtypeStruct((t, d), F32), jax.ShapeDtypeStruct((t, d), BF16)],
        compiler_params=_params("parallel", "parallel"),
    )(qkv, qkv, qkv)


def _attn_bwd(qkv, do, of, name):
    t, d3 = qkv.shape
    d = d3 // 3
    heads = d // HEAD_DIM
    tq = _tile(t, 256, LANES)
    nq = t // tq
    scale = HEAD_DIM ** -0.5

    def body(q_ref, k_ref, v_ref, do_ref, of_ref, dq_ref, dk_ref, dv_ref, dk_acc, dv_acc):
        i = pl.program_id(1)

        @pl.when(i == 0)
        def _():
            dk_acc[...] = jnp.zeros_like(dk_acc)
            dv_acc[...] = jnp.zeros_like(dv_acc)

        q, dov = q_ref[...], do_ref[...]
        delta = jnp.sum(dov.astype(F32) * of_ref[...], axis=-1, keepdims=True)
        qpos = i * tq + lax.broadcasted_iota(jnp.int32, (tq, 1), 0)
        tri = _suffix_ones(tq)

        def step(carry):
            n, dq, r_carry, g_carry, _ = carry
            kb = i - n
            rows = pl.ds(pl.multiple_of(kb * tq, tq), tq)
            ks, vs = k_ref[rows, :], v_ref[rows, :]
            z, mask, e, inc, att = _sb_block(q, ks, qpos, kb, tq, scale, r_carry, tri)
            g = _dot_nt(dov, vs) * att
            ghi, glo = _split(g)
            ginc = _dot_nn(ghi, tri) + _dot_nn(glo, tri)
            beta = jnp.where(z >= 0.0, 1.0, e) / (1.0 + e)
            dz = jnp.where(mask, g * (1.0 - beta) - (delta - g_carry - ginc) * beta, 0.0) * scale
            dzb = dz.astype(BF16)
            dk_acc[rows, :] += _dot_tn(dzb, q)
            dv_acc[rows, :] += _dot_tn(att.astype(BF16), dov)
            r_new = r_carry + inc[:, 0:1]
            return n + 1, dq + _dot_nn(dzb, ks), r_new, g_carry + ginc[:, 0:1], jnp.max(r_new)

        zero = jnp.zeros((tq, 1), F32)
        init = (jnp.int32(0), jnp.zeros((tq, HEAD_DIM), F32), zero, zero, jnp.float32(0.0))
        _, dq, _, _, _ = _key_blocks(i, step, init)
        dq_ref[...] = dq.astype(dq_ref.dtype)

        @pl.when(i == nq - 1)
        def _():
            dk_ref[...] = dk_acc[...].astype(dk_ref.dtype)
            dv_ref[...] = dv_acc[...].astype(dv_ref.dtype)

    blk = pl.BlockSpec((tq, HEAD_DIM), lambda h, i: (i, h))
    full = pl.BlockSpec((t, HEAD_DIM), lambda h, i: (0, h))
    return pl.pallas_call(
        body, name=name, grid=(heads, nq),
        in_specs=[blk, pl.BlockSpec((t, HEAD_DIM), lambda h, i: (0, heads + h)),
                  pl.BlockSpec((t, HEAD_DIM), lambda h, i: (0, 2 * heads + h)), blk, blk],
        out_specs=[blk, full, full],
        out_shape=[jax.ShapeDtypeStruct((t, d), BF16)] * 3,
        scratch_shapes=[pltpu.VMEM((t, HEAD_DIM), F32), pltpu.VMEM((t, HEAD_DIM), F32)],
        compiler_params=_params("parallel", "arbitrary"),
    )(qkv, qkv, qkv, do, of)


def _pair_add(g, pair, name):
    _, _, rh, c = g.shape
    tr = _tile(rh, 256, 16)
    core = lax.axis_index("c").astype(jnp.int32).reshape(1)

    def body(c_ref, g_ref, p_ref, o_ref):
        o_ref[...] = (g_ref[...].astype(F32) + p_ref[...].astype(F32)).astype(o_ref.dtype)

    return pl.pallas_call(
        body, name=name,
        grid_spec=pltpu.PrefetchScalarGridSpec(
            num_scalar_prefetch=1, grid=(4, rh // tr),
            in_specs=[pl.BlockSpec((None, None, tr, c), lambda j, r, cr: (j, cr[0], r, 0)),
                      pl.BlockSpec((None, tr, c), lambda j, r, cr: (j, r, 0))],
            out_specs=pl.BlockSpec((None, tr, c), lambda j, r, cr: (j, r, 0))),
        out_shape=jax.ShapeDtypeStruct((4, rh, c), BF16),
        compiler_params=_params("parallel", "parallel"),
    )(core, g, pair)


def _sum_slots(x, name):
    n, r, c = x.shape
    tr = _tile(r, 256, 16)

    def body(x_ref, o_ref):
        acc = x_ref[0].astype(F32)
        for k in range(1, n):
            acc = acc + x_ref[k].astype(F32)
        o_ref[...] = acc

    return pl.pallas_call(
        body, name=name, grid=(r // tr,),
        in_specs=[pl.BlockSpec((n, tr, c), lambda i: (0, i, 0))],
        out_specs=pl.BlockSpec((tr, c), lambda i: (i, 0)),
        out_shape=jax.ShapeDtypeStruct((r, c), F32), compiler_params=_params("parallel"),
    )(x)


def _sum_own_slots(sums, slots, name):
    _, rh, c = sums.shape
    tr = _tile(rh, 256, 16)
    chip = (2 * lax.axis_index("x") + lax.axis_index("y")).astype(jnp.int32).reshape(1)

    def body(chip_ref, own_ref, slot_ref, o_ref):
        acc = own_ref[...].astype(F32)
        for k in range(3):
            acc = acc + slot_ref[k].astype(F32)
        o_ref[...] = acc

    return pl.pallas_call(
        body, name=name,
        grid_spec=pltpu.PrefetchScalarGridSpec(
            num_scalar_prefetch=1, grid=(rh // tr,),
            in_specs=[pl.BlockSpec((None, tr, c), lambda r, me: (me[0], r, 0)),
                      pl.BlockSpec((3, tr, c), lambda r, me: (0, r, 0))],
            out_specs=pl.BlockSpec((tr, c), lambda r, me: (r, 0))),
        out_shape=jax.ShapeDtypeStruct((rh, c), F32), compiler_params=_params("parallel"),
    )(chip, sums, slots)


def _adam_update(w, g, m, v):
    mn = ADAM_B1 * m + (1.0 - ADAM_B1) * g
    vn = ADAM_B2 * v + (1.0 - ADAM_B2) * (g * g)
    c1, c2 = 1.0 - ADAM_B1 ** ADAM_STEP, 1.0 - ADAM_B2 ** ADAM_STEP
    return -ADAM_LR * ((mn / c1) / (jnp.sqrt(vn / c2) + ADAM_EPS) + ADAM_WD * w), mn, vn


def _adamw_halves(w, own, recv, m, v, name):
    layers, _, rh, c = w.shape
    tr = _tile(rh, max(8, (1 << 18) // c), 8)
    core = lax.axis_index("c").astype(jnp.int32).reshape(1)

    def body(c_ref, w_ref, own_ref, recv_ref, m_ref, v_ref, g_ref, d_ref, mo_ref, vo_ref):
        g = jnp.where(pl.program_id(1) == c_ref[0], own_ref[...], recv_ref[...])
        g_ref[...] = g
        d_ref[...], mo_ref[...], vo_ref[...] = _adam_update(w_ref[...], g, m_ref[...], v_ref[...])

    full = pl.BlockSpec((None, None, tr, c), lambda l, hf, r, cr: (l, hf, r, 0))
    half = pl.BlockSpec((None, tr, c), lambda l, hf, r, cr: (l, r, 0))
    return pl.pallas_call(
        body, name=name,
        grid_spec=pltpu.PrefetchScalarGridSpec(
            num_scalar_prefetch=1, grid=(layers, 2, rh // tr),
            in_specs=[full, half, half, full, full], out_specs=[full] * 4),
        out_shape=[jax.ShapeDtypeStruct(w.shape, F32)] * 4,
        compiler_params=_params("parallel", "parallel", "parallel"),
    )(core, w, own, recv, m, v)


def _adamw(w, g, m, v, name):
    r, c = w.shape
    tr = _tile(r, max(8, (1 << 18) // c), 8)

    def body(w_ref, g_ref, m_ref, v_ref, d_ref, mo_ref, vo_ref):
        d_ref[...], mo_ref[...], vo_ref[...] = _adam_update(w_ref[...], g_ref[...], m_ref[...], v_ref[...])

    spec = pl.BlockSpec((tr, c), lambda i: (i, 0))
    return pl.pallas_call(
        body, name=name, grid=(r // tr,), in_specs=[spec] * 4, out_specs=[spec] * 3,
        out_shape=[jax.ShapeDtypeStruct((r, c), F32)] * 3, compiler_params=_params("parallel"),
    )(w, g, m, v)


def _place():
    x, y, c = lax.axis_index("x"), lax.axis_index("y"), lax.axis_index("c")
    chips = [(1 - x, y), (x, 1 - y), (1 - x, 1 - y)]
    return x, y, c, chips


def _remote(src, dst, send_sems, recv_sems, k, to):
    return pltpu.make_async_remote_copy(src_ref=src, dst_ref=dst, send_sem=send_sems.at[k], recv_sem=recv_sems.at[k],
                                        device_id=to, device_id_type=MESH)


def _comm_call(body, name, ins, out_shape, n_remote, n_local=0):
    sems = [pltpu.SemaphoreType.DMA((n_remote,)), pltpu.SemaphoreType.DMA((n_remote,))]
    if n_local:
        sems.append(pltpu.SemaphoreType.DMA((n_local,)))
    return pl.pallas_call(
        body, name=name, in_specs=[ANY] * len(ins), out_specs=[ANY] * len(out_shape), out_shape=out_shape,
        scratch_shapes=sems,
    )(*ins)


def _all_gather(shards, name):
    n = len(shards)

    def body(*refs):
        ins, outs = refs[:n], refs[n:2 * n]
        send_sems, recv_sems = refs[2 * n:]
        x, y, c, chips = _place()
        me, sib = 2 * x + y, (x, y, 1 - c)
        started = []
        for a in range(n):
            cp = _remote(ins[a], outs[a].at[me], send_sems, recv_sems, 7 * a + 6, sib)
            cp.start()
            started.append(cp)
            for k, (px, py) in enumerate(chips):
                cp = _remote(ins[a].at[c], outs[a].at[me, c], send_sems, recv_sems, 7 * a + k, (px, py, c))
                cp.start()
                started.append(cp)
        for a in range(n):
            for k, (px, py) in enumerate(chips):
                land = outs[a].at[2 * px + py, c]
                _remote(land, land, send_sems, recv_sems, 7 * a + k, sib).wait_recv()
                cp = _remote(land, land, send_sems, recv_sems, 7 * a + 3 + k, sib)
                cp.start()
                started.append(cp)
        for a in range(n):
            for k, (px, py) in enumerate(chips):
                land = outs[a].at[2 * px + py, 1 - c]
                _remote(land, land, send_sems, recv_sems, 7 * a + 3 + k, sib).wait_recv()
            own = outs[a].at[me]
            _remote(own, own, send_sems, recv_sems, 7 * a + 6, sib).wait_recv()
        for cp in started:
            cp.wait_send()

    out_shape = [jax.ShapeDtypeStruct((4,) + s.shape, s.dtype) for s in shards]
    return _comm_call(body, name, shards, out_shape, 7 * n)


def _pair_exchange_side(grads):
    n = len(grads)

    def copies(ins, outs, send_sems, recv_sems):
        x, y, c, _ = _place()
        return [_remote(ins[a].at[j, 1 - c], outs[a].at[j], send_sems, recv_sems, 4 * a + j, (x, y, 1 - c))
                for a in range(n) for j in range(4)]

    return Side(grads, [jax.ShapeDtypeStruct((4,) + g.shape[2:], g.dtype) for g in grads], 4 * n, copies)


def _gather_sides(shards):
    n = len(shards)
    full_shape = [jax.ShapeDtypeStruct((4,) + s.shape, s.dtype) for s in shards]

    def first(sel):
        def ici(ins, outs, send_sems, recv_sems):
            x, y, c, chips = _place()
            return [_remote(ins[a].at[c], outs[a].at[2 * x + y, c], send_sems, recv_sems, 3 * a + k, (px, py, c))
                    for a in range(len(sel)) for k, (px, py) in enumerate(chips)]

        return Side([shards[a] for a in sel], [full_shape[a] for a in sel], 3 * len(sel), ici)

    def d2d(ins, outs, send_sems, recv_sems):
        x, y, c, chips = _place()
        sib, cps = (x, y, 1 - c), []
        for a in range(n):
            cps.append(_remote(ins[a], outs[a].at[2 * x + y], send_sems, recv_sems, 4 * a + 3, sib))
            for k, (px, py) in enumerate(chips):
                land = outs[a].at[2 * px + py, c]
                cps.append(_remote(land, land, send_sems, recv_sems, 4 * a + k, sib))
        return cps

    def second(partial):
        return Side(list(shards) + list(partial), full_shape, 4 * n, d2d, aliases={n + a: a for a in range(n)})

    return first, second


def _chip_exchange_side(sums):
    n = len(sums)

    def copies(ins, outs, send_sems, recv_sems):
        _, _, c, chips = _place()
        return [_remote(ins[a].at[2 * px + py], outs[a].at[k], send_sems, recv_sems, 3 * a + k, (px, py, c))
                for a in range(n) for k, (px, py) in enumerate(chips)]

    return Side(sums, [jax.ShapeDtypeStruct((3,) + s.shape[1:], s.dtype) for s in sums], 3 * n, copies)


def _halves_exchange(own, name):
    n = len(own)

    def body(*refs):
        ins, outs = refs[:n], refs[n:2 * n]
        send_sems, recv_sems = refs[2 * n:]
        x, y, c, _ = _place()
        started = [_remote(ins[a], outs[a], send_sems, recv_sems, a, (x, y, 1 - c)) for a in range(n)]
        for cp in started:
            cp.start()
        for cp in started:
            cp.wait_recv()
        for cp in started:
            cp.wait_send()

    return _comm_call(body, name, own, [jax.ShapeDtypeStruct(t.shape, t.dtype) for t in own], n)


def _broadcast_small(buf, name):
    def body(in_ref, out_ref, send_sems, recv_sems, loc_sems):
        x, y, c, _ = _place()
        me = 4 * x + 2 * y + c
        loc = pltpu.make_async_copy(in_ref, out_ref.at[me], loc_sems.at[0])
        loc.start()
        peers = [(x ^ (k >> 2 & 1), y ^ (k >> 1 & 1), c ^ (k & 1)) for k in range(1, 8)]
        started = []
        for k, to in enumerate(peers):
            cp = _remote(in_ref, out_ref.at[me], send_sems, recv_sems, k, to)
            cp.start()
            started.append(cp)
        for k, (px, py, pc) in enumerate(peers):
            land = out_ref.at[4 * px + 2 * py + pc]
            _remote(land, land, send_sems, recv_sems, k, (px, py, pc)).wait_recv()
        for cp in started:
            cp.wait_send()
        loc.wait()

    return _comm_call(body, name, [buf], [jax.ShapeDtypeStruct((8,) + buf.shape, buf.dtype)], 7, 1)[0]


def _pack(arrays):
    flat = jnp.concatenate([a.reshape(-1).astype(F32) for a in arrays])
    pad = -flat.shape[0] % (256 * LANES)
    return jnp.pad(flat, (0, pad)).reshape(-1, LANES)


def _unpack(buf, shapes):
    flat, out, at = buf.reshape(-1), [], 0
    for s in shapes:
        size = math.prod(s)
        out.append(flat[at:at + size].reshape(s))
        at += size
    return out


def _adamw_any(w, g, m, v, name):
    shape = w.shape
    two = (-1, shape[-1]) if w.ndim >= 2 else (1, -1)
    outs = _adamw(w.reshape(two), g.reshape(two), m.reshape(two), v.reshape(two), name)
    return [o.reshape(shape) for o in outs]


BIG = ("sc_w_in", "sc_w_out", "sg_w_in", "sg_w_out", "sb_w_qkv", "sb_w_out", "ffn_w_up", "ffn_w_down")
ROW_SHARDED = ("sc_w_out", "sg_w_out", "sb_w_out", "ffn_w_down")
MIXER_WEIGHTS = (("sc_w_in", "sc_w_out"), ("sg_w_in", "sg_w_out"), ("sb_w_qkv", "sb_w_out"))
WEIGHTS = ("norm_mix_pre", "norm_mix_post", "norm_ffn_pre", "norm_ffn_post", "sc_w_in", "sc_conv_w", "sc_w_out",
           "sg_w_in", "sg_ln_g", "sg_ln_b", "sg_w_s", "sg_b_s", "sg_w_out", "sb_w_qkv", "sb_w_out",
           "ffn_w_up", "ffn_conv_w", "ffn_conv_b", "ffn_w_down")


def _step(p):
    t, d = p["x"].shape[1:]
    x, target = p["x"].reshape(t, d), p["loss_target"].reshape(t, d)
    depth = p["norm_mix_pre"].shape[0]
    chip = 2 * lax.axis_index("x") + lax.axis_index("y")

    def layer_entities(i):
        name_in, name_out = MIXER_WEIGHTS[i % N_MIXERS]
        return [(name_in, i // N_MIXERS), (name_out, i // N_MIXERS), ("ffn_w_up", i), ("ffn_w_down", i)]

    def shard(e):
        s = p[e[0]][e[1]].astype(BF16)
        return s.reshape(2, s.shape[0] // 2, s.shape[1])

    def as_weight(e, g):
        rows, cols = p[e[0]].shape[1:]
        return g.reshape(1, 4 * rows, cols) if e[0] in ROW_SHARDED else g.reshape(4, rows, cols)

    ents = [e for i in range(depth) for e in layer_entities(i)]
    first_mixer, first_ffn = layer_entities(0)[:2], layer_entities(0)[2:]
    full = {e: as_weight(e, g) for e, g in zip(first_mixer, _all_gather([shard(e) for e in first_mixer], "gather_weights_0"))}

    conv_shapes = [p["sc_conv_w"].shape, p["ffn_conv_w"].shape]
    conv_all = _broadcast_small(_pack([p["sc_conv_w"], p["ffn_conv_w"]]), "gather_conv_taps")[0::2]
    sc_cw, ffn_cw = [jnp.moveaxis(jnp.stack([_unpack(conv_all[j], conv_shapes)[i] for j in range(4)]), 0, 2)
                     .reshape(s[0], s[1], 4 * s[2]) for i, s in enumerate(conv_shapes)]

    ws = p["sg_w_s"]
    groups, ch = ws.shape[1], ws.shape[2]
    bs_t = [jnp.pad(p["sg_b_s"][j].T, ((0, 0), (0, LANES - groups))) for j in range(ws.shape[0])]
    g1, g2, g3, g4 = [[p[k][i:i + 1] for i in range(depth)] for k in WEIGHTS[:4]]

    saved = []
    h, hn = x, _rms_fwd(x, g1[0], "rms_in")
    for i in range(depth):
        kind, j = i % N_MIXERS, i // N_MIXERS
        nxt = layer_entities(i + 1) if i < depth - 1 else []
        first, second = _gather_sides([shard(e) for e in nxt]) if nxt else (lambda sel: None, lambda partial: None)
        w_in, w_out = [full[name, j] for name in MIXER_WEIGHTS[kind]]
        own_first, own_second = _gather_sides([shard(e) for e in first_ffn]) if i == 0 else (first, None)
        pre, *part_a = _listed(_mm_nn(hn, w_in, BF16 if kind == 2 else F32, f"{('sc', 'sg', 'sb')[kind]}_in_{i}", own_first([0, 1])))
        extra = None
        if kind == 0:
            y = _sc_fwd(pre, sc_cw[j], f"sc_mix_{i}")
        elif kind == 1:
            y = _sgu_fwd(pre, p["sg_ln_g"][j:j + 1], p["sg_ln_b"][j:j + 1], ws[j], bs_t[j], f"sg_mix_{i}")
        else:
            extra, y = _attn_fwd(pre, f"sb_mix_{i}")
        if i == 0:
            m, *arrived = _listed(_mm_nn(y, w_out, F32, f"mix_out_{i}", own_second(part_a)))
            full.update({e: as_weight(e, g) for e, g in zip(first_ffn, arrived)})
            part_a = []
        else:
            m = _mm_nn(y, w_out, F32, f"mix_out_{i}")
        h1, fn = _norm_step(h, m, g2[i], g3[i], f"norm_mid_{i}")
        up, *part_b = _listed(_mm_nn(fn, full["ffn_w_up", i], F32, f"ffn_up_{i}", first([0, 1, 2, 3] if i == 0 else [2, 3])))
        act = _ffn_fwd(up, ffn_cw[i], p["ffn_conv_b"][i:i + 1], f"ffn_act_{i}")
        f, *gathered = _listed(_mm_nn(act, full["ffn_w_down", i], F32, f"ffn_down_{i}", second(part_a + part_b)))
        full.update({e: as_weight(e, g) for e, g in zip(nxt, gathered)})
        saved.append(dict(h=h, hn=hn, pre=pre, y=y, extra=extra, m=m, h1=h1, fn=fn, up=up, act=act, f=f,
                          w_in=w_in, w_out=w_out))
        if i < depth - 1:
            h, hn = _norm_step(h1, f, g4[i], g1[i + 1], f"norm_end_{i}")
    loss, dh, df, dg4 = _final_loss(h1, f, g4[depth - 1], target, "loss_head")

    gbig, gsm = {}, {k: [None] * depth for k in ("g1", "g2", "g3", "g4", "ffn_cw", "ffn_cb")}
    gsm["g4"][depth - 1] = dg4
    gsm["sc_cw"] = [None] * p["sc_conv_w"].shape[0]
    for k in ("sg_lg", "sg_lb", "sg_ws", "sg_bs"):
        gsm[k] = [None] * ws.shape[0]
    pending, sums, slots = [], {}, {}

    def parts(e):
        return gbig[e].reshape(4, 2, -1, gbig[e].shape[-1])

    def pair_side(entities):
        return _pair_exchange_side([parts(e) for e in entities]) if entities else None

    def pair_sums(entities, pair, tag):
        sums.update({e: _pair_add(parts(e), q, f"grad_pair_add_{tag}_{a}") for a, (e, q) in enumerate(zip(entities, pair))})

    def chip_side(entities):
        return _chip_exchange_side([sums[e] for e in entities]) if entities else None

    for i in reversed(range(depth)):
        s = saved[i]
        kind, j = i % N_MIXERS, i // N_MIXERS
        ffn_ents = [("ffn_w_up", i), ("ffn_w_down", i)]
        dact, *pair = _listed(_mm_nt(df, full["ffn_w_down", i], F32, f"d_ffn_act_{i}", pair_side(pending)))
        pair_sums(pending, pair, f"mix_{i + 1}")
        gbig["ffn_w_down", i], *landed = _listed(_mm_tn(s["act"], df, 1, f"g_ffn_down_{i}", chip_side(pending)))
        slots.update(zip(pending, landed))
        dup, dcw = _ffn_bwd(s["up"], dact, ffn_cw[i], p["ffn_conv_b"][i:i + 1], f"d_ffn_up_{i}")
        gsm["ffn_cw"][i], gsm["ffn_cb"][i] = dcw[0:3], dcw[3:4]
        dfn = _mm_nt(dup, full["ffn_w_up", i], F32, f"d_ffn_in_{i}")
        gbig["ffn_w_up", i] = _mm_tn(s["fn"], dup, 4, f"g_ffn_up_{i}")
        dh1, gsm["g3"][i], dm, gsm["g2"][i] = _norm_bwd_step(dh, dfn, s["h1"], g3[i], (s["m"], g2[i]), f"d_norm_mid_{i}")
        name_in, name_out = MIXER_WEIGHTS[kind]
        gbig[name_out, j], *pair = _listed(_mm_tn(s["y"], dm, 1, f"g_mix_out_{i}", pair_side(ffn_ents)))
        pair_sums(ffn_ents, pair, f"ffn_{i}")
        if kind == 0:
            dy = _mm_nt(dm, s["w_out"], F32, f"d_mix_y_{i}")
            dpre, dwc = _sc_bwd(s["pre"], dy, sc_cw[j], f"d_sc_mix_{i}")
            gsm["sc_cw"][j] = dwc[0:3]
        elif kind == 1:
            dy = _mm_nt(dm, s["w_out"], F32, f"d_mix_y_{i}")
            dpre, dws, dbs, dlg, dlb = _sgu_bwd(s["pre"], dy, p["sg_ln_g"][j:j + 1], p["sg_ln_b"][j:j + 1], ws[j], bs_t[j],
                                                f"d_sg_mix_{i}")
            gsm["sg_lg"][j], gsm["sg_lb"][j], gsm["sg_ws"][j], gsm["sg_bs"][j] = dlg, dlb, dws, dbs[:, :groups].T
        else:
            do = _mm_nt(dm, s["w_out"], BF16, f"d_mix_y_{i}")
            dpre = jnp.concatenate(_attn_bwd(s["pre"], do, s["extra"], f"d_sb_mix_{i}"), axis=1)
        dhn, *landed = _listed(_mm_nt(dpre, s["w_in"], F32, f"d_mix_in_{i}", chip_side([("ffn_w_up", i)])))
        slots["ffn_w_up", i] = landed[0]
        gbig[name_in, j], *landed = _listed(_mm_tn(s["hn"], dpre, 4, f"g_mix_in_{i}", chip_side([("ffn_w_down", i)])))
        slots["ffn_w_down", i] = landed[0]
        if i > 0:
            dh, gsm["g1"][i], df, gsm["g4"][i - 1] = _norm_bwd_step(dh1, dhn, s["h"], g1[i],
                                                                   (saved[i - 1]["f"], g4[i - 1]), f"d_norm_in_{i}")
        else:
            dx, gsm["g1"][0] = _norm_bwd_step(dh1, dhn, s["h"], g1[0], None, "d_norm_in_0")
        pending = [(name_in, j), (name_out, j)]
    pair_sums(pending, _side_call(pair_side(pending), "grad_pair_exchange_mix_0"), "mix_0")
    slots.update(zip(pending, _side_call(chip_side(pending), "grad_chip_exchange_mix_0")))

    halves = {e: _sum_own_slots(sums[e], slots[e], f"grad_chip_sum_{a}") for a, e in enumerate(ents)}
    own = [jnp.stack([halves[name, l] for l in range(p[name].shape[0])]) for name in BIG]
    recv = _halves_exchange(own, "grad_halves_exchange")
    grads, delta, new_m, new_v = {}, {}, {}, {}
    for name, mine, theirs in zip(BIG, own, recv):
        shape = p[name].shape
        view = (shape[0], 2, shape[1] // 2, shape[2])
        outs = _adamw_halves(p[name].reshape(view), mine, theirs, p["m_" + name].reshape(view), p["v_" + name].reshape(view),
                             f"adamw_{name}")
        grads[name], delta[name], new_m[name], new_v[name] = [o.reshape(shape) for o in outs]

    small = [jnp.concatenate(gsm[k]) for k in ("g1", "g2", "g3", "g4", "sg_lg", "sg_lb")] + [
        jnp.stack(gsm["sg_ws"]), jnp.stack(gsm["sg_bs"]),
        jnp.concatenate(gsm["ffn_cb"]), jnp.stack(gsm["sc_cw"]), jnp.stack(gsm["ffn_cw"])]
    small_shapes = [a.shape for a in small]
    total = _sum_slots(_broadcast_small(_pack(small), "small_grad_exchange"), "small_grad_sum")
    sm = _unpack(total, small_shapes)
    for k, name in enumerate(("norm_mix_pre", "norm_mix_post", "norm_ffn_pre", "norm_ffn_post", "sg_ln_g", "sg_ln_b")):
        grads[name] = sm[k].reshape(p[name].shape)
    grads["sg_w_s"], grads["sg_b_s"] = sm[6].reshape(ws.shape), sm[7].reshape(p["sg_b_s"].shape)
    grads["ffn_conv_b"] = sm[8].reshape(p["ffn_conv_b"].shape)
    for name, full_g in (("sc_conv_w", sm[9]), ("ffn_conv_w", sm[10])):
        n = p[name].shape[-1]
        grads[name] = lax.dynamic_slice_in_dim(full_g, chip * n, n, axis=2)

    for name in WEIGHTS:
        if name in BIG:
            continue
        delta[name], new_m[name], new_v[name] = _adamw_any(p[name], grads[name], p["m_" + name], p["v_" + name], f"adamw_{name}")

    loss = lax.psum(loss[0, 0], ("x", "y", "c"))
    return (loss, dx.reshape(p["x"].shape), *[grads[n] for n in WEIGHTS], *[delta[n] for n in WEIGHTS],
            *[new_m[n] for n in WEIGHTS], *[new_v[n] for n in WEIGHTS])


def kernel(x, norm_mix_pre, norm_mix_post, norm_ffn_pre, norm_ffn_post, sc_w_in, sc_conv_w, sc_w_out, sg_w_in, sg_ln_g, sg_ln_b, sg_w_s, sg_b_s, sg_w_out, sb_w_qkv, sb_w_out, ffn_w_up, ffn_conv_w, ffn_conv_b, ffn_w_down, loss_target, m_norm_mix_pre, m_norm_mix_post, m_norm_ffn_pre, m_norm_ffn_post, m_sc_w_in, m_sc_conv_w, m_sc_w_out, m_sg_w_in, m_sg_ln_g, m_sg_ln_b, m_sg_w_s, m_sg_b_s, m_sg_w_out, m_sb_w_qkv, m_sb_w_out, m_ffn_w_up, m_ffn_conv_w, m_ffn_conv_b, m_ffn_w_down, v_norm_mix_pre, v_norm_mix_post, v_norm_ffn_pre, v_norm_ffn_post, v_sc_w_in, v_sc_conv_w, v_sc_w_out, v_sg_w_in, v_sg_ln_g, v_sg_ln_b, v_sg_w_s, v_sg_b_s, v_sg_w_out, v_sb_w_qkv, v_sb_w_out, v_ffn_w_up, v_ffn_conv_w, v_ffn_conv_b, v_ffn_w_down):
    return _step(dict(locals()))
```
